```python
import jax, jax.numpy as jnp
from jax import lax
import numpy as np

D_MODEL = 1024
BATCH = 8
SEQ = 2048
DEPTH = 1

MEM_LEN = 256
EPS = 1e-6

HG_HEADS = 4
HG_FDIM = 128
HG_IDIM = 128
HG_WIDTH = HG_HEADS * HG_IDIM
GLA_HEADS = 4
GLA_KDIM = 64
GLA_VDIM = 128
GLA_WIDTH = GLA_HEADS * GLA_VDIM
GLA_GATE_RANK = 16
GLA_GATE_NORMALIZER = 16.0
MIX_WIDTH = HG_WIDTH + GLA_WIDTH
CHUNK = 64
IN_SPLITS = (HG_HEADS * HG_FDIM,
             HG_HEADS * HG_FDIM,
             HG_WIDTH,
             HG_WIDTH,
             GLA_HEADS * GLA_KDIM,
             GLA_HEADS * GLA_KDIM,
             GLA_WIDTH,
             GLA_WIDTH,
             GLA_GATE_RANK)
IN_WIDTH = sum(IN_SPLITS)

XA_HEADS = 4
XA_HDIM = D_MODEL // XA_HEADS

N_EXPERTS = 32
TOP_K = 4
D_EXPERT = D_MODEL
SWIGLU_LIMIT = 7.0
SWIGLU_ALPHA = 1.702
MOE_BLOCK = 256

kernel_name = "hybrid_hgrn2_gla_xattn_moe"


def rmsnorm(x, w):
    xf = x.astype(jnp.float32)
    y = xf * lax.rsqrt(jnp.mean(xf * xf, axis=-1, keepdims=True) + EPS)
    return (y * w.astype(jnp.float32)).astype(x.dtype)


def split_heads(a, n):
    b, t, _ = a.shape
    return a.reshape(b, t, n, -1).transpose(0, 2, 1, 3)


def chunked_gated_recurrence(q, k, v, g, scale):
    B, H, T, K = q.shape
    V = v.shape[-1]
    n = T // CHUNK

    def to_chunks(a):
        return a.reshape(B, H, n, CHUNK, a.shape[-1]).transpose(2, 0, 1, 3, 4)

    qc, kc, vc = to_chunks(q * scale), to_chunks(k), to_chunks(v)
    gc = to_chunks(g.astype(jnp.float32))
    causal = jnp.tril(jnp.ones((CHUNK, CHUNK), dtype=bool))[:, :, None]

    def step(S, inp):
        qb, kb, vb, gb = inp
        b = jnp.cumsum(gb, axis=2)
        diff = b[:, :, :, None, :] - b[:, :, None, :, :]
        decay = jnp.exp(jnp.where(causal, diff, -jnp.inf))
        scores = jnp.einsum('bhik,bhjk,bhijk->bhij', qb, kb, decay)
        o = (jnp.einsum('bhij,bhjv->bhiv', scores, vb)
             + jnp.einsum('bhik,bhkv->bhiv', qb * jnp.exp(b), S))
        b_last = b[:, :, -1:, :]
        S_new = (jnp.exp(b_last[:, :, 0, :])[..., None] * S
                 + jnp.einsum('bhjk,bhjv->bhkv', kb * jnp.exp(b_last - b), vb))
        return S_new, o

    S0 = jnp.zeros((B, H, K, V), jnp.float32)
    _, o = lax.scan(step, S0, (qc, kc, vc, gc))
    return o.transpose(1, 2, 0, 3, 4).reshape(B, H, T, V).astype(v.dtype)


def gated_head_norm(o, gate, w):
    b, h, t, vd = o.shape
    o = rmsnorm(o.transpose(0, 2, 1, 3), w)
    return (o * jax.nn.silu(gate.reshape(b, t, h, vd))).reshape(b, t, h * vd)


def token_mix(h, w_in, lb, hg_onorm_w, gla_w_gk2, gla_b_gk, gla_onorm_w, w_out):
    f32 = jnp.float32
    proj = h @ w_in
    idx = np.cumsum(IN_SPLITS)[:-1].tolist()
    (hq, hf, hi, hgate, gq, gk_, gv, ggate, glr) = jnp.split(proj, idx, axis=-1)

    f = lb + (1.0 - lb) * jax.nn.sigmoid(hf.astype(f32))
    hk = (1.0 - f).astype(h.dtype)
    hlogf = jnp.log(f)
    o_hg = chunked_gated_recurrence(split_heads(jax.nn.silu(hq), HG_HEADS),
                                    split_heads(hk, HG_HEADS),
                                    split_heads(hi, HG_HEADS),
                                    split_heads(hlogf, HG_HEADS), 1.0)
    y_hg = gated_head_norm(o_hg, hgate, hg_onorm_w)

    gdecay = jax.nn.log_sigmoid((glr @ gla_w_gk2 + gla_b_gk).astype(f32)) / GLA_GATE_NORMALIZER
    o_gla = chunked_gated_recurrence(split_heads(gq, GLA_HEADS),
                                     split_heads(gk_, GLA_HEADS),
                                     split_heads(gv, GLA_HEADS),
                                     split_heads(gdecay, GLA_HEADS), GLA_KDIM ** -0.5)
    y_gla = gated_head_norm(o_gla, ggate, gla_onorm_w)

    return jnp.concatenate([y_hg, y_gla], axis=-1) @ w_out


def cross_attend(h, m, w_xq, w_xkv, w_xo):
    b, t, d = h.shape
    q = (h @ w_xq).reshape(b, t, XA_HEADS, XA_HDIM)
    k, v = jnp.split(m @ w_xkv, 2, axis=-1)
    k = k.reshape(b, -1, XA_HEADS, XA_HDIM)
    v = v.reshape(b, -1, XA_HEADS, XA_HDIM)
    s = jnp.einsum('bthd,bmhd->bhtm', q, k).astype(jnp.float32) * (XA_HDIM ** -0.5)
    p = jax.nn.softmax(s, axis=-1).astype(v.dtype)
    o = jnp.einsum('bhtm,bmhd->bthd', p, v).reshape(b, t, d)
    return o @ w_xo


def moe(h, w_router, b_router, w_gate_up, b_gate_up, w_down, b_down):
    b, t, d = h.shape
    n_tok = b * t
    h2 = h.reshape(n_tok, d)
    logits = (h2 @ w_router + b_router).astype(jnp.float32)
    top_v, top_i = lax.top_k(logits, TOP_K)
    gates = jax.nn.softmax(top_v, axis=-1).astype(h.dtype)

    n_assign = n_tok * TOP_K
    flat_e = top_i.reshape(-1).astype(jnp.int32)
    flat_tok = jnp.arange(n_assign, dtype=jnp.int32) // TOP_K
    flat_g = gates.reshape(-1)
    order = jnp.argsort(flat_e, stable=True)
    se, stok, sg = flat_e[order], flat_tok[order], flat_g[order]

    counts = jnp.bincount(flat_e, length=N_EXPERTS).astype(jnp.int32)
    starts = jnp.cumsum(counts) - counts
    padded = (counts + MOE_BLOCK - 1) // MOE_BLOCK * MOE_BLOCK
    pends = jnp.cumsum(padded)
    pstarts = pends - padded
    dest = pstarts[se] + (jnp.arange(n_assign, dtype=jnp.int32) - starts[se])

    n_rows = -(-n_assign // MOE_BLOCK) * MOE_BLOCK + N_EXPERTS * MOE_BLOCK
    n_blocks = n_rows // MOE_BLOCK
    row_tok = jnp.zeros((n_rows,), jnp.int32).at[dest].set(stok)
    row_g = jnp.zeros((n_rows,), h.dtype).at[dest].set(sg)
    block_starts = jnp.arange(n_blocks, dtype=jnp.int32) * MOE_BLOCK
    block_e = jnp.clip(jnp.searchsorted(pends, block_starts, side='right'), 0, N_EXPERTS - 1)

    def expert_block(args):
        xb, e = args
        gu = xb @ w_gate_up[e] + b_gate_up[e]
        gate, up = gu[:, :D_EXPERT], gu[:, D_EXPERT:]
        gate = jnp.minimum(gate, SWIGLU_LIMIT)
        up = jnp.clip(up, -SWIGLU_LIMIT, SWIGLU_LIMIT)
        act = (up + 1.0) * gate * jax.nn.sigmoid(SWIGLU_ALPHA * gate)
        return act @ w_down[e] + b_down[e]

    xb = h2[row_tok].reshape(n_blocks, MOE_BLOCK, d)
    yb = lax.map(expert_block, (xb, block_e)).reshape(n_rows, d)
    y = jnp.zeros_like(h2).at[row_tok].add(yb * row_g[:, None])
    return y.reshape(b, t, d)


def setup_inputs(seed: int = 0) -> dict:
    key = jax.random.key(seed)
    ks = jax.random.split(key, 24)
    f32 = jnp.float32
    nrm = lambda k, shape, scale: jax.random.normal(k, shape, f32) * scale
    gain = lambda k, shape: 1.0 + 0.05 * jax.random.normal(k, shape, f32)
    L, D, E, F = DEPTH, D_MODEL, N_EXPERTS, D_EXPERT
    return {
        "x": nrm(ks[0], (BATCH, SEQ, D), 1.0),
        "mem": nrm(ks[1], (BATCH, MEM_LEN, D), 1.0),
        "norm_mix_w": gain(ks[2], (L, D)),
        "w_in": nrm(ks[3], (L, D, IN_WIDTH), D ** -0.5),
        "hg_lb_logits": nrm(ks[4], (L + 1, HG_HEADS * HG_FDIM), 0.1),
        "hg_onorm_w": gain(ks[5], (L, HG_IDIM)),
        "gla_w_gk2": nrm(ks[6], (L, GLA_GATE_RANK, GLA_HEADS * GLA_KDIM), GLA_GATE_RANK ** -0.5),
        "gla_b_gk": nrm(ks[7], (L, GLA_HEADS * GLA_KDIM), 0.1),
        "gla_onorm_w": gain(ks[8], (L, GLA_VDIM)),
        "w_out": nrm(ks[9], (L, MIX_WIDTH, D), MIX_WIDTH ** -0.5),
        "norm_xa_w": gain(ks[10], (L, D)),
        "norm_mem_w": gain(ks[11], (L, D)),
        "w_xq": nrm(ks[12], (L, D, D), D ** -0.5),
        "w_xkv": nrm(ks[13], (L, D, 2 * D), D ** -0.5),
        "w_xo": nrm(ks[14], (L, D, D), D ** -0.5),
        "norm_moe_w": gain(ks[15], (L, D)),
        "w_router": nrm(ks[16], (L, D, E), D ** -0.5),
        "b_router": nrm(ks[17], (L, E), 0.01),
        "w_gate_up": nrm(ks[18], (L, E, D, 2 * F), D ** -0.5),
        "b_gate_up": nrm(ks[19], (L, E, 2 * F), 0.01),
        "w_down": nrm(ks[20], (L, E, F, D), F ** -0.5),
        "b_down": nrm(ks[21], (L, E, D), 0.01),
        "norm_final_w": gain(ks[22], (D,)),
    }


def reference(x, mem, norm_mix_w, w_in, hg_lb_logits, hg_onorm_w, gla_w_gk2, gla_b_gk,
              gla_onorm_w, w_out, norm_xa_w, norm_mem_w, w_xq, w_xkv, w_xo, norm_moe_w,
              w_router, b_router, w_gate_up, b_gate_up, w_down, b_down, norm_final_w):
    lbs = jnp.cumsum(jax.nn.softmax(hg_lb_logits.astype(jnp.float32), axis=0), axis=0)
    for l in range(DEPTH):
        h = rmsnorm(x, norm_mix_w[l])
        x = x + token_mix(h, w_in[l], lbs[l], hg_onorm_w[l], gla_w_gk2[l], gla_b_gk[l],
                          gla_onorm_w[l], w_out[l])
        x = x + cross_attend(rmsnorm(x, norm_xa_w[l]), rmsnorm(mem, norm_mem_w[l]),
                             w_xq[l], w_xkv[l], w_xo[l])
        x = x + moe(rmsnorm(x, norm_moe_w[l]), w_router[l], b_router[l], w_gate_up[l],
                    b_gate_up[l], w_down[l], b_down[l])
    return rmsnorm(x, norm_final_w)
```

```python
import functools

import jax
import jax.numpy as jnp
from jax import lax
from jax.experimental import pallas as pl
from jax.experimental.pallas import tpu as pltpu

F32 = jnp.float32
BF16 = jnp.bfloat16
EPS = 1e-6

HEAD_W = 128
HG_HEADS = 4
GLA_HEADS = 4
GLA_KDIM = 64
N_HEADS = HG_HEADS + GLA_HEADS
GLA_RANK = 16
GLA_GATE_NORMALIZER = 16.0
CHUNK = 64
SUB = 16
XA_HEADS = 4
N_EXPERTS = 32
TOP_K = 4
SWIGLU_LIMIT = 7.0
SWIGLU_ALPHA = 1.702

LANES = 128
VMEM_LIMIT_BYTES = 56 * 1024 * 1024

MIX_ROWS = 256
XA_ROWS = 256
MOE_ROWS = 256
ROW_TILE = 256

_W = HG_HEADS * HEAD_W
COL_HQ, COL_HF, COL_HI, COL_HGATE = 0, _W, 2 * _W, 3 * _W
COL_GQ, COL_GK, COL_GV, COL_GGATE = 4 * _W, 5 * _W, 6 * _W, 7 * _W
COL_GLR = 8 * _W
PROJ_W = COL_GLR + LANES


def _rms(x, w):
    return x * lax.rsqrt(jnp.mean(x * x, axis=-1, keepdims=True) + EPS) * w


def _dot(a, b):
    return jnp.dot(a, b, preferred_element_type=F32)


def _dot_nt(a, b):
    return lax.dot_general(a, b, (((1,), (1,)), ((), ())), preferred_element_type=F32)


def _dot_tn(a, b):
    return lax.dot_general(a, b, (((0,), (0,)), ((), ())), preferred_element_type=F32)


def _head_chunk(q, k, v, b, st_ref, sub_blk, lane_blk, lane_id8, lane_id16, sub_id8, sub_id16):
    c = [None] + [b[SUB * i - 1:SUB * i] for i in range(1, CHUNK // SUB)]
    b_last = b[CHUNK - 1:CHUNK]
    c_full = jnp.concatenate(
        [jnp.zeros((SUB, HEAD_W), F32)] + [jnp.broadcast_to(ci, (SUB, HEAD_W)) for ci in c[1:]], axis=0)
    q_rel = (q * jnp.exp(b - c_full)).astype(BF16)
    q_abs = (q * jnp.exp(b)).astype(BF16)
    k_end = (k * jnp.exp(b_last - b)).astype(BF16)
    v16 = v.astype(BF16)

    k_stack = jnp.concatenate(
        [(k * jnp.exp(jnp.minimum(ci - b, 0.0))).astype(BF16) for ci in c[1:]], axis=0)
    r = _dot_nt(k_stack, q_rel)
    inter = jnp.where(lane_blk == 1, r[0:CHUNK],
                      jnp.where(lane_blk == 2, r[CHUNK:2 * CHUNK], r[2 * CHUNK:3 * CHUNK]))
    inter = jnp.where(sub_blk < lane_blk, inter, 0.0)

    blocks = []
    neg_inf = jnp.float32(-jnp.inf)
    for blk in range(CHUNK // SUB):
        s = SUB * blk
        b_blk, k_blk = b[s:s + SUB], k[s:s + SUB]
        top = jnp.zeros((8, CHUNK), F32)
        for i in range(8):
            row = s + i
            e = jnp.exp(jnp.where(sub_id8 <= i, b[row:row + 1] - b_blk[:8], neg_inf))
            p = jnp.sum(q[row:row + 1] * k_blk[:8] * e, axis=-1, keepdims=True)
            top = jnp.where(lane_id8 == row, p, top)
        both = jnp.concatenate([top, jnp.zeros((8, CHUNK), F32)], axis=0)
        for i in range(8, SUB):
            row = s + i
            e = jnp.exp(jnp.where(sub_id16 <= i, b[row:row + 1] - b_blk, neg_inf))
            p = jnp.sum(q[row:row + 1] * k_blk * e, axis=-1, keepdims=True)
            both = jnp.where(lane_id16 == row, p, both)
        blocks.append(both)
    scores_t = (jnp.concatenate(blocks, axis=0) + inter).astype(BF16)

    st = st_ref[...]
    o = _dot_tn(scores_t, v16) + _dot_nt(q_abs, st.astype(BF16))
    st_ref[...] = st * jnp.exp(b_last) + _dot_tn(v16, k_end)
    return o


def _mix_kernel(x_ref, nw_ref, win_ref, lbl_ref, hgw_ref, wgk_ref, bgk_ref, glw_ref, wout_ref,
                o_ref, proj_ref, oall_ref, st_ref):
    @pl.when(pl.program_id(1) == 0)
    def _():
        st_ref[...] = jnp.zeros_like(st_ref)

    x = x_ref[0]
    h = _rms(x, nw_ref[...]).astype(BF16)
    proj_ref[...] = _dot(h, win_ref[...])

    lbl = lbl_ref[...]
    e = jnp.exp(lbl - jnp.max(lbl, axis=0, keepdims=True))
    lb = e[0:1] / jnp.sum(e, axis=0, keepdims=True)

    ri = lax.broadcasted_iota(jnp.int32, (CHUNK, CHUNK), 0)
    ci = lax.broadcasted_iota(jnp.int32, (CHUNK, CHUNK), 1)
    tri = jnp.where(ri >= ci, 1.0, 0.0).astype(BF16)
    sub_blk, lane_blk = ri // SUB, ci // SUB
    lane_id8 = lax.broadcasted_iota(jnp.int32, (8, CHUNK), 1)
    lane_id16 = lax.broadcasted_iota(jnp.int32, (SUB, CHUNK), 1)
    sub_id8 = lax.broadcasted_iota(jnp.int32, (8, HEAD_W), 0)
    sub_id16 = lax.broadcasted_iota(jnp.int32, (SUB, HEAD_W), 0)

    def chunk(ck, carry):
        rows = pl.ds(pl.multiple_of(ck * CHUNK, CHUNK), CHUNK)
        f = lb + (1.0 - lb) * jax.nn.sigmoid(proj_ref[rows, COL_HF:COL_HF + _W])
        g_hg = jnp.log(f)
        k_hg = 1.0 - f
        z = _dot(proj_ref[rows, COL_GLR:COL_GLR + LANES].astype(BF16), wgk_ref[...]) + bgk_ref[...]
        g_gla = (jnp.minimum(z, 0.0) - jnp.log1p(jnp.exp(-jnp.abs(z)))) * (1.0 / GLA_GATE_NORMALIZER)
        g_all = jnp.concatenate([g_hg, g_gla], axis=-1)
        g_hi = g_all.astype(BF16)
        g_lo = (g_all - g_hi.astype(F32)).astype(BF16)
        b_all = _dot(tri, g_hi) + _dot(tri, g_lo)

        for hd in range(N_HEADS):
            lo = hd * HEAD_W
            if hd < HG_HEADS:
                hq = proj_ref[rows, COL_HQ + lo:COL_HQ + lo + HEAD_W]
                q = hq * jax.nn.sigmoid(hq)
                k = k_hg[:, lo:lo + HEAD_W]
                v = proj_ref[rows, COL_HI + lo:COL_HI + lo + HEAD_W]
            else:
                gl = lo - _W
                q = proj_ref[rows, COL_GQ + gl:COL_GQ + gl + HEAD_W] * (GLA_KDIM ** -0.5)
                k = proj_ref[rows, COL_GK + gl:COL_GK + gl + HEAD_W]
                v = proj_ref[rows, COL_GV + gl:COL_GV + gl + HEAD_W]
            o = _head_chunk(q, k, v, b_all[:, lo:lo + HEAD_W], st_ref.at[hd],
                            sub_blk, lane_blk, lane_id8, lane_id16, sub_id8, sub_id16)
            oall_ref[rows, lo:lo + HEAD_W] = o
        return carry

    lax.fori_loop(0, MIX_ROWS // CHUNK, chunk, 0)

    ys = []
    for hd in range(N_HEADS):
        lo = hd * HEAD_W
        o = oall_ref[:, lo:lo + HEAD_W]
        if hd < HG_HEADS:
            w, gate = hgw_ref[...], proj_ref[:, COL_HGATE + lo:COL_HGATE + lo + HEAD_W]
        else:
            w, gate = glw_ref[...], proj_ref[:, COL_GGATE + lo - _W:COL_GGATE + lo - _W + HEAD_W]
        ys.append((_rms(o, w) * (gate * jax.nn.sigmoid(gate))).astype(BF16))
    y = jnp.concatenate(ys, axis=-1)
    o_ref[0] = x + _dot(y, wout_ref[...])


def _token_mix(x, norm_w, w_in_p, lb_logits, hg_onorm_w, w_gk2_p, b_gk_p, gla_onorm_w, w_out):
    bsz, seq, d = x.shape
    const = lambda shape: pl.BlockSpec(shape, lambda b, t: (0,) * len(shape))
    return pl.pallas_call(
        _mix_kernel,
        grid=(bsz, seq // MIX_ROWS),
        in_specs=[
            pl.BlockSpec((1, MIX_ROWS, d), lambda b, t: (b, t, 0)),
            const((1, d)),
            const((d, PROJ_W)),
            const(lb_logits.shape),
            const((1, HEAD_W)),
            const((LANES, _W)),
            const((1, _W)),
            const((1, HEAD_W)),
            const((2 * _W, d)),
        ],
        out_specs=pl.BlockSpec((1, MIX_ROWS, d), lambda b, t: (b, t, 0)),
        out_shape=jax.ShapeDtypeStruct((bsz, seq, d), F32),
        scratch_shapes=[
            pltpu.VMEM((MIX_ROWS, PROJ_W), F32),
            pltpu.VMEM((MIX_ROWS, N_HEADS * HEAD_W), F32),
            pltpu.VMEM((N_HEADS, HEAD_W, HEAD_W), F32),
        ],
        compiler_params=pltpu.CompilerParams(
            dimension_semantics=("arbitrary", "arbitrary"), vmem_limit_bytes=VMEM_LIMIT_BYTES),
        name="token_mix",
    )(x, norm_w, w_in_p, lb_logits, hg_onorm_w, w_gk2_p, b_gk_p, gla_onorm_w, w_out)


def _xattn_kernel(x_ref, mem_ref, nxw_ref, nmw_ref, wq_ref, wkv_ref, wo_ref, nmoe_ref,
                  wr_hi_ref, wr_lo_ref, br_ref,
                  x2_ref, hm_ref, meta_ref, cnt_ref, k_scr, v_scr, run_scr):
    d = x_ref.shape[-1]
    hdim = d // XA_HEADS

    @pl.when(pl.program_id(1) == 0)
    def _():
        m = _rms(mem_ref[0], nmw_ref[...]).astype(BF16)
        kv = _dot(m, wkv_ref[...])
        k_scr[...] = kv[:, :d].astype(BF16)
        v_scr[...] = kv[:, d:].astype(BF16)

    @pl.when((pl.program_id(0) == 0) & (pl.program_id(1) == 0))
    def _():
        run_scr[...] = jnp.zeros_like(run_scr)

    x = x_ref[0]
    q = _dot(_rms(x, nxw_ref[...]).astype(BF16), wq_ref[...])
    outs = []
    for h in range(XA_HEADS):
        sl = slice(h * hdim, (h + 1) * hdim)
        s = _dot_nt(q[:, sl].astype(BF16), k_scr[:, sl]) * (hdim ** -0.5)
        p = jnp.exp(s - jnp.max(s, axis=-1, keepdims=True))
        p = p / jnp.sum(p, axis=-1, keepdims=True)
        outs.append(_dot(p.astype(BF16), v_scr[:, sl]).astype(BF16))
    x2 = x + _dot(jnp.concatenate(outs, axis=-1), wo_ref[...])
    x2_ref[0] = x2

    hm = _rms(x2, nmoe_ref[...])
    hm_ref[0] = hm

    hm_hi = hm.astype(BF16)
    hm_lo = (hm - hm_hi.astype(F32)).astype(BF16)
    logits = (_dot(hm_hi, wr_hi_ref[...]) + _dot(hm_lo, wr_hi_ref[...]) + _dot(hm_hi, wr_lo_ref[...])
              + br_ref[...])

    rows = logits.shape[0]
    lane = lax.broadcasted_iota(jnp.int32, (rows, LANES), 1)
    lane_f = lane.astype(F32)
    neg_inf = jnp.float32(-jnp.inf)
    top_v, top_i, hots = [], [], []
    work = logits
    for _ in range(TOP_K):
        m = jnp.max(work, axis=-1, keepdims=True)
        idx = jnp.min(jnp.where(work == m, lane_f, float(LANES)), axis=-1, keepdims=True)
        hot = lane_f == idx
        work = jnp.where(hot, neg_inf, work)
        top_v.append(m)
        top_i.append(idx)
        hots.append(hot)
    es = [jnp.exp(v - top_v[0]) for v in top_v]
    denom = es[0] + es[1] + es[2] + es[3]
    gates = [e / denom for e in es]

    chosen = jnp.where(hots[0] | hots[1] | hots[2] | hots[3], 1.0, 0.0)
    ri = lax.broadcasted_iota(jnp.int32, (rows, rows), 0)
    ci = lax.broadcasted_iota(jnp.int32, (rows, rows), 1)
    strict = jnp.where(ri > ci, 1.0, 0.0).astype(BF16)
    before = _dot(strict, chosen.astype(BF16)) + run_scr[0:1, :]
    ranks = [jnp.sum(jnp.where(hot, before, 0.0), axis=-1, keepdims=True) for hot in hots]
    run_scr[...] = run_scr[...] + jnp.sum(chosen, axis=0, keepdims=True)
    cnt_ref[...] = run_scr[...]

    meta = jnp.zeros((rows, LANES), F32)
    for j, col in enumerate(top_i + ranks + gates):
        meta = jnp.where(lane == j, col, meta)
    meta_ref[...] = meta


def _xattn_router(x1, mem, norm_xa_w, norm_mem_w, w_xq, w_xkv, w_xo, norm_moe_w, wr_hi, wr_lo, br_p):
    bsz, seq, d = x1.shape
    mlen = mem.shape[1]
    n_t = seq // XA_ROWS
    const = lambda shape: pl.BlockSpec(shape, lambda b, t: (0,) * len(shape))
    return pl.pallas_call(
        _xattn_kernel,
        grid=(bsz, n_t),
        in_specs=[
            pl.BlockSpec((1, XA_ROWS, d), lambda b, t: (b, t, 0)),
            pl.BlockSpec((1, mlen, d), lambda b, t: (b, 0, 0)),
            const((1, d)), const((1, d)),
            const((d, d)), const((d, 2 * d)), const((d, d)),
            const((1, d)),
            const((d, LANES)), const((d, LANES)), const((1, LANES)),
        ],
        out_specs=[
            pl.BlockSpec((1, XA_ROWS, d), lambda b, t: (b, t, 0)),
            pl.BlockSpec((1, XA_ROWS, d), lambda b, t: (b, t, 0)),
            pl.BlockSpec((XA_ROWS, LANES), lambda b, t: (b * n_t + t, 0)),
            pl.BlockSpec((8, LANES), lambda b, t: (0, 0)),
        ],
        out_shape=[
            jax.ShapeDtypeStruct((bsz, seq, d), F32),
            jax.ShapeDtypeStruct((bsz, seq, d), F32),
            jax.ShapeDtypeStruct((bsz * seq, LANES), F32),
            jax.ShapeDtypeStruct((8, LANES), F32),
        ],
        scratch_shapes=[
            pltpu.VMEM((mlen, d), BF16),
            pltpu.VMEM((mlen, d), BF16),
            pltpu.VMEM((8, LANES), F32),
        ],
        compiler_params=pltpu.CompilerParams(
            dimension_semantics=("arbitrary", "arbitrary"), vmem_limit_bytes=VMEM_LIMIT_BYTES),
        name="xattn_router",
    )(x1, mem, norm_xa_w, norm_mem_w, w_xq, w_xkv, w_xo, norm_moe_w, wr_hi, wr_lo, br_p)


def _row_copy(src_ref, src_row, dst_ref, dst_row, sem):
    return pltpu.make_async_copy(src_ref.at[pl.ds(src_row, 1)], dst_ref.at[pl.ds(dst_row, 1)], sem)


def _scatter_kernel(dest_ref, pad_ref, hm_ref, xs_ref, zero_ref, sem):
    @pl.when(pl.program_id(0) == 0)
    def _():
        zero_ref[...] = jnp.zeros_like(zero_ref)

        def per_expert(e, carry):
            n = pad_ref[1, e]
            off = pad_ref[0, e]

            def start(i, c):
                _row_copy(zero_ref, 0, xs_ref, off + i, sem).start()
                return c

            def wait(i, c):
                _row_copy(zero_ref, 0, xs_ref, 0, sem).wait()
                return c

            lax.fori_loop(0, n, start, 0)
            lax.fori_loop(0, n, wait, 0)
            return carry

        lax.fori_loop(0, N_EXPERTS, per_expert, 0)

    def issue(t, carry):
        for j in range(TOP_K):
            _row_copy(hm_ref, t, xs_ref, dest_ref[0, 0, t * TOP_K + j], sem).start()
        return carry

    lax.fori_loop(0, ROW_TILE, issue, 0)

    def drain(t, carry):
        for j in range(TOP_K):
            _row_copy(hm_ref, 0, xs_ref, 0, sem).wait()
        return carry

    lax.fori_loop(0, ROW_TILE, drain, 0)


def _scatter_rows(hm, dest3, pad_info, n_rows):
    n_tok, d = hm.shape
    return pl.pallas_call(
        _scatter_kernel,
        grid=(n_tok // ROW_TILE,),
        in_specs=[
            pl.BlockSpec((1, 1, ROW_TILE * TOP_K), lambda i: (i, 0, 0), memory_space=pltpu.SMEM),
            pl.BlockSpec(memory_space=pltpu.SMEM),
            pl.BlockSpec((ROW_TILE, d), lambda i: (i, 0)),
        ],
        out_specs=pl.BlockSpec(memory_space=pl.ANY),
        out_shape=jax.ShapeDtypeStruct((n_rows, d), F32),
        scratch_shapes=[pltpu.VMEM((8, d), F32), pltpu.SemaphoreType.DMA],
        compiler_params=pltpu.CompilerParams(
            dimension_semantics=("arbitrary",), vmem_limit_bytes=VMEM_LIMIT_BYTES),
        name="moe_scatter",
    )(dest3, pad_info, hm)


def _expert_kernel(be_ref, nu_ref, xs_ref, wgu_ref, bgu_ref, wd_ref, bd_ref, ys_ref):
    f = wd_ref.shape[1]

    @pl.when(pl.program_id(0) < nu_ref[0])
    def _():
        gu = _dot(xs_ref[...].astype(BF16), wgu_ref[0]) + bgu_ref[0]
        gate = jnp.minimum(gu[:, :f], SWIGLU_LIMIT)
        up = jnp.clip(gu[:, f:], -SWIGLU_LIMIT, SWIGLU_LIMIT)
        act = (up + 1.0) * gate * jax.nn.sigmoid(SWIGLU_ALPHA * gate)
        ys_ref[...] = _dot(act.astype(BF16), wd_ref[0]) + bd_ref[0]


def _expert_mlp(xs, block_e, n_used, w_gu, b_gu, w_d, b_d):
    n_rows, d = xs.shape
    n_blocks = n_rows // MOE_ROWS
    f = w_d.shape[1]
    row_map = lambda i, be, nu: (jnp.minimum(i, nu[0] - 1), 0)
    exp_map = lambda i, be, nu: (be[i], 0, 0)
    return pl.pallas_call(
        _expert_kernel,
        grid_spec=pltpu.PrefetchScalarGridSpec(
            num_scalar_prefetch=2,
            grid=(n_blocks,),
            in_specs=[
                pl.BlockSpec((MOE_ROWS, d), row_map),
                pl.BlockSpec((1, d, 2 * f), exp_map),
                pl.BlockSpec((1, 1, 2 * f), exp_map),
                pl.BlockSpec((1, f, d), exp_map),
                pl.BlockSpec((1, 1, d), exp_map),
            ],
            out_specs=pl.BlockSpec((MOE_ROWS, d), row_map),
        ),
        out_shape=jax.ShapeDtypeStruct((n_rows, d), F32),
        compiler_params=pltpu.CompilerParams(
            dimension_semantics=("arbitrary",), vmem_limit_bytes=VMEM_LIMIT_BYTES),
        name="moe_experts",
    )(block_e, n_used, xs, w_gu, b_gu, w_d, b_d)


def _combine_kernel(dest_ref, x2_ref, meta_ref, nfw_ref, ys_ref, out_ref, buf_ref, sem):
    def issue(t, carry):
        for j in range(TOP_K):
            _row_copy(ys_ref, dest_ref[0, 0, t * TOP_K + j], buf_ref.at[j], t, sem).start()
        return carry

    lax.fori_loop(0, ROW_TILE, issue, 0)

    def drain(t, carry):
        for j in range(TOP_K):
            _row_copy(ys_ref, 0, buf_ref.at[j], 0, sem).wait()
        return carry

    lax.fori_loop(0, ROW_TILE, drain, 0)

    acc = x2_ref[...]
    for j in range(TOP_K):
        col = 2 * TOP_K + j
        acc = acc + meta_ref[:, col:col + 1] * buf_ref[j]
    out_ref[...] = _rms(acc, nfw_ref[...])


def _combine(x2, meta, dest3, ys, norm_final_w):
    n_tok, d = x2.shape
    return pl.pallas_call(
        _combine_kernel,
        grid=(n_tok // ROW_TILE,),
        in_specs=[
            pl.BlockSpec((1, 1, ROW_TILE * TOP_K), lambda i: (i, 0, 0), memory_space=pltpu.SMEM),
            pl.BlockSpec((ROW_TILE, d), lambda i: (i, 0)),
            pl.BlockSpec((ROW_TILE, LANES), lambda i: (i, 0)),
            pl.BlockSpec((1, d), lambda i: (0, 0)),
            pl.BlockSpec(memory_space=pl.ANY),
        ],
        out_specs=pl.BlockSpec((ROW_TILE, d), lambda i: (i, 0)),
        out_shape=jax.ShapeDtypeStruct((n_tok, d), F32),
        scratch_shapes=[pltpu.VMEM((TOP_K, ROW_TILE, d), F32), pltpu.SemaphoreType.DMA],
        compiler_params=pltpu.CompilerParams(
            dimension_semantics=("arbitrary",), vmem_limit_bytes=VMEM_LIMIT_BYTES),
        name="moe_combine",
    )(dest3, x2, meta, norm_final_w, ys)


def _pad_heads(w, n_heads, width):
    lead = w.shape[:-1]
    w = w.reshape(lead + (n_heads, width))
    w = jnp.pad(w, [(0, 0)] * len(lead) + [(0, 0), (0, HEAD_W - width)])
    return w.reshape(lead + (n_heads * HEAD_W,))


def kernel(x, mem, norm_mix_w, w_in, hg_lb_logits, hg_onorm_w, gla_w_gk2, gla_b_gk, gla_onorm_w, w_out, norm_xa_w, norm_mem_w, w_xq, w_xkv, w_xo, norm_moe_w, w_router, b_router, w_gate_up, b_gate_up, w_down, b_down, norm_final_w):
    assert w_in.shape[0] == 1, "single-layer block"
    bsz, seq, d = x.shape
    n_tok = bsz * seq
    gk = GLA_HEADS * GLA_KDIM

    wi = w_in[0]
    w_in_p = jnp.concatenate([
        wi[:, :4 * _W],
        _pad_heads(wi[:, 4 * _W:4 * _W + gk], GLA_HEADS, GLA_KDIM),
        _pad_heads(wi[:, 4 * _W + gk:4 * _W + 2 * gk], GLA_HEADS, GLA_KDIM),
        wi[:, 4 * _W + 2 * gk:4 * _W + 2 * gk + 2 * _W],
        jnp.pad(wi[:, 4 * _W + 2 * gk + 2 * _W:], ((0, 0), (0, LANES - GLA_RANK))),
    ], axis=1).astype(BF16)
    w_gk2_p = jnp.pad(_pad_heads(gla_w_gk2[0], GLA_HEADS, GLA_KDIM),
                      ((0, LANES - GLA_RANK), (0, 0))).astype(BF16)
    b_gk_p = _pad_heads(gla_b_gk, GLA_HEADS, GLA_KDIM)
    wr = jnp.pad(w_router[0], ((0, 0), (0, LANES - N_EXPERTS)))
    wr_hi = wr.astype(BF16)
    wr_lo = (wr - wr_hi.astype(F32)).astype(BF16)
    br_p = jnp.pad(b_router, ((0, 0), (0, LANES - N_EXPERTS)), constant_values=-1e30)

    x1 = _token_mix(x, norm_mix_w, w_in_p, hg_lb_logits, hg_onorm_w, w_gk2_p, b_gk_p, gla_onorm_w,
                    w_out[0].astype(BF16))
    x2, hm, meta, cnt = _xattn_router(x1, mem, norm_xa_w, norm_mem_w, w_xq[0].astype(BF16),
                                      w_xkv[0].astype(BF16), w_xo[0].astype(BF16), norm_moe_w,
                                      wr_hi, wr_lo, br_p)

    eidx = meta[:, 0:TOP_K].astype(jnp.int32)
    rank = meta[:, TOP_K:2 * TOP_K].astype(jnp.int32)
    counts = cnt[0, :N_EXPERTS].astype(jnp.int32)
    padded = (counts + MOE_ROWS - 1) // MOE_ROWS * MOE_ROWS
    pends = jnp.cumsum(padded)
    pstarts = pends - padded
    hot = eidx[..., None] == jnp.arange(N_EXPERTS, dtype=jnp.int32)
    dest = jnp.sum(jnp.where(hot, pstarts, 0), axis=-1) + rank
    dest3 = dest.reshape(n_tok // ROW_TILE, 1, ROW_TILE * TOP_K)
    n_blocks = n_tok * TOP_K // MOE_ROWS + N_EXPERTS
    n_used = (pends[-1] // MOE_ROWS).astype(jnp.int32)
    blk = jnp.arange(n_blocks, dtype=jnp.int32)
    block_e = jnp.sum(blk[:, None] * MOE_ROWS >= pends[None, :], axis=-1).astype(jnp.int32)
    block_e = jnp.minimum(block_e, N_EXPERTS - 1)
    block_e = jnp.where(blk < n_used, block_e, block_e[jnp.maximum(n_used - 1, 0)])
    pad_info = jnp.stack([pstarts + counts, padded - counts]).astype(jnp.int32)

    xs = _scatter_rows(hm.reshape(n_tok, d), dest3, pad_info, n_blocks * MOE_ROWS)
    ys = _expert_mlp(xs, block_e, n_used.reshape(1), w_gate_up[0].astype(BF16), b_gate_up[0][:, None, :],
                     w_down[0].astype(BF16), b_down[0][:, None, :])
    out = _combine(x2.reshape(n_tok, d), meta, dest3, ys, norm_final_w[None, :])
    return out.reshape(bsz, seq, d)
```

```python
import functools

import jax
import jax.numpy as jnp
from jax import lax
from jax.experimental import pallas as pl
from jax.experimental.pallas import tpu as pltpu

F32 = jnp.float32
BF16 = jnp.bfloat16
EPS = 1e-6
LOG2E = 1.4426950408889634

HEAD_W = 128
HG_HEADS = 4
GLA_HEADS = 4
GLA_KDIM = 64
N_HEADS = HG_HEADS + GLA_HEADS
GLA_RANK = 16
GLA_GATE_NORMALIZER = 16.0
CHUNK = 64
XA_HEADS = 4
N_EXPERTS = 32
TOP_K = 4
SWIGLU_LIMIT = 7.0
SWIGLU_ALPHA = 1.702

LANES = 128
VMEM_LIMIT_BYTES = 56 * 1024 * 1024

MIX_ROWS = 256
TOK_TILE = 256
MOE_ROWS = 256
ROW_ALIGN = 8
STAGE_ROWS = 1280
MAX_PIECES = STAGE_ROWS // ROW_ALIGN

_W = HG_HEADS * HEAD_W
COL_HQ, COL_HF, COL_HI, COL_HGATE = 0, _W, 2 * _W, 3 * _W
COL_GQ, COL_GK, COL_GV, COL_GGATE = 4 * _W, 5 * _W, 6 * _W, 7 * _W
COL_GLR = 8 * _W
PROJ_W = COL_GLR + LANES


def _rms(x, w):
    return x * lax.rsqrt(jnp.mean(x * x, axis=-1, keepdims=True) + EPS) * w


def _dot(a, b):
    return jnp.dot(a, b, preferred_element_type=F32)


def _dot_nt(a, b):
    return lax.dot_general(a, b, (((1,), (1,)), ((), ())), preferred_element_type=F32)


def _dot_tn(a, b):
    return lax.dot_general(a, b, (((0,), (0,)), ((), ())), preferred_element_type=F32)


def _head_chunk(hd, q_scr, k_scr, b2_scr, v, st_ref, sc_ref, masks):
    sl = slice(hd * HEAD_W, (hd + 1) * HEAD_W)
    q, k, b2 = q_scr[:, sl], k_scr[:, sl], b2_scr[:, sl]

    for s in range(0, CHUNK, 8):
        q_blk, b_blk = q[s:s + 8], b2[s:s + 8]
        for j in range(s, s + 8):
            e = jnp.exp2(b_blk - b2_scr[j:j + 1, sl])
            sc_ref[s:s + 8, j:j + 1] = jnp.sum(q_blk * (k_scr[j:j + 1, sl] * e), axis=-1, keepdims=True)
    scores = jnp.where(masks[0], sc_ref[...], 0.0)

    for lvl, size in enumerate((16, 32, 64)):
        mids = [b2_scr[r:r + 1, sl] for r in range(size // 2 - 1, CHUNK, size)]
        mid = jnp.concatenate([jnp.broadcast_to(m, (size, HEAD_W)) for m in mids], axis=0)
        dist = b2 - mid
        e = jnp.exp2(jnp.minimum(dist, -dist))
        r = _dot_nt((q * e).astype(BF16), (k * e).astype(BF16))
        scores = jnp.where(masks[lvl + 1], r, scores)

    b2_last = b2_scr[CHUNK - 1:CHUNK, sl]
    q_abs = (q * jnp.exp2(b2)).astype(BF16)
    k_end = (k * jnp.exp2(b2_last - b2)).astype(BF16)
    v16 = v.astype(BF16)
    st = st_ref[...]
    o = _dot(scores.astype(BF16), v16) + _dot_nt(q_abs, st.astype(BF16))
    st_ref[...] = st * jnp.exp2(b2_last) + _dot_tn(v16, k_end)
    return o


def _mix_kernel(x_ref, nw_ref, win_ref, lbl_ref, hgw_ref, wgk_ref, bgk_ref, glw_ref, wout_ref,
                o_ref, proj_ref, oall_ref, st_ref, q_scr, k_scr, b2_scr, sc_scr):
    @pl.when(pl.program_id(1) == 0)
    def _():
        st_ref[...] = jnp.zeros_like(st_ref)
        sc_scr[...] = jnp.zeros_like(sc_scr)

    x = x_ref[0]
    h = _rms(x, nw_ref[...]).astype(BF16)
    proj_ref[...] = _dot(h, win_ref[...])

    lbl = lbl_ref[...]
    e = jnp.exp(lbl - jnp.max(lbl, axis=0, keepdims=True))
    lb = e[0:1] / jnp.sum(e, axis=0, keepdims=True)

    ri = lax.broadcasted_iota(jnp.int32, (CHUNK, CHUNK), 0)
    ci = lax.broadcasted_iota(jnp.int32, (CHUNK, CHUNK), 1)
    tri = jnp.where(ri >= ci, 1.0, 0.0).astype(BF16)
    masks = [(ri // 8 == ci // 8) & (ri >= ci)]
    for size in (16, 32, 64):
        masks.append((ri // size == ci // size) & (ri // (size // 2) > ci // (size // 2)))

    def chunk(ck, carry):
        rows = pl.ds(pl.multiple_of(ck * CHUNK, CHUNK), CHUNK)
        f = lb + (1.0 - lb) * jax.nn.sigmoid(proj_ref[rows, COL_HF:COL_HF + _W])
        g_hg = jnp.log(f)
        z = _dot(proj_ref[rows, COL_GLR:COL_GLR + LANES].astype(BF16), wgk_ref[...]) + bgk_ref[...]
        g_gla = (jnp.minimum(z, 0.0) - jnp.log1p(jnp.exp(-jnp.abs(z)))) * (1.0 / GLA_GATE_NORMALIZER)
        g_all = jnp.concatenate([g_hg, g_gla], axis=-1)
        g_hi = g_all.astype(BF16)
        g_lo = (g_all - g_hi.astype(F32)).astype(BF16)
        b2_scr[...] = (_dot(tri, g_hi) + _dot(tri, g_lo)) * LOG2E
        hq = proj_ref[rows, COL_HQ:COL_HQ + _W]
        q_scr[:, 0:_W] = hq * jax.nn.sigmoid(hq)
        q_scr[:, _W:2 * _W] = proj_ref[rows, COL_GQ:COL_GQ + _W] * (GLA_KDIM ** -0.5)
        k_scr[:, 0:_W] = 1.0 - f
        k_scr[:, _W:2 * _W] = proj_ref[rows, COL_GK:COL_GK + _W]

        for hd in range(N_HEADS):
            lo = hd * HEAD_W
            col_v = COL_HI + lo if hd < HG_HEADS else COL_GV + lo - _W
            o = _head_chunk(hd, q_scr, k_scr, b2_scr, proj_ref[rows, col_v:col_v + HEAD_W],
                            st_ref.at[hd], sc_scr.at[hd], masks)
            oall_ref[rows, lo:lo + HEAD_W] = o
        return carry

    lax.fori_loop(0, MIX_ROWS // CHUNK, chunk, 0)

    ys = []
    for hd in range(N_HEADS):
        lo = hd * HEAD_W
        o = oall_ref[:, lo:lo + HEAD_W]
        if hd < HG_HEADS:
            w, gate = hgw_ref[...], proj_ref[:, COL_HGATE + lo:COL_HGATE + lo + HEAD_W]
        else:
            w, gate = glw_ref[...], proj_ref[:, COL_GGATE + lo - _W:COL_GGATE + lo - _W + HEAD_W]
        ys.append((_rms(o, w) * (gate * jax.nn.sigmoid(gate))).astype(BF16))
    y = jnp.concatenate(ys, axis=-1)
    o_ref[0] = x + _dot(y, wout_ref[...])


def _token_mix(x, norm_w, w_in_p, lb_logits, hg_onorm_w, w_gk2_p, b_gk_p, gla_onorm_w, w_out):
    bsz, seq, d = x.shape
    const = lambda shape: pl.BlockSpec(shape, lambda b, t: (0,) * len(shape))
    return pl.pallas_call(
        _mix_kernel,
        grid=(bsz, seq // MIX_ROWS),
        in_specs=[
            pl.BlockSpec((1, MIX_ROWS, d), lambda b, t: (b, t, 0)),
            const((1, d)),
            const((d, PROJ_W)),
            const(lb_logits.shape),
            const((1, HEAD_W)),
            const((LANES, _W)),
            const((1, _W)),
            const((1, HEAD_W)),
            const((2 * _W, d)),
        ],
        out_specs=pl.BlockSpec((1, MIX_ROWS, d), lambda b, t: (b, t, 0)),
        out_shape=jax.ShapeDtypeStruct((bsz, seq, d), F32),
        scratch_shapes=[
            pltpu.VMEM((MIX_ROWS, PROJ_W), F32),
            pltpu.VMEM((MIX_ROWS, N_HEADS * HEAD_W), F32),
            pltpu.VMEM((N_HEADS, HEAD_W, HEAD_W), F32),
            pltpu.VMEM((CHUNK, N_HEADS * HEAD_W), F32),
            pltpu.VMEM((CHUNK, N_HEADS * HEAD_W), F32),
            pltpu.VMEM((CHUNK, N_HEADS * HEAD_W), F32),
            pltpu.VMEM((N_HEADS, CHUNK, CHUNK), F32),
        ],
        compiler_params=pltpu.CompilerParams(
            dimension_semantics=("arbitrary", "arbitrary"), vmem_limit_bytes=VMEM_LIMIT_BYTES),
        name="token_mix",
    )(x, norm_w, w_in_p, lb_logits, hg_onorm_w, w_gk2_p, b_gk_p, gla_onorm_w, w_out)


def _xattn_kernel(x_ref, mem_ref, nxw_ref, nmw_ref, wq_ref, wkv_ref, wo_ref, nmoe_ref,
                  wr_hi_ref, wr_lo_ref, br_ref,
                  x2_ref, hm_ref, meta_ref, cnt_ref, k_scr, v_scr):
    d = x_ref.shape[-1]
    hdim = d // XA_HEADS

    @pl.when(pl.program_id(1) == 0)
    def _():
        m = _rms(mem_ref[0], nmw_ref[...]).astype(BF16)
        kv = _dot(m, wkv_ref[...])
        k_scr[...] = kv[:, :d].astype(BF16)
        v_scr[...] = kv[:, d:].astype(BF16)

    x = x_ref[0]
    q = _dot(_rms(x, nxw_ref[...]).astype(BF16), wq_ref[...])
    outs = []
    for h in range(XA_HEADS):
        sl = slice(h * hdim, (h + 1) * hdim)
        s = _dot_nt(q[:, sl].astype(BF16), k_scr[:, sl]) * (hdim ** -0.5)
        p = jnp.exp(s - jnp.max(s, axis=-1, keepdims=True))
        p = p / jnp.sum(p, axis=-1, keepdims=True)
        outs.append(_dot(p.astype(BF16), v_scr[:, sl]).astype(BF16))
    x2 = x + _dot(jnp.concatenate(outs, axis=-1), wo_ref[...])
    x2_ref[0] = x2

    hm = _rms(x2, nmoe_ref[...])
    hm_hi = hm.astype(BF16)
    hm_ref[0] = hm_hi

    hm_lo = (hm - hm_hi.astype(F32)).astype(BF16)
    logits = (_dot(hm_hi, wr_hi_ref[...]) + _dot(hm_lo, wr_hi_ref[...]) + _dot(hm_hi, wr_lo_ref[...])
              + br_ref[...])

    rows = logits.shape[0]
    lane = lax.broadcasted_iota(jnp.int32, (rows, LANES), 1)
    lane_f = lane.astype(F32)
    neg_inf = jnp.float32(-jnp.inf)
    top_v, top_i, hots = [], [], []
    work = logits
    for _ in range(TOP_K):
        m = jnp.max(work, axis=-1, keepdims=True)
        idx = jnp.min(jnp.where(work == m, lane_f, float(LANES)), axis=-1, keepdims=True)
        hot = lane_f == idx
        work = jnp.where(hot, neg_inf, work)
        top_v.append(m)
        top_i.append(idx)
        hots.append(hot)
    es = [jnp.exp(v - top_v[0]) for v in top_v]
    denom = es[0] + es[1] + es[2] + es[3]
    gates = [e / denom for e in es]

    chosen = jnp.where(hots[0] | hots[1] | hots[2] | hots[3], 1.0, 0.0)
    ri = lax.broadcasted_iota(jnp.int32, (rows, rows), 0)
    ci = lax.broadcasted_iota(jnp.int32, (rows, rows), 1)
    strict = jnp.where(ri > ci, 1.0, 0.0).astype(BF16)
    before = _dot(strict, chosen.astype(BF16))
    ranks = [jnp.sum(jnp.where(hot, before, 0.0), axis=-1, keepdims=True) for hot in hots]
    cnt_ref[...] = jnp.broadcast_to(jnp.sum(chosen, axis=0, keepdims=True), cnt_ref.shape)

    meta = jnp.zeros((rows, LANES), F32)
    for j, col in enumerate(top_i + ranks + gates):
        meta = jnp.where(lane == j, col, meta)
    meta_ref[...] = meta


def _xattn_router(x1, mem, norm_xa_w, norm_mem_w, w_xq, w_xkv, w_xo, norm_moe_w, wr_hi, wr_lo, br_p):
    bsz, seq, d = x1.shape
    mlen = mem.shape[1]
    n_t = seq // TOK_TILE
    const = lambda shape: pl.BlockSpec(shape, lambda b, t: (0,) * len(shape))
    return pl.pallas_call(
        _xattn_kernel,
        grid=(bsz, n_t),
        in_specs=[
            pl.BlockSpec((1, TOK_TILE, d), lambda b, t: (b, t, 0)),
            pl.BlockSpec((1, mlen, d), lambda b, t: (b, 0, 0)),
            const((1, d)), const((1, d)),
            const((d, d)), const((d, 2 * d)), const((d, d)),
            const((1, d)),
            const((d, LANES)), const((d, LANES)), const((1, LANES)),
        ],
        out_specs=[
            pl.BlockSpec((1, TOK_TILE, d), lambda b, t: (b, t, 0)),
            pl.BlockSpec((1, TOK_TILE, d), lambda b, t: (b, t, 0)),
            pl.BlockSpec((TOK_TILE, LANES), lambda b, t: (b * n_t + t, 0)),
            pl.BlockSpec((8, LANES), lambda b, t: (b * n_t + t, 0)),
        ],
        out_shape=[
            jax.ShapeDtypeStruct((bsz, seq, d), F32),
            jax.ShapeDtypeStruct((bsz, seq, d), BF16),
            jax.ShapeDtypeStruct((bsz * seq, LANES), F32),
            jax.ShapeDtypeStruct((bsz * n_t * 8, LANES), F32),
        ],
        scratch_shapes=[
            pltpu.VMEM((mlen, d), BF16),
            pltpu.VMEM((mlen, d), BF16),
        ],
        compiler_params=pltpu.CompilerParams(
            dimension_semantics=("arbitrary", "arbitrary"), vmem_limit_bytes=VMEM_LIMIT_BYTES),
        name="xattn_router",
    )(x1, mem, norm_xa_w, norm_mem_w, w_xq, w_xkv, w_xo, norm_moe_w, wr_hi, wr_lo, br_p)


def _piece_copy(src_ref, src_row, dst_ref, dst_row, sem):
    return pltpu.make_async_copy(src_ref.at[pl.ds(pl.multiple_of(src_row, ROW_ALIGN), ROW_ALIGN)],
                                 dst_ref.at[pl.ds(pl.multiple_of(dst_row, ROW_ALIGN), ROW_ALIGN)], sem)


def _dispatch_kernel(pdst_ref, np_ref, tail_ref, tr_ref, g_ref, hm_ref, xs_ref, gs_ref,
                     stage_ref, gstage_ref, sem_x, sem_g):
    j = pl.program_id(0)

    @pl.when(j == 0)
    def _():
        stage_ref[0:ROW_ALIGN, :] = jnp.zeros((ROW_ALIGN, stage_ref.shape[1]), F32)
        gstage_ref[0:ROW_ALIGN, :] = jnp.zeros((ROW_ALIGN, LANES), F32)

        def per_expert(e, carry):
            n = tail_ref[1, e]
            off = tail_ref[0, e]

            def start(i, c):
                _piece_copy(stage_ref, 0, xs_ref, off + i * ROW_ALIGN, sem_x).start()
                _piece_copy(gstage_ref, 0, gs_ref, off + i * ROW_ALIGN, sem_g).start()
                return c

            def wait(i, c):
                _piece_copy(stage_ref, 0, xs_ref, 0, sem_x).wait()
                _piece_copy(gstage_ref, 0, gs_ref, 0, sem_g).wait()
                return c

            lax.fori_loop(0, n, start, 0)
            lax.fori_loop(0, n, wait, 0)
            return carry

        lax.fori_loop(0, N_EXPERTS, per_expert, 0)

    x16 = hm_ref[...]
    tr = tr_ref[0]
    g = g_ref[0]
    tile = x16.shape[0]
    for c in range(STAGE_ROWS // tile):
        rid = lax.broadcasted_iota(jnp.int32, (tile, tile), 0) + c * tile
        hot = [rid == tr[k:k + 1, :] for k in range(TOP_K)]
        sel = jnp.where(hot[0] | hot[1] | hot[2] | hot[3], 1.0, 0.0).astype(BF16)
        stage_ref[c * tile:(c + 1) * tile, :] = _dot(sel, x16)
        gsum = jnp.zeros((tile, 1), F32)
        for k in range(TOP_K):
            gsum = gsum + jnp.sum(jnp.where(hot[k], g[k:k + 1, :], 0.0), axis=-1, keepdims=True)
        gstage_ref[c * tile:(c + 1) * tile, :] = jnp.broadcast_to(gsum, (tile, LANES))

    n_pieces = np_ref[j]

    def start(q, c):
        _piece_copy(stage_ref, q * ROW_ALIGN, xs_ref, pdst_ref[0, 0, q], sem_x).start()
        _piece_copy(gstage_ref, q * ROW_ALIGN, gs_ref, pdst_ref[0, 0, q], sem_g).start()
        return c

    def wait(q, c):
        _piece_copy(stage_ref, 0, xs_ref, 0, sem_x).wait()
        _piece_copy(gstage_ref, 0, gs_ref, 0, sem_g).wait()
        return c

    lax.fori_loop(0, n_pieces, start, 0)
    lax.fori_loop(0, n_pieces, wait, 0)


def _dispatch(hm, pdst3, n_pieces, tail_info, tr_t, g_t, n_rows):
    n_tok, d = hm.shape
    n_tiles = n_tok // TOK_TILE
    return pl.pallas_call(
        _dispatch_kernel,
        grid=(n_tiles,),
        in_specs=[
            pl.BlockSpec((1, 1, MAX_PIECES), lambda j: (j, 0, 0), memory_space=pltpu.SMEM),
            pl.BlockSpec(memory_space=pltpu.SMEM),
            pl.BlockSpec(memory_space=pltpu.SMEM),
            pl.BlockSpec((1, 8, TOK_TILE), lambda j: (j, 0, 0)),
            pl.BlockSpec((1, 8, TOK_TILE), lambda j: (j, 0, 0)),
            pl.BlockSpec((TOK_TILE, d), lambda j: (j, 0)),
        ],
        out_specs=[pl.BlockSpec(memory_space=pl.ANY), pl.BlockSpec(memory_space=pl.ANY)],
        out_shape=[jax.ShapeDtypeStruct((n_rows, d), F32), jax.ShapeDtypeStruct((n_rows, LANES), F32)],
        scratch_shapes=[pltpu.VMEM((STAGE_ROWS, d), F32), pltpu.VMEM((STAGE_ROWS, LANES), F32),
                        pltpu.SemaphoreType.DMA, pltpu.SemaphoreType.DMA],
        compiler_params=pltpu.CompilerParams(
            dimension_semantics=("arbitrary",), vmem_limit_bytes=VMEM_LIMIT_BYTES),
        name="moe_dispatch",
    )(pdst3, n_pieces, tail_info, tr_t, g_t, hm)


def _expert_kernel(be_ref, nu_ref, xs_ref, gs_ref, wgu_ref, bgu_ref, wd_ref, bd_ref, ys_ref,
                   wgu16_ref, wd16_ref):
    i = pl.program_id(0)
    f = wd_ref.shape[1]

    @pl.when((i == 0) | (be_ref[i] != be_ref[jnp.maximum(i - 1, 0)]))
    def _():
        wgu16_ref[...] = wgu_ref[0].astype(BF16)
        wd16_ref[...] = wd_ref[0].astype(BF16)

    @pl.when(i < nu_ref[0])
    def _():
        gu = _dot(xs_ref[...].astype(BF16), wgu16_ref[...]) + bgu_ref[0]
        gate = jnp.minimum(gu[:, :f], SWIGLU_LIMIT)
        up = jnp.clip(gu[:, f:], -SWIGLU_LIMIT, SWIGLU_LIMIT)
        act = (up + 1.0) * gate * jax.nn.sigmoid(SWIGLU_ALPHA * gate)
        ys_ref[...] = (_dot(act.astype(BF16), wd16_ref[...]) + bd_ref[0]) * gs_ref[:, 0:1]


def _expert_mlp(xs, gs, block_e, n_used, w_gu, b_gu, w_d, b_d):
    n_rows, d = xs.shape
    n_blocks = n_rows // MOE_ROWS
    f = w_d.shape[1]
    row_map = lambda i, be, nu: (jnp.minimum(i, nu[0] - 1), 0)
    exp_map = lambda i, be, nu: (be[i], 0, 0)
    return pl.pallas_call(
        _expert_kernel,
        grid_spec=pltpu.PrefetchScalarGridSpec(
            num_scalar_prefetch=2,
            grid=(n_blocks,),
            in_specs=[
                pl.BlockSpec((MOE_ROWS, d), row_map),
                pl.BlockSpec((MOE_ROWS, LANES), row_map),
                pl.BlockSpec((1, d, 2 * f), exp_map),
                pl.BlockSpec((1, 1, 2 * f), exp_map),
                pl.BlockSpec((1, f, d), exp_map),
                pl.BlockSpec((1, 1, d), exp_map),
            ],
            out_specs=pl.BlockSpec((MOE_ROWS, d), row_map),
            scratch_shapes=[pltpu.VMEM((d, 2 * f), BF16), pltpu.VMEM((f, d), BF16)],
        ),
        out_shape=jax.ShapeDtypeStruct((n_rows, d), F32),
        compiler_params=pltpu.CompilerParams(
            dimension_semantics=("arbitrary",), vmem_limit_bytes=VMEM_LIMIT_BYTES),
        name="moe_experts",
    )(block_e, n_used, xs, gs, w_gu, b_gu, w_d, b_d)


def _combine_kernel(pdst_ref, np_ref, x2_ref, tr_ref, nfw_ref, ys_ref, out_ref, ybuf_ref, sem):
    j = pl.program_id(0)

    @pl.when(j == 0)
    def _():
        ybuf_ref[...] = jnp.zeros_like(ybuf_ref)

    n_pieces = np_ref[j]

    def start(q, c):
        _piece_copy(ys_ref, pdst_ref[0, 0, q], ybuf_ref, q * ROW_ALIGN, sem).start()
        return c

    def wait(q, c):
        _piece_copy(ys_ref, 0, ybuf_ref, 0, sem).wait()
        return c

    lax.fori_loop(0, n_pieces, start, 0)
    lax.fori_loop(0, n_pieces, wait, 0)

    tr = tr_ref[...]
    tile = tr.shape[0]
    acc = x2_ref[...]
    for c in range(STAGE_ROWS // tile):
        cid = lax.broadcasted_iota(jnp.int32, (tile, tile), 1) + c * tile
        hot = [cid == tr[:, k:k + 1] for k in range(TOP_K)]
        sel = jnp.where(hot[0] | hot[1] | hot[2] | hot[3], 1.0, 0.0).astype(BF16)
        acc = acc + _dot(sel, ybuf_ref[c * tile:(c + 1) * tile, :].astype(BF16))
    out_ref[...] = _rms(acc, nfw_ref[...])


def _combine(x2, tr, pdst3, n_pieces, ys, norm_final_w):
    n_tok, d = x2.shape
    return pl.pallas_call(
        _combine_kernel,
        grid=(n_tok // TOK_TILE,),
        in_specs=[
            pl.BlockSpec((1, 1, MAX_PIECES), lambda j: (j, 0, 0), memory_space=pltpu.SMEM),
            pl.BlockSpec(memory_space=pltpu.SMEM),
            pl.BlockSpec((TOK_TILE, d), lambda j: (j, 0)),
            pl.BlockSpec((TOK_TILE, TOP_K), lambda j: (j, 0)),
            pl.BlockSpec((1, d), lambda j: (0, 0)),
            pl.BlockSpec(memory_space=pl.ANY),
        ],
        out_specs=pl.BlockSpec((TOK_TILE, d), lambda j: (j, 0)),
        out_shape=jax.ShapeDtypeStruct((n_tok, d), F32),
        scratch_shapes=[pltpu.VMEM((STAGE_ROWS, d), F32), pltpu.SemaphoreType.DMA],
        compiler_params=pltpu.CompilerParams(
            dimension_semantics=("arbitrary",), vmem_limit_bytes=VMEM_LIMIT_BYTES),
        name="moe_combine",
    )(pdst3, n_pieces, x2, tr, norm_final_w, ys)


def _pad_heads(w, n_heads, width):
    lead = w.shape[:-1]
    w = w.reshape(lead + (n_heads, width))
    w = jnp.pad(w, [(0, 0)] * len(lead) + [(0, 0), (0, HEAD_W - width)])
    return w.reshape(lead + (n_heads * HEAD_W,))


def _round_up(x, m):
    return (x + m - 1) // m * m


def _routing_tables(meta, cnt, n_tiles):
    i32 = jnp.int32
    eidx = meta[:, 0:TOP_K].astype(i32)
    lrank = meta[:, TOP_K:2 * TOP_K].astype(i32)
    counts = cnt.reshape(n_tiles, 8, LANES)[:, 0, :N_EXPERTS].astype(i32)
    cnt8 = _round_up(counts, ROW_ALIGN)
    lend = jnp.cumsum(cnt8, axis=1)
    lstart = lend - cnt8
    tot = jnp.sum(cnt8, axis=0)
    padded = _round_up(tot, MOE_ROWS)
    pends = jnp.cumsum(padded)
    pstarts = pends - padded
    goff = pstarts[None, :] + jnp.cumsum(cnt8, axis=0) - cnt8

    hot = eidx.reshape(n_tiles, TOK_TILE, TOP_K, 1) == jnp.arange(N_EXPERTS, dtype=i32)
    tr = jnp.sum(jnp.where(hot, lstart[:, None, None, :], 0), axis=-1) + lrank.reshape(n_tiles, TOK_TILE, TOP_K)
    tr_t = jnp.concatenate([jnp.swapaxes(tr, 1, 2), jnp.full((n_tiles, 8 - TOP_K, TOK_TILE), -1, i32)], axis=1)
    g_t = jnp.concatenate([jnp.swapaxes(meta[:, 2 * TOP_K:3 * TOP_K].reshape(n_tiles, TOK_TILE, TOP_K), 1, 2),
                           jnp.zeros((n_tiles, 8 - TOP_K, TOK_TILE), F32)], axis=1)

    prow = jnp.arange(MAX_PIECES, dtype=i32) * ROW_ALIGN
    pe = jnp.sum(prow[None, :, None] >= lend[:, None, :], axis=-1)
    pe = jnp.minimum(pe, N_EXPERTS - 1)
    pick = pe[:, :, None] == jnp.arange(N_EXPERTS, dtype=i32)
    pdst = (jnp.sum(jnp.where(pick, (goff - lstart)[:, None, :], 0), axis=-1) + prow[None, :]).astype(i32)
    n_pieces = (lend[:, -1] // ROW_ALIGN).astype(i32)
    pdst = jnp.where(jnp.arange(MAX_PIECES, dtype=i32)[None, :] < n_pieces[:, None], pdst, 0)

    max_rows = n_tiles * (TOK_TILE * TOP_K + N_EXPERTS * (ROW_ALIGN - 1)) + N_EXPERTS * (MOE_ROWS - ROW_ALIGN)
    n_blocks = -(-max_rows // MOE_ROWS)
    n_used = (pends[-1] // MOE_ROWS).astype(i32)
    blk = jnp.arange(n_blocks, dtype=i32)
    block_e = jnp.minimum(jnp.sum(blk[:, None] * MOE_ROWS >= pends[None, :], axis=-1), N_EXPERTS - 1).astype(i32)
    block_e = jnp.where(blk < n_used, block_e, block_e[jnp.maximum(n_used - 1, 0)])
    tail_info = jnp.stack([pstarts + tot, (padded - tot) // ROW_ALIGN]).astype(i32)
    return (tr.reshape(-1, TOP_K), tr_t, g_t, pdst.reshape(n_tiles, 1, MAX_PIECES), n_pieces, tail_info,
            block_e, n_used.reshape(1), n_blocks * MOE_ROWS)


def kernel(x, mem, norm_mix_w, w_in, hg_lb_logits, hg_onorm_w, gla_w_gk2, gla_b_gk, gla_onorm_w, w_out, norm_xa_w, norm_mem_w, w_xq, w_xkv, w_xo, norm_moe_w, w_router, b_router, w_gate_up, b_gate_up, w_down, b_down, norm_final_w):
    assert w_in.shape[0] == 1, "single-layer block"
    bsz, seq, d = x.shape
    n_tok = bsz * seq
    gk = GLA_HEADS * GLA_KDIM

    wi = w_in[0]
    w_in_p = jnp.concatenate([
        wi[:, :4 * _W],
        _pad_heads(wi[:, 4 * _W:4 * _W + gk], GLA_HEADS, GLA_KDIM),
        _pad_heads(wi[:, 4 * _W + gk:4 * _W + 2 * gk], GLA_HEADS, GLA_KDIM),
        wi[:, 4 * _W + 2 * gk:4 * _W + 2 * gk + 2 * _W],
        jnp.pad(wi[:, 4 * _W + 2 * gk + 2 * _W:], ((0, 0), (0, LANES - GLA_RANK))),
    ], axis=1).astype(BF16)
    w_gk2_p = jnp.pad(_pad_heads(gla_w_gk2[0], GLA_HEADS, GLA_KDIM),
                      ((0, LANES - GLA_RANK), (0, 0))).astype(BF16)
    b_gk_p = _pad_heads(gla_b_gk, GLA_HEADS, GLA_KDIM)
    wr = jnp.pad(w_router[0], ((0, 0), (0, LANES - N_EXPERTS)))
    wr_hi = wr.astype(BF16)
    wr_lo = (wr - wr_hi.astype(F32)).astype(BF16)
    br_p = jnp.pad(b_router, ((0, 0), (0, LANES - N_EXPERTS)), constant_values=-1e30)

    x1 = _token_mix(x, norm_mix_w, w_in_p, hg_lb_logits, hg_onorm_w, w_gk2_p, b_gk_p, gla_onorm_w,
                    w_out[0].astype(BF16))
    x2, hm, meta, cnt = _xattn_router(x1, mem, norm_xa_w, norm_mem_w, w_xq[0].astype(BF16),
                                      w_xkv[0].astype(BF16), w_xo[0].astype(BF16), norm_moe_w,
                                      wr_hi, wr_lo, br_p)

    n_tiles = n_tok // TOK_TILE
    tr, tr_t, g_t, pdst3, n_pieces, tail_info, block_e, n_used, n_rows = _routing_tables(meta, cnt, n_tiles)

    xs, gs = _dispatch(hm.reshape(n_tok, d), pdst3, n_pieces, tail_info, tr_t, g_t, n_rows)
    ys = _expert_mlp(xs, gs, block_e, n_used, w_gate_up[0], b_gate_up[0][:, None, :],
                     w_down[0], b_down[0][:, None, :])
    out = _combine(x2.reshape(n_tok, d), tr, pdst3, n_pieces, ys, norm_final_w[None, :])
    return out.reshape(bsz, seq, d)
```

```python
import functools

import jax
import jax.numpy as jnp
from jax import lax
from jax.experimental import pallas as pl
from jax.experimental.pallas import tpu as pltpu

F32 = jnp.float32
BF16 = jnp.bfloat16
EPS = 1e-6
LOG2E = 1.4426950408889634

HEAD_W = 128
HG_HEADS = 4
GLA_HEADS = 4
GLA_KDIM = 64
N_HEADS = HG_HEADS + GLA_HEADS
GLA_RANK = 16
GLA_GATE_NORMALIZER = 16.0
CHUNK = 64
XA_HEADS = 4
N_EXPERTS = 32
TOP_K = 4
SWIGLU_LIMIT = 7.0
SWIGLU_ALPHA = 1.702

LANES = 128
VMEM_LIMIT_BYTES = 56 * 1024 * 1024

MIX_ROWS = 256
TOK_TILE = 256
MOE_ROWS = 256
ROW_ALIGN = 8
STAGE_ROWS = 1280
MAX_PIECES = STAGE_ROWS // ROW_ALIGN

_W = HG_HEADS * HEAD_W
COL_HQ, COL_HF, COL_HI, COL_HGATE = 0, _W, 2 * _W, 3 * _W
COL_GQ, COL_GK, COL_GV, COL_GGATE = 4 * _W, 5 * _W, 6 * _W, 7 * _W
COL_GLR = 8 * _W
PROJ_W = COL_GLR + LANES


def _rms(x, w):
    return x * lax.rsqrt(jnp.mean(x * x, axis=-1, keepdims=True) + EPS) * w


def _dot(a, b):
    return jnp.dot(a, b, preferred_element_type=F32)


def _dot_nt(a, b):
    return lax.dot_general(a, b, (((1,), (1,)), ((), ())), preferred_element_type=F32)


def _dot_tn(a, b):
    return lax.dot_general(a, b, (((0,), (0,)), ((), ())), preferred_element_type=F32)


def _head_chunk(hd, q_scr, k_scr, b2_scr, v, st_ref, sc_ref, masks):
    sl = slice(hd * HEAD_W, (hd + 1) * HEAD_W)
    q, k, b2 = q_scr[:, sl], k_scr[:, sl], b2_scr[:, sl]

    for s in range(0, CHUNK, 8):
        q_blk, b_blk = q[s:s + 8], b2[s:s + 8]
        for j in range(s, s + 8):
            e = jnp.exp2(b_blk - b2_scr[j:j + 1, sl])
            sc_ref[s:s + 8, j:j + 1] = jnp.sum(q_blk * (k_scr[j:j + 1, sl] * e), axis=-1, keepdims=True)
    scores = jnp.where(masks[0], sc_ref[...], 0.0)

    for lvl, size in enumerate((16, 32, 64)):
        mids = [b2_scr[r:r + 1, sl] for r in range(size // 2 - 1, CHUNK, size)]
        mid = jnp.concatenate([jnp.broadcast_to(m, (size, HEAD_W)) for m in mids], axis=0)
        dist = b2 - mid
        e = jnp.exp2(jnp.minimum(dist, -dist))
        r = _dot_nt((q * e).astype(BF16), (k * e).astype(BF16))
        scores = jnp.where(masks[lvl + 1], r, scores)

    b2_last = b2_scr[CHUNK - 1:CHUNK, sl]
    q_abs = (q * jnp.exp2(b2)).astype(BF16)
    k_end = (k * jnp.exp2(b2_last - b2)).astype(BF16)
    v16 = v.astype(BF16)
    st = st_ref[...]
    o = _dot(scores.astype(BF16), v16) + _dot_nt(q_abs, st.astype(BF16))
    st_ref[...] = st * jnp.exp2(b2_last) + _dot_tn(v16, k_end)
    return o


def _mix_kernel(x_ref, nw_ref, win_ref, lbl_ref, hgw_ref, wgk_ref, bgk_ref, glw_ref, wout_ref,
                o_ref, proj_ref, oall_ref, st_ref, q_scr, k_scr, b2_scr, sc_scr):
    @pl.when(pl.program_id(1) == 0)
    def _():
        st_ref[...] = jnp.zeros_like(st_ref)
        sc_scr[...] = jnp.zeros_like(sc_scr)

    x = x_ref[0]
    h = _rms(x, nw_ref[...]).astype(BF16)
    proj_ref[...] = _dot(h, win_ref[...])

    lbl = lbl_ref[...]
    e = jnp.exp(lbl - jnp.max(lbl, axis=0, keepdims=True))
    lb = e[0:1] / jnp.sum(e, axis=0, keepdims=True)

    ri = lax.broadcasted_iota(jnp.int32, (CHUNK, CHUNK), 0)
    ci = lax.broadcasted_iota(jnp.int32, (CHUNK, CHUNK), 1)
    tri = jnp.where(ri >= ci, 1.0, 0.0).astype(BF16)
    masks = [(ri // 8 == ci // 8) & (ri >= ci)]
    for size in (16, 32, 64):
        masks.append((ri // size == ci // size) & (ri // (size // 2) > ci // (size // 2)))

    def chunk(ck):
        rows = pl.ds(ck * CHUNK, CHUNK)
        f = lb + (1.0 - lb) * jax.nn.sigmoid(proj_ref[rows, COL_HF:COL_HF + _W])
        g_hg = jnp.log(f)
        z = _dot(proj_ref[rows, COL_GLR:COL_GLR + LANES].astype(BF16), wgk_ref[...]) + bgk_ref[...]
        g_gla = (jnp.minimum(z, 0.0) - jnp.log1p(jnp.exp(-jnp.abs(z)))) * (1.0 / GLA_GATE_NORMALIZER)
        g_all = jnp.concatenate([g_hg, g_gla], axis=-1)
        g_hi = g_all.astype(BF16)
        g_lo = (g_all - g_hi.astype(F32)).astype(BF16)
        b2_scr[...] = (_dot(tri, g_hi) + _dot(tri, g_lo)) * LOG2E
        hq = proj_ref[rows, COL_HQ:COL_HQ + _W]
        q_scr[:, 0:_W] = hq * jax.nn.sigmoid(hq)
        q_scr[:, _W:2 * _W] = proj_ref[rows, COL_GQ:COL_GQ + _W] * (GLA_KDIM ** -0.5)
        k_scr[:, 0:_W] = 1.0 - f
        k_scr[:, _W:2 * _W] = proj_ref[rows, COL_GK:COL_GK + _W]

        for hd in range(N_HEADS):
            lo = hd * HEAD_W
            col_v = COL_HI + lo if hd < HG_HEADS else COL_GV + lo - _W
            o = _head_chunk(hd, q_scr, k_scr, b2_scr, proj_ref[rows, col_v:col_v + HEAD_W],
                            st_ref.at[hd], sc_scr.at[hd], masks)
            oall_ref[rows, lo:lo + HEAD_W] = o

    for ck in range(MIX_ROWS // CHUNK):
        chunk(ck)

    ys = []
    for hd in range(N_HEADS):
        lo = hd * HEAD_W
        o = oall_ref[:, lo:lo + HEAD_W]
        if hd < HG_HEADS:
            w, gate = hgw_ref[...], proj_ref[:, COL_HGATE + lo:COL_HGATE + lo + HEAD_W]
        else:
            w, gate = glw_ref[...], proj_ref[:, COL_GGATE + lo - _W:COL_GGATE + lo - _W + HEAD_W]
        ys.append((_rms(o, w) * (gate * jax.nn.sigmoid(gate))).astype(BF16))
    y = jnp.concatenate(ys, axis=-1)
    o_ref[0] = x + _dot(y, wout_ref[...])


def _token_mix(x, norm_w, w_in_p, lb_logits, hg_onorm_w, w_gk2_p, b_gk_p, gla_onorm_w, w_out):
    bsz, seq, d = x.shape
    const = lambda shape: pl.BlockSpec(shape, lambda b, t: (0,) * len(shape))
    return pl.pallas_call(
        _mix_kernel,
        grid=(bsz, seq // MIX_ROWS),
        in_specs=[
            pl.BlockSpec((1, MIX_ROWS, d), lambda b, t: (b, t, 0)),
            const((1, d)),
            const((d, PROJ_W)),
            const(lb_logits.shape),
            const((1, HEAD_W)),
            const((LANES, _W)),
            const((1, _W)),
            const((1, HEAD_W)),
            const((2 * _W, d)),
        ],
        out_specs=pl.BlockSpec((1, MIX_ROWS, d), lambda b, t: (b, t, 0)),
        out_shape=jax.ShapeDtypeStruct((bsz, seq, d), F32),
        scratch_shapes=[
            pltpu.VMEM((MIX_ROWS, PROJ_W), F32),
            pltpu.VMEM((MIX_ROWS, N_HEADS * HEAD_W), F32),
            pltpu.VMEM((N_HEADS, HEAD_W, HEAD_W), F32),
            pltpu.VMEM((CHUNK, N_HEADS * HEAD_W), F32),
            pltpu.VMEM((CHUNK, N_HEADS * HEAD_W), F32),
            pltpu.VMEM((CHUNK, N_HEADS * HEAD_W), F32),
            pltpu.VMEM((N_HEADS, CHUNK, CHUNK), F32),
        ],
        compiler_params=pltpu.CompilerParams(
            dimension_semantics=("arbitrary", "arbitrary"), vmem_limit_bytes=VMEM_LIMIT_BYTES),
        name="token_mix",
    )(x, norm_w, w_in_p, lb_logits, hg_onorm_w, w_gk2_p, b_gk_p, gla_onorm_w, w_out)


def _xattn_kernel(x_ref, mem_ref, nxw_ref, nmw_ref, wq_ref, wkv_ref, wo_ref, nmoe_ref,
                  wr_hi_ref, wr_lo_ref, br_ref,
                  x2_ref, hm_ref, meta_ref, cnt_ref, k_scr, v_scr):
    d = x_ref.shape[-1]
    hdim = d // XA_HEADS

    @pl.when(pl.program_id(1) == 0)
    def _():
        m = _rms(mem_ref[0], nmw_ref[...]).astype(BF16)
        kv = _dot(m, wkv_ref[...])
        k_scr[...] = kv[:, :d].astype(BF16)
        v_scr[...] = kv[:, d:].astype(BF16)

    x = x_ref[0]
    q = _dot(_rms(x, nxw_ref[...]).astype(BF16), wq_ref[...])
    outs = []
    for h in range(XA_HEADS):
        sl = slice(h * hdim, (h + 1) * hdim)
        s = _dot_nt(q[:, sl].astype(BF16), k_scr[:, sl]) * (hdim ** -0.5)
        p = jnp.exp(s - jnp.max(s, axis=-1, keepdims=True))
        p = p / jnp.sum(p, axis=-1, keepdims=True)
        outs.append(_dot(p.astype(BF16), v_scr[:, sl]).astype(BF16))
    x2 = x + _dot(jnp.concatenate(outs, axis=-1), wo_ref[...])
    x2_ref[0] = x2

    hm = _rms(x2, nmoe_ref[...])
    hm_hi = hm.astype(BF16)
    hm_ref[0] = hm_hi

    hm_lo = (hm - hm_hi.astype(F32)).astype(BF16)
    logits = (_dot(hm_hi, wr_hi_ref[...]) + _dot(hm_lo, wr_hi_ref[...]) + _dot(hm_hi, wr_lo_ref[...])
              + br_ref[...])

    rows = logits.shape[0]
    lane = lax.broadcasted_iota(jnp.int32, (rows, LANES), 1)
    lane_f = lane.astype(F32)
    neg_inf = jnp.float32(-jnp.inf)
    top_v, top_i, hots = [], [], []
    work = logits
    for _ in range(TOP_K):
        m = jnp.max(work, axis=-1, keepdims=True)
        idx = jnp.min(jnp.where(work == m, lane_f, float(LANES)), axis=-1, keepdims=True)
        hot = lane_f == idx
        work = jnp.where(hot, neg_inf, work)
        top_v.append(m)
        top_i.append(idx)
        hots.append(hot)
    es = [jnp.exp(v - top_v[0]) for v in top_v]
    denom = es[0] + es[1] + es[2] + es[3]
    gates = [e / denom for e in es]

    chosen = jnp.where(hots[0] | hots[1] | hots[2] | hots[3], 1.0, 0.0)
    ri = lax.broadcasted_iota(jnp.int32, (rows, rows), 0)
    ci = lax.broadcasted_iota(jnp.int32, (rows, rows), 1)
    strict = jnp.where(ri > ci, 1.0, 0.0).astype(BF16)
    before = _dot(strict, chosen.astype(BF16))
    ranks = [jnp.sum(jnp.where(hot, before, 0.0), axis=-1, keepdims=True) for hot in hots]
    cnt_ref[...] = jnp.broadcast_to(jnp.sum(chosen, axis=0, keepdims=True), cnt_ref.shape)

    meta = jnp.zeros((rows, LANES), F32)
    for j, col in enumerate(top_i + ranks + gates):
        meta = jnp.where(lane == j, col, meta)
    meta_ref[...] = meta


def _xattn_router(x1, mem, norm_xa_w, norm_mem_w, w_xq, w_xkv, w_xo, norm_moe_w, wr_hi, wr_lo, br_p):
    bsz, seq, d = x1.shape
    mlen = mem.shape[1]
    n_t = seq // TOK_TILE
    const = lambda shape: pl.BlockSpec(shape, lambda b, t: (0,) * len(shape))
    return pl.pallas_call(
        _xattn_kernel,
        grid=(bsz, n_t),
        in_specs=[
            pl.BlockSpec((1, TOK_TILE, d), lambda b, t: (b, t, 0)),
            pl.BlockSpec((1, mlen, d), lambda b, t: (b, 0, 0)),
            const((1, d)), const((1, d)),
            const((d, d)), const((d, 2 * d)), const((d, d)),
            const((1, d)),
            const((d, LANES)), const((d, LANES)), const((1, LANES)),
        ],
        out_specs=[
            pl.BlockSpec((1, TOK_TILE, d), lambda b, t: (b, t, 0)),
            pl.BlockSpec((1, TOK_TILE, d), lambda b, t: (b, t, 0)),
            pl.BlockSpec((TOK_TILE, LANES), lambda b, t: (b * n_t + t, 0)),
            pl.BlockSpec((8, LANES), lambda b, t: (b * n_t + t, 0)),
        ],
        out_shape=[
            jax.ShapeDtypeStruct((bsz, seq, d), F32),
            jax.ShapeDtypeStruct((bsz, seq, d), BF16),
            jax.ShapeDtypeStruct((bsz * seq, LANES), F32),
            jax.ShapeDtypeStruct((bsz * n_t * 8, LANES), F32),
        ],
        scratch_shapes=[
            pltpu.VMEM((mlen, d), BF16),
            pltpu.VMEM((mlen, d), BF16),
        ],
        compiler_params=pltpu.CompilerParams(
            dimension_semantics=("arbitrary", "arbitrary"), vmem_limit_bytes=VMEM_LIMIT_BYTES),
        name="xattn_router",
    )(x1, mem, norm_xa_w, norm_mem_w, w_xq, w_xkv, w_xo, norm_moe_w, wr_hi, wr_lo, br_p)


def _piece_copy(src_ref, src_row, dst_ref, dst_row, sem):
    return pltpu.make_async_copy(src_ref.at[pl.ds(pl.multiple_of(src_row, ROW_ALIGN), ROW_ALIGN)],
                                 dst_ref.at[pl.ds(pl.multiple_of(dst_row, ROW_ALIGN), ROW_ALIGN)], sem)


def _block_copy(src_ref, dst_ref, dst_block, sem):
    return pltpu.make_async_copy(src_ref, dst_ref.at[pl.ds(pl.multiple_of(dst_block * MOE_ROWS, MOE_ROWS), MOE_ROWS)], sem)


def _dispatch_kernel(pdst_ref, np_ref, tail_ref, cap_ref, tr_ref, g_ref, hm_ref, xs_ref, gs_ref,
                     stage_ref, gstage_ref, zx_ref, zg_ref, sem_x, sem_g, sem_z):
    j = pl.program_id(0)

    def spare_blocks(fn):
        def body(i, c):
            fn(_block_copy(zx_ref, xs_ref, cap_ref[0] + i, sem_z))
            fn(_block_copy(zg_ref, gs_ref, cap_ref[0] + i, sem_z))
            return c
        lax.fori_loop(0, cap_ref[1], body, 0)

    @pl.when(j == 0)
    def _():
        zx_ref[...] = jnp.zeros_like(zx_ref)
        zg_ref[...] = jnp.zeros_like(zg_ref)
        spare_blocks(lambda cp: cp.start())

        def per_expert(e, carry):
            n = tail_ref[1, e]
            off = tail_ref[0, e]

            def start(i, c):
                _piece_copy(zx_ref, 0, xs_ref, off + i * ROW_ALIGN, sem_x).start()
                _piece_copy(zg_ref, 0, gs_ref, off + i * ROW_ALIGN, sem_g).start()
                return c

            def wait(i, c):
                _piece_copy(zx_ref, 0, xs_ref, 0, sem_x).wait()
                _piece_copy(zg_ref, 0, gs_ref, 0, sem_g).wait()
                return c

            lax.fori_loop(0, n, start, 0)
            lax.fori_loop(0, n, wait, 0)
            return carry

        lax.fori_loop(0, N_EXPERTS, per_expert, 0)

    @pl.when(j == pl.num_programs(0) - 1)
    def _():
        spare_blocks(lambda cp: cp.wait())

    x16 = hm_ref[...]
    tr = tr_ref[0]
    g = g_ref[0]
    tile = x16.shape[0]
    for c in range(STAGE_ROWS // tile):
        rid = lax.broadcasted_iota(jnp.int32, (tile, tile), 0) + c * tile
        hot = [rid == tr[k:k + 1, :] for k in range(TOP_K)]
        sel = jnp.where(hot[0] | hot[1] | hot[2] | hot[3], 1.0, 0.0).astype(BF16)
        stage_ref[c * tile:(c + 1) * tile, :] = _dot(sel, x16)
        gsum = jnp.zeros((tile, 1), F32)
        for k in range(TOP_K):
            gsum = gsum + jnp.sum(jnp.where(hot[k], g[k:k + 1, :], 0.0), axis=-1, keepdims=True)
        gstage_ref[c * tile:(c + 1) * tile, :] = jnp.broadcast_to(gsum, (tile, LANES))

    n_pieces = np_ref[j]

    def start(q, c):
        _piece_copy(stage_ref, q * ROW_ALIGN, xs_ref, pdst_ref[0, 0, q], sem_x).start()
        _piece_copy(gstage_ref, q * ROW_ALIGN, gs_ref, pdst_ref[0, 0, q], sem_g).start()
        return c

    def wait(q, c):
        _piece_copy(stage_ref, 0, xs_ref, 0, sem_x).wait()
        _piece_copy(gstage_ref, 0, gs_ref, 0, sem_g).wait()
        return c

    lax.fori_loop(0, n_pieces, start, 0)
    lax.fori_loop(0, n_pieces, wait, 0)


def _dispatch(hm, pdst3, n_pieces, tail_info, spare, tr_t, g_t, n_rows):
    n_tok, d = hm.shape
    n_tiles = n_tok // TOK_TILE
    return pl.pallas_call(
        _dispatch_kernel,
        grid=(n_tiles,),
        in_specs=[
            pl.BlockSpec((1, 1, MAX_PIECES), lambda j: (j, 0, 0), memory_space=pltpu.SMEM),
            pl.BlockSpec(memory_space=pltpu.SMEM),
            pl.BlockSpec(memory_space=pltpu.SMEM),
            pl.BlockSpec(memory_space=pltpu.SMEM),
            pl.BlockSpec((1, 8, TOK_TILE), lambda j: (j, 0, 0)),
            pl.BlockSpec((1, 8, TOK_TILE), lambda j: (j, 0, 0)),
            pl.BlockSpec((TOK_TILE, d), lambda j: (j, 0)),
        ],
        out_specs=[pl.BlockSpec(memory_space=pl.ANY), pl.BlockSpec(memory_space=pl.ANY)],
        out_shape=[jax.ShapeDtypeStruct((n_rows, d), F32), jax.ShapeDtypeStruct((n_rows, LANES), F32)],
        scratch_shapes=[pltpu.VMEM((STAGE_ROWS, d), F32), pltpu.VMEM((STAGE_ROWS, LANES), F32),
                        pltpu.VMEM((MOE_ROWS, d), F32), pltpu.VMEM((MOE_ROWS, LANES), F32),
                        pltpu.SemaphoreType.DMA, pltpu.SemaphoreType.DMA, pltpu.SemaphoreType.DMA],
        compiler_params=pltpu.CompilerParams(
            dimension_semantics=("arbitrary",), vmem_limit_bytes=VMEM_LIMIT_BYTES),
        name="moe_dispatch",
    )(pdst3, n_pieces, tail_info, spare, tr_t, g_t, hm)


def _expert_kernel(be_ref, nu_ref, first_ref, slot_ref, nxt_ref, xs_ref, gs_ref, wgu_hbm, bgu_ref, wd_hbm,
                   bd_ref, ys_ref, wgu32_ref, wd32_ref, wgu16_ref, wd16_ref, sem):
    i = pl.program_id(0)
    f = wd16_ref.shape[0]

    def weight_copies(e, slot):
        return (pltpu.make_async_copy(wgu_hbm.at[e], wgu32_ref.at[slot], sem.at[0, slot]),
                pltpu.make_async_copy(wd_hbm.at[e], wd32_ref.at[slot], sem.at[1, slot]))

    @pl.when(i == 0)
    def _():
        for cp in weight_copies(be_ref[0], 0):
            cp.start()

    @pl.when(first_ref[i] == 1)
    def _():
        slot = slot_ref[i]
        for cp in weight_copies(be_ref[i], slot):
            cp.wait()

        @pl.when(nxt_ref[i] >= 0)
        def _():
            for cp in weight_copies(nxt_ref[i], 1 - slot):
                cp.start()

        wgu16_ref[...] = wgu32_ref[slot].astype(BF16)
        wd16_ref[...] = wd32_ref[slot].astype(BF16)

    @pl.when(i < nu_ref[0])
    def _():
        gu = _dot(xs_ref[...].astype(BF16), wgu16_ref[...]) + bgu_ref[0]
        gate = jnp.minimum(gu[:, :f], SWIGLU_LIMIT)
        up = jnp.clip(gu[:, f:], -SWIGLU_LIMIT, SWIGLU_LIMIT)
        act = (up + 1.0) * gate * jax.nn.sigmoid(SWIGLU_ALPHA * gate)
        ys_ref[...] = (_dot(act.astype(BF16), wd16_ref[...]) + bd_ref[0]) * gs_ref[:, 0:1]

    @pl.when(i >= nu_ref[0])
    def _():
        ys_ref[...] = jnp.zeros_like(ys_ref)


def _expert_mlp(xs, gs, block_e, n_used, first, slot, nxt, w_gu, b_gu, w_d, b_d):
    n_rows, d = xs.shape
    n_blocks = n_rows // MOE_ROWS
    f = w_d.shape[1]
    row_map = lambda i, be, nu, *_: (jnp.minimum(i, nu[0] - 1), 0)
    exp_map = lambda i, be, *_: (be[i], 0, 0)
    return pl.pallas_call(
        _expert_kernel,
        grid_spec=pltpu.PrefetchScalarGridSpec(
            num_scalar_prefetch=5,
            grid=(n_blocks,),
            in_specs=[
                pl.BlockSpec((MOE_ROWS, d), row_map),
                pl.BlockSpec((MOE_ROWS, LANES), row_map),
                pl.BlockSpec(memory_space=pl.ANY),
                pl.BlockSpec((1, 1, 2 * f), exp_map),
                pl.BlockSpec(memory_space=pl.ANY),
                pl.BlockSpec((1, 1, d), exp_map),
            ],
            out_specs=pl.BlockSpec((MOE_ROWS, d), lambda i, *_: (i, 0)),
            scratch_shapes=[
                pltpu.VMEM((2, d, 2 * f), F32), pltpu.VMEM((2, f, d), F32),
                pltpu.VMEM((d, 2 * f), BF16), pltpu.VMEM((f, d), BF16),
                pltpu.SemaphoreType.DMA((2, 2)),
            ],
        ),
        out_shape=jax.ShapeDtypeStruct((n_rows, d), F32),
        compiler_params=pltpu.CompilerParams(
            dimension_semantics=("arbitrary",), vmem_limit_bytes=VMEM_LIMIT_BYTES),
        name="moe_experts",
    )(block_e, n_used, first, slot, nxt, xs, gs, w_gu, b_gu, w_d, b_d)


def _combine_kernel(pdst_ref, np_ref, x2_ref, tr_ref, nfw_ref, ys_ref, out_ref, ybuf_ref, sem):
    j = pl.program_id(0)

    @pl.when(j == 0)
    def _():
        ybuf_ref[...] = jnp.zeros_like(ybuf_ref)

    n_pieces = np_ref[j]

    def start(q, c):
        _piece_copy(ys_ref, pdst_ref[0, 0, q], ybuf_ref, q * ROW_ALIGN, sem).start()
        return c

    def wait(q, c):
        _piece_copy(ys_ref, 0, ybuf_ref, 0, sem).wait()
        return c

    lax.fori_loop(0, n_pieces, start, 0)
    lax.fori_loop(0, n_pieces, wait, 0)

    tr = tr_ref[...]
    tile = tr.shape[0]
    acc = x2_ref[...]
    for c in range(STAGE_ROWS // tile):
        cid = lax.broadcasted_iota(jnp.int32, (tile, tile), 1) + c * tile
        hot = [cid == tr[:, k:k + 1] for k in range(TOP_K)]
        sel = jnp.where(hot[0] | hot[1] | hot[2] | hot[3], 1.0, 0.0).astype(BF16)
        acc = acc + _dot(sel, ybuf_ref[c * tile:(c + 1) * tile, :].astype(BF16))
    out_ref[...] = _rms(acc, nfw_ref[...])


def _combine(x2, tr, pdst3, n_pieces, ys, norm_final_w):
    n_tok, d = x2.shape
    return pl.pallas_call(
        _combine_kernel,
        grid=(n_tok // TOK_TILE,),
        in_specs=[
            pl.BlockSpec((1, 1, MAX_PIECES), lambda j: (j, 0, 0), memory_space=pltpu.SMEM),
            pl.BlockSpec(memory_space=pltpu.SMEM),
            pl.BlockSpec((TOK_TILE, d), lambda j: (j, 0)),
            pl.BlockSpec((TOK_TILE, TOP_K), lambda j: (j, 0)),
            pl.BlockSpec((1, d), lambda j: (0, 0)),
            pl.BlockSpec(memory_space=pl.ANY),
        ],
        out_specs=pl.BlockSpec((TOK_TILE, d), lambda j: (j, 0)),
        out_shape=jax.ShapeDtypeStruct((n_tok, d), F32),
        scratch_shapes=[pltpu.VMEM((STAGE_ROWS, d), F32), pltpu.SemaphoreType.DMA],
        compiler_params=pltpu.CompilerParams(
            dimension_semantics=("arbitrary",), vmem_limit_bytes=VMEM_LIMIT_BYTES),
        name="moe_combine",
    )(pdst3, n_pieces, x2, tr, norm_final_w, ys)


def _pad_heads(w, n_heads, width):
    lead = w.shape[:-1]
    w = w.reshape(lead + (n_heads, width))
    w = jnp.pad(w, [(0, 0)] * len(lead) + [(0, 0), (0, HEAD_W - width)])
    return w.reshape(lead + (n_heads * HEAD_W,))


def _round_up(x, m):
    return (x + m - 1) // m * m


def _routing_tables(meta, cnt, n_tiles):
    i32 = jnp.int32
    eidx = meta[:, 0:TOP_K].astype(i32)
    lrank = meta[:, TOP_K:2 * TOP_K].astype(i32)
    counts = cnt.reshape(n_tiles, 8, LANES)[:, 0, :N_EXPERTS].astype(i32)
    cnt8 = _round_up(counts, ROW_ALIGN)
    lend = jnp.cumsum(cnt8, axis=1)
    lstart = lend - cnt8
    tot = jnp.sum(cnt8, axis=0)
    padded = _round_up(tot, MOE_ROWS)
    pends = jnp.cumsum(padded)
    pstarts = pends - padded
    goff = pstarts[None, :] + jnp.cumsum(cnt8, axis=0) - cnt8

    hot = eidx.reshape(n_tiles, TOK_TILE, TOP_K, 1) == jnp.arange(N_EXPERTS, dtype=i32)
    tr = jnp.sum(jnp.where(hot, lstart[:, None, None, :], 0), axis=-1) + lrank.reshape(n_tiles, TOK_TILE, TOP_K)
    tr_t = jnp.concatenate([jnp.swapaxes(tr, 1, 2), jnp.full((n_tiles, 8 - TOP_K, TOK_TILE), -1, i32)], axis=1)
    g_t = jnp.concatenate([jnp.swapaxes(meta[:, 2 * TOP_K:3 * TOP_K].reshape(n_tiles, TOK_TILE, TOP_K), 1, 2),
                           jnp.zeros((n_tiles, 8 - TOP_K, TOK_TILE), F32)], axis=1)

    prow = jnp.arange(MAX_PIECES, dtype=i32) * ROW_ALIGN
    pe = jnp.sum(prow[None, :, None] >= lend[:, None, :], axis=-1)
    pe = jnp.minimum(pe, N_EXPERTS - 1)
    pick = pe[:, :, None] == jnp.arange(N_EXPERTS, dtype=i32)
    pdst = (jnp.sum(jnp.where(pick, (goff - lstart)[:, None, :], 0), axis=-1) + prow[None, :]).astype(i32)
    n_pieces = (lend[:, -1] // ROW_ALIGN).astype(i32)
    pdst = jnp.where(jnp.arange(MAX_PIECES, dtype=i32)[None, :] < n_pieces[:, None], pdst, 0)

    max_rows = n_tiles * (TOK_TILE * TOP_K + N_EXPERTS * (ROW_ALIGN - 1)) + N_EXPERTS * (MOE_ROWS - ROW_ALIGN)
    n_blocks = -(-max_rows // MOE_ROWS)
    n_used = (pends[-1] // MOE_ROWS).astype(i32)
    blk = jnp.arange(n_blocks, dtype=i32)
    block_e = jnp.minimum(jnp.sum(blk[:, None] * MOE_ROWS >= pends[None, :], axis=-1), N_EXPERTS - 1).astype(i32)
    block_e = jnp.where(blk < n_used, block_e, block_e[jnp.maximum(n_used - 1, 0)])
    first = ((blk < n_used) & ((blk == 0) | (block_e != jnp.roll(block_e, 1)))).astype(i32)
    has_rows = padded > 0
    slot_e = (jnp.cumsum(has_rows.astype(i32)) - 1) % 2
    later = jnp.arange(N_EXPERTS, dtype=i32)[None, :] > jnp.arange(N_EXPERTS, dtype=i32)[:, None]
    nxt_e = jnp.min(jnp.where(later & has_rows[None, :], jnp.arange(N_EXPERTS, dtype=i32)[None, :], N_EXPERTS), axis=1)
    nxt_e = jnp.where(nxt_e < N_EXPERTS, nxt_e, -1).astype(i32)
    tail_info = jnp.stack([pstarts + tot, (padded - tot) // ROW_ALIGN]).astype(i32)
    spare = jnp.stack([n_used, n_blocks - n_used]).astype(i32)
    return (tr.reshape(-1, TOP_K), tr_t, g_t, pdst.reshape(n_tiles, 1, MAX_PIECES), n_pieces, tail_info, spare,
            (block_e, n_used.reshape(1), first, slot_e[block_e], nxt_e[block_e]), n_blocks * MOE_ROWS)


def kernel(x, mem, norm_mix_w, w_in, hg_lb_logits, hg_onorm_w, gla_w_gk2, gla_b_gk, gla_onorm_w, w_out, norm_xa_w, norm_mem_w, w_xq, w_xkv, w_xo, norm_moe_w, w_router, b_router, w_gate_up, b_gate_up, w_down, b_down, norm_final_w):
    assert w_in.shape[0] == 1, "single-layer block"
    bsz, seq, d = x.shape
    n_tok = bsz * seq
    gk = GLA_HEADS * GLA_KDIM

    wi = w_in[0]
    w_in_p = jnp.concatenate([
        wi[:, :4 * _W],
        _pad_heads(wi[:, 4 * _W:4 * _W + gk], GLA_HEADS, GLA_KDIM),
        _pad_heads(wi[:, 4 * _W + gk:4 * _W + 2 * gk], GLA_HEADS, GLA_KDIM),
        wi[:, 4 * _W + 2 * gk:4 * _W + 2 * gk + 2 * _W],
        jnp.pad(wi[:, 4 * _W + 2 * gk + 2 * _W:], ((0, 0), (0, LANES - GLA_RANK))),
    ], axis=1).astype(BF16)
    w_gk2_p = jnp.pad(_pad_heads(gla_w_gk2[0], GLA_HEADS, GLA_KDIM),
                      ((0, LANES - GLA_RANK), (0, 0))).astype(BF16)
    b_gk_p = _pad_heads(gla_b_gk, GLA_HEADS, GLA_KDIM)
    wr = jnp.pad(w_router[0], ((0, 0), (0, LANES - N_EXPERTS)))
    wr_hi = wr.astype(BF16)
    wr_lo = (wr - wr_hi.astype(F32)).astype(BF16)
    br_p = jnp.pad(b_router, ((0, 0), (0, LANES - N_EXPERTS)), constant_values=-1e30)

    x1 = _token_mix(x, norm_mix_w, w_in_p, hg_lb_logits, hg_onorm_w, w_gk2_p, b_gk_p, gla_onorm_w,
                    w_out[0].astype(BF16))
    x2, hm, meta, cnt = _xattn_router(x1, mem, norm_xa_w, norm_mem_w, w_xq[0].astype(BF16),
                                      w_xkv[0].astype(BF16), w_xo[0].astype(BF16), norm_moe_w,
                                      wr_hi, wr_lo, br_p)

    n_tiles = n_tok // TOK_TILE
    tr, tr_t, g_t, pdst3, n_pieces, tail_info, spare, block_tables, n_rows = _routing_tables(meta, cnt, n_tiles)

    xs, gs = _dispatch(hm.reshape(n_tok, d), pdst3, n_pieces, tail_info, spare, tr_t, g_t, n_rows)
    ys = _expert_mlp(xs, gs, *block_tables, w_gate_up[0], b_gate_up[0][:, None, :],
                     w_down[0], b_down[0][:, None, :])
    out = _combine(x2.reshape(n_tok, d), tr, pdst3, n_pieces, ys, norm_final_w[None, :])
    return out.reshape(bsz, seq, d)
```

```python
import functools

import jax
import jax.numpy as jnp
from jax import lax
from jax.experimental import pallas as pl
from jax.experimental.pallas import tpu as pltpu

F32 = jnp.float32
BF16 = jnp.bfloat16
EPS = 1e-6
LOG2E = 1.4426950408889634

HEAD_W = 128
HG_HEADS = 4
GLA_HEADS = 4
GLA_KDIM = 64
N_HEADS = HG_HEADS + GLA_HEADS
GLA_RANK = 16
GLA_GATE_NORMALIZER = 16.0
CHUNK = 64
XA_HEADS = 4
N_EXPERTS = 32
TOP_K = 4
SWIGLU_LIMIT = 7.0
SWIGLU_ALPHA = 1.702

LANES = 128
VMEM_LIMIT_BYTES = 56 * 1024 * 1024

MIX_ROWS = 256
TOK_TILE = 256
MOE_ROWS = 512
ROW_ALIGN = 8
STAGE_ROWS = 1280
MAX_PIECES = STAGE_ROWS // ROW_ALIGN

_W = HG_HEADS * HEAD_W
COL_HQ, COL_HF, COL_HI, COL_HGATE = 0, _W, 2 * _W, 3 * _W
COL_GQ, COL_GK, COL_GV, COL_GGATE = 4 * _W, 5 * _W, 6 * _W, 7 * _W
COL_GLR = 8 * _W
PROJ_W = COL_GLR + LANES


def _rms(x, w):
    return x * lax.rsqrt(jnp.mean(x * x, axis=-1, keepdims=True) + EPS) * w


def _dot(a, b):
    return jnp.dot(a, b, preferred_element_type=F32)


def _dot_nt(a, b):
    return lax.dot_general(a, b, (((1,), (1,)), ((), ())), preferred_element_type=F32)


def _dot_tn(a, b):
    return lax.dot_general(a, b, (((0,), (0,)), ((), ())), preferred_element_type=F32)


def _head_chunk(hd, q_scr, k_scr, b2_scr, v, st_ref, sc_ref, masks):
    sl = slice(hd * HEAD_W, (hd + 1) * HEAD_W)
    q, k, b2 = q_scr[:, sl], k_scr[:, sl], b2_scr[:, sl]

    for s in range(0, CHUNK, 8):
        q_blk, b_blk = q[s:s + 8], b2[s:s + 8]
        for j in range(s, s + 8):
            e = jnp.exp2(b_blk - b2_scr[j:j + 1, sl])
            sc_ref[s:s + 8, j:j + 1] = jnp.sum(q_blk * (k_scr[j:j + 1, sl] * e), axis=-1, keepdims=True)
    scores = jnp.where(masks[0], sc_ref[...], 0.0)

    for lvl, size in enumerate((16, 32, 64)):
        mids = [b2_scr[r:r + 1, sl] for r in range(size // 2 - 1, CHUNK, size)]
        mid = jnp.concatenate([jnp.broadcast_to(m, (size, HEAD_W)) for m in mids], axis=0)
        dist = b2 - mid
        e = jnp.exp2(jnp.minimum(dist, -dist))
        r = _dot_nt((q * e).astype(BF16), (k * e).astype(BF16))
        scores = jnp.where(masks[lvl + 1], r, scores)

    b2_last = b2_scr[CHUNK - 1:CHUNK, sl]
    q_abs = (q * jnp.exp2(b2)).astype(BF16)
    k_end = (k * jnp.exp2(b2_last - b2)).astype(BF16)
    v16 = v.astype(BF16)
    st = st_ref[...]
    o = _dot(scores.astype(BF16), v16) + _dot_nt(q_abs, st.astype(BF16))
    st_ref[...] = st * jnp.exp2(b2_last) + _dot_tn(v16, k_end)
    return o


def _mix_kernel(x_ref, nw_ref, win_ref, lbl_ref, hgw_ref, wgk_ref, bgk_ref, glw_ref, wout_ref,
                o_ref, proj_ref, oall_ref, st_ref, q_scr, k_scr, b2_scr, sc_scr):
    @pl.when(pl.program_id(1) == 0)
    def _():
        st_ref[...] = jnp.zeros_like(st_ref)
        sc_scr[...] = jnp.zeros_like(sc_scr)

    x = x_ref[0]
    h = _rms(x, nw_ref[...]).astype(BF16)
    proj_ref[...] = _dot(h, win_ref[...])

    lbl = lbl_ref[...]
    e = jnp.exp(lbl - jnp.max(lbl, axis=0, keepdims=True))
    lb = e[0:1] / jnp.sum(e, axis=0, keepdims=True)

    ri = lax.broadcasted_iota(jnp.int32, (CHUNK, CHUNK), 0)
    ci = lax.broadcasted_iota(jnp.int32, (CHUNK, CHUNK), 1)
    tri = jnp.where(ri >= ci, 1.0, 0.0).astype(BF16)
    masks = [(ri // 8 == ci // 8) & (ri >= ci)]
    for size in (16, 32, 64):
        masks.append((ri // size == ci // size) & (ri // (size // 2) > ci // (size // 2)))

    def chunk(ck):
        rows = pl.ds(ck * CHUNK, CHUNK)
        f = lb + (1.0 - lb) * jax.nn.sigmoid(proj_ref[rows, COL_HF:COL_HF + _W])
        g_hg = jnp.log(f)
        z = _dot(proj_ref[rows, COL_GLR:COL_GLR + LANES].astype(BF16), wgk_ref[...]) + bgk_ref[...]
        g_gla = (jnp.minimum(z, 0.0) - jnp.log1p(jnp.exp(-jnp.abs(z)))) * (1.0 / GLA_GATE_NORMALIZER)
        g_all = jnp.concatenate([g_hg, g_gla], axis=-1)
        g_hi = g_all.astype(BF16)
        g_lo = (g_all - g_hi.astype(F32)).astype(BF16)
        b2_scr[...] = (_dot(tri, g_hi) + _dot(tri, g_lo)) * LOG2E
        hq = proj_ref[rows, COL_HQ:COL_HQ + _W]
        q_scr[:, 0:_W] = hq * jax.nn.sigmoid(hq)
        q_scr[:, _W:2 * _W] = proj_ref[rows, COL_GQ:COL_GQ + _W] * (GLA_KDIM ** -0.5)
        k_scr[:, 0:_W] = 1.0 - f
        k_scr[:, _W:2 * _W] = proj_ref[rows, COL_GK:COL_GK + _W]

        for hd in range(N_HEADS):
            lo = hd * HEAD_W
            col_v = COL_HI + lo if hd < HG_HEADS else COL_GV + lo - _W
            o = _head_chunk(hd, q_scr, k_scr, b2_scr, proj_ref[rows, col_v:col_v + HEAD_W],
                            st_ref.at[hd], sc_scr.at[hd], masks)
            oall_ref[rows, lo:lo + HEAD_W] = o

    for ck in range(MIX_ROWS // CHUNK):
        chunk(ck)

    ys = []
    for hd in range(N_HEADS):
        lo = hd * HEAD_W
        o = oall_ref[:, lo:lo + HEAD_W]
        if hd < HG_HEADS:
            w, gate = hgw_ref[...], proj_ref[:, COL_HGATE + lo:COL_HGATE + lo + HEAD_W]
        else:
            w, gate = glw_ref[...], proj_ref[:, COL_GGATE + lo - _W:COL_GGATE + lo - _W + HEAD_W]
        ys.append((_rms(o, w) * (gate * jax.nn.sigmoid(gate))).astype(BF16))
    y = jnp.concatenate(ys, axis=-1)
    o_ref[0] = x + _dot(y, wout_ref[...])


def _token_mix(x, norm_w, w_in_p, lb_logits, hg_onorm_w, w_gk2_p, b_gk_p, gla_onorm_w, w_out):
    bsz, seq, d = x.shape
    const = lambda shape: pl.BlockSpec(shape, lambda b, t: (0,) * len(shape))
    return pl.pallas_call(
        _mix_kernel,
        grid=(bsz, seq // MIX_ROWS),
        in_specs=[
            pl.BlockSpec((1, MIX_ROWS, d), lambda b, t: (b, t, 0)),
            const((1, d)),
            const((d, PROJ_W)),
            const(lb_logits.shape),
            const((1, HEAD_W)),
            const((LANES, _W)),
            const((1, _W)),
            const((1, HEAD_W)),
            const((2 * _W, d)),
        ],
        out_specs=pl.BlockSpec((1, MIX_ROWS, d), lambda b, t: (b, t, 0)),
        out_shape=jax.ShapeDtypeStruct((bsz, seq, d), F32),
        scratch_shapes=[
            pltpu.VMEM((MIX_ROWS, PROJ_W), F32),
            pltpu.VMEM((MIX_ROWS, N_HEADS * HEAD_W), F32),
            pltpu.VMEM((N_HEADS, HEAD_W, HEAD_W), F32),
            pltpu.VMEM((CHUNK, N_HEADS * HEAD_W), F32),
            pltpu.VMEM((CHUNK, N_HEADS * HEAD_W), F32),
            pltpu.VMEM((CHUNK, N_HEADS * HEAD_W), F32),
            pltpu.VMEM((N_HEADS, CHUNK, CHUNK), F32),
        ],
        compiler_params=pltpu.CompilerParams(
            dimension_semantics=("arbitrary", "arbitrary"), vmem_limit_bytes=VMEM_LIMIT_BYTES),
        name="token_mix",
    )(x, norm_w, w_in_p, lb_logits, hg_onorm_w, w_gk2_p, b_gk_p, gla_onorm_w, w_out)


def _xattn_kernel(x_ref, mem_ref, nxw_ref, nmw_ref, wq_ref, wkv_ref, wo_ref, nmoe_ref,
                  wr_hi_ref, wr_lo_ref, br_ref,
                  x2_ref, hm_ref, meta_ref, cnt_ref, k_scr, v_scr):
    d = x_ref.shape[-1]
    hdim = d // XA_HEADS

    @pl.when(pl.program_id(1) == 0)
    def _():
        m = _rms(mem_ref[0], nmw_ref[...]).astype(BF16)
        kv = _dot(m, wkv_ref[...])
        k_scr[...] = kv[:, :d].astype(BF16)
        v_scr[...] = kv[:, d:].astype(BF16)

    x = x_ref[0]
    q = _dot(_rms(x, nxw_ref[...]).astype(BF16), wq_ref[...])
    outs = []
    for h in range(XA_HEADS):
        sl = slice(h * hdim, (h + 1) * hdim)
        s = _dot_nt(q[:, sl].astype(BF16), k_scr[:, sl]) * (hdim ** -0.5)
        p = jnp.exp(s - jnp.max(s, axis=-1, keepdims=True))
        p = p / jnp.sum(p, axis=-1, keepdims=True)
        outs.append(_dot(p.astype(BF16), v_scr[:, sl]).astype(BF16))
    x2 = x + _dot(jnp.concatenate(outs, axis=-1), wo_ref[...])
    x2_ref[0] = x2

    hm = _rms(x2, nmoe_ref[...])
    hm_hi = hm.astype(BF16)
    hm_ref[0] = hm_hi

    hm_lo = (hm - hm_hi.astype(F32)).astype(BF16)
    logits = (_dot(hm_hi, wr_hi_ref[...]) + _dot(hm_lo, wr_hi_ref[...]) + _dot(hm_hi, wr_lo_ref[...])
              + br_ref[...])

    rows = logits.shape[0]
    lane = lax.broadcasted_iota(jnp.int32, (rows, LANES), 1)
    lane_f = lane.astype(F32)
    neg_inf = jnp.float32(-jnp.inf)
    top_v, top_i, hots = [], [], []
    work = logits
    for _ in range(TOP_K):
        m = jnp.max(work, axis=-1, keepdims=True)
        idx = jnp.min(jnp.where(work == m, lane_f, float(LANES)), axis=-1, keepdims=True)
        hot = lane_f == idx
        work = jnp.where(hot, neg_inf, work)
        top_v.append(m)
        top_i.append(idx)
        hots.append(hot)
    es = [jnp.exp(v - top_v[0]) for v in top_v]
    denom = es[0] + es[1] + es[2] + es[3]
    gates = [e / denom for e in es]

    chosen = jnp.where(hots[0] | hots[1] | hots[2] | hots[3], 1.0, 0.0)
    ri = lax.broadcasted_iota(jnp.int32, (rows, rows), 0)
    ci = lax.broadcasted_iota(jnp.int32, (rows, rows), 1)
    strict = jnp.where(ri > ci, 1.0, 0.0).astype(BF16)
    before = _dot(strict, chosen.astype(BF16))
    ranks = [jnp.sum(jnp.where(hot, before, 0.0), axis=-1, keepdims=True) for hot in hots]
    cnt_ref[...] = jnp.broadcast_to(jnp.sum(chosen, axis=0, keepdims=True), cnt_ref.shape)

    meta = jnp.zeros((rows, LANES), F32)
    for j, col in enumerate(top_i + ranks + gates):
        meta = jnp.where(lane == j, col, meta)
    meta_ref[...] = meta


def _xattn_router(x1, mem, norm_xa_w, norm_mem_w, w_xq, w_xkv, w_xo, norm_moe_w, wr_hi, wr_lo, br_p):
    bsz, seq, d = x1.shape
    mlen = mem.shape[1]
    n_t = seq // TOK_TILE
    const = lambda shape: pl.BlockSpec(shape, lambda b, t: (0,) * len(shape))
    return pl.pallas_call(
        _xattn_kernel,
        grid=(bsz, n_t),
        in_specs=[
            pl.BlockSpec((1, TOK_TILE, d), lambda b, t: (b, t, 0)),
            pl.BlockSpec((1, mlen, d), lambda b, t: (b, 0, 0)),
            const((1, d)), const((1, d)),
            const((d, d)), const((d, 2 * d)), const((d, d)),
            const((1, d)),
            const((d, LANES)), const((d, LANES)), const((1, LANES)),
        ],
        out_specs=[
            pl.BlockSpec((1, TOK_TILE, d), lambda b, t: (b, t, 0)),
            pl.BlockSpec((1, TOK_TILE, d), lambda b, t: (b, t, 0)),
            pl.BlockSpec((TOK_TILE, LANES), lambda b, t: (b * n_t + t, 0)),
            pl.BlockSpec((8, LANES), lambda b, t: (b * n_t + t, 0)),
        ],
        out_shape=[
            jax.ShapeDtypeStruct((bsz, seq, d), F32),
            jax.ShapeDtypeStruct((bsz, seq, d), BF16),
            jax.ShapeDtypeStruct((bsz * seq, LANES), F32),
            jax.ShapeDtypeStruct((bsz * n_t * 8, LANES), F32),
        ],
        scratch_shapes=[
            pltpu.VMEM((mlen, d), BF16),
            pltpu.VMEM((mlen, d), BF16),
        ],
        compiler_params=pltpu.CompilerParams(
            dimension_semantics=("arbitrary", "arbitrary"), vmem_limit_bytes=VMEM_LIMIT_BYTES),
        name="xattn_router",
    )(x1, mem, norm_xa_w, norm_mem_w, w_xq, w_xkv, w_xo, norm_moe_w, wr_hi, wr_lo, br_p)


def _piece_copy(src_ref, src_row, dst_ref, dst_row, sem):
    return pltpu.make_async_copy(src_ref.at[pl.ds(pl.multiple_of(src_row, ROW_ALIGN), ROW_ALIGN)],
                                 dst_ref.at[pl.ds(pl.multiple_of(dst_row, ROW_ALIGN), ROW_ALIGN)], sem)


def _block_copy(src_ref, dst_ref, dst_block, sem):
    return pltpu.make_async_copy(src_ref, dst_ref.at[pl.ds(pl.multiple_of(dst_block * MOE_ROWS, MOE_ROWS), MOE_ROWS)], sem)


def _dispatch_kernel(pdst_ref, np_ref, tail_ref, cap_ref, tr_ref, g_ref, hm_ref, xs_ref, gs_ref,
                     stage_ref, gstage_ref, zx_ref, zg_ref, sem_x, sem_g, sem_z):
    j = pl.program_id(0)

    def spare_blocks(fn):
        def body(i, c):
            fn(_block_copy(zx_ref, xs_ref, cap_ref[0] + i, sem_z))
            fn(_block_copy(zg_ref, gs_ref, cap_ref[0] + i, sem_z))
            return c
        lax.fori_loop(0, cap_ref[1], body, 0)

    @pl.when(j == 0)
    def _():
        zx_ref[...] = jnp.zeros_like(zx_ref)
        zg_ref[...] = jnp.zeros_like(zg_ref)
        spare_blocks(lambda cp: cp.start())

        def per_expert(e, carry):
            n = tail_ref[1, e]
            off = tail_ref[0, e]

            def start(i, c):
                _piece_copy(zx_ref, 0, xs_ref, off + i * ROW_ALIGN, sem_x).start()
                _piece_copy(zg_ref, 0, gs_ref, off + i * ROW_ALIGN, sem_g).start()
                return c

            def wait(i, c):
                _piece_copy(zx_ref, 0, xs_ref, 0, sem_x).wait()
                _piece_copy(zg_ref, 0, gs_ref, 0, sem_g).wait()
                return c

            lax.fori_loop(0, n, start, 0)
            lax.fori_loop(0, n, wait, 0)
            return carry

        lax.fori_loop(0, N_EXPERTS, per_expert, 0)

    @pl.when(j == pl.num_programs(0) - 1)
    def _():
        spare_blocks(lambda cp: cp.wait())

    x16 = hm_ref[...]
    tr = tr_ref[0]
    g = g_ref[0]
    tile = x16.shape[0]
    for c in range(STAGE_ROWS // tile):
        rid = lax.broadcasted_iota(jnp.int32, (tile, tile), 0) + c * tile
        hot = [rid == tr[k:k + 1, :] for k in range(TOP_K)]
        sel = jnp.where(hot[0] | hot[1] | hot[2] | hot[3], 1.0, 0.0).astype(BF16)
        stage_ref[c * tile:(c + 1) * tile, :] = _dot(sel, x16)
        gsum = jnp.zeros((tile, 1), F32)
        for k in range(TOP_K):
            gsum = gsum + jnp.sum(jnp.where(hot[k], g[k:k + 1, :], 0.0), axis=-1, keepdims=True)
        gstage_ref[c * tile:(c + 1) * tile, :] = jnp.broadcast_to(gsum, (tile, LANES))

    n_pieces = np_ref[j]

    def start(q, c):
        _piece_copy(stage_ref, q * ROW_ALIGN, xs_ref, pdst_ref[0, 0, q], sem_x).start()
        _piece_copy(gstage_ref, q * ROW_ALIGN, gs_ref, pdst_ref[0, 0, q], sem_g).start()
        return c

    def wait(q, c):
        _piece_copy(stage_ref, 0, xs_ref, 0, sem_x).wait()
        _piece_copy(gstage_ref, 0, gs_ref, 0, sem_g).wait()
        return c

    lax.fori_loop(0, n_pieces, start, 0)
    lax.fori_loop(0, n_pieces, wait, 0)


def _dispatch(hm, pdst3, n_pieces, tail_info, spare, tr_t, g_t, n_rows):
    n_tok, d = hm.shape
    n_tiles = n_tok // TOK_TILE
    return pl.pallas_call(
        _dispatch_kernel,
        grid=(n_tiles,),
        in_specs=[
            pl.BlockSpec((1, 1, MAX_PIECES), lambda j: (j, 0, 0), memory_space=pltpu.SMEM),
            pl.BlockSpec(memory_space=pltpu.SMEM),
            pl.BlockSpec(memory_space=pltpu.SMEM),
            pl.BlockSpec(memory_space=pltpu.SMEM),
            pl.BlockSpec((1, 8, TOK_TILE), lambda j: (j, 0, 0)),
            pl.BlockSpec((1, 8, TOK_TILE), lambda j: (j, 0, 0)),
            pl.BlockSpec((TOK_TILE, d), lambda j: (j, 0)),
        ],
        out_specs=[pl.BlockSpec(memory_space=pl.ANY), pl.BlockSpec(memory_space=pl.ANY)],
        out_shape=[jax.ShapeDtypeStruct((n_rows, d), F32), jax.ShapeDtypeStruct((n_rows, LANES), F32)],
        scratch_shapes=[pltpu.VMEM((STAGE_ROWS, d), F32), pltpu.VMEM((STAGE_ROWS, LANES), F32),
                        pltpu.VMEM((MOE_ROWS, d), F32), pltpu.VMEM((MOE_ROWS, LANES), F32),
                        pltpu.SemaphoreType.DMA, pltpu.SemaphoreType.DMA, pltpu.SemaphoreType.DMA],
        compiler_params=pltpu.CompilerParams(
            dimension_semantics=("arbitrary",), vmem_limit_bytes=VMEM_LIMIT_BYTES),
        name="moe_dispatch",
    )(pdst3, n_pieces, tail_info, spare, tr_t, g_t, hm)


def _expert_kernel(be_ref, nu_ref, first_ref, slot_ref, nxt_ref, xs_ref, gs_ref, wgu_hbm, bgu_ref, wd_hbm,
                   bd_ref, ys_ref, wgu32_ref, wd32_ref, wgu16_ref, wd16_ref, sem):
    i = pl.program_id(0)
    f = wd16_ref.shape[0]

    def weight_copies(e, slot):
        return (pltpu.make_async_copy(wgu_hbm.at[e], wgu32_ref.at[slot], sem.at[0, slot]),
                pltpu.make_async_copy(wd_hbm.at[e], wd32_ref.at[slot], sem.at[1, slot]))

    @pl.when(i == 0)
    def _():
        for cp in weight_copies(be_ref[0], 0):
            cp.start()

    @pl.when(first_ref[i] == 1)
    def _():
        slot = slot_ref[i]
        for cp in weight_copies(be_ref[i], slot):
            cp.wait()

        @pl.when(nxt_ref[i] >= 0)
        def _():
            for cp in weight_copies(nxt_ref[i], 1 - slot):
                cp.start()

        wgu16_ref[...] = wgu32_ref[slot].astype(BF16)
        wd16_ref[...] = wd32_ref[slot].astype(BF16)

    @pl.when(i < nu_ref[0])
    def _():
        gu = _dot(xs_ref[...].astype(BF16), wgu16_ref[...]) + bgu_ref[0]
        gate = jnp.minimum(gu[:, :f], SWIGLU_LIMIT)
        up = jnp.clip(gu[:, f:], -SWIGLU_LIMIT, SWIGLU_LIMIT)
        act = (up + 1.0) * gate * jax.nn.sigmoid(SWIGLU_ALPHA * gate)
        ys_ref[...] = (_dot(act.astype(BF16), wd16_ref[...]) + bd_ref[0]) * gs_ref[:, 0:1]

    @pl.when(i >= nu_ref[0])
    def _():
        ys_ref[...] = jnp.zeros_like(ys_ref)


def _expert_mlp(xs, gs, block_e, n_used, first, slot, nxt, w_gu, b_gu, w_d, b_d):
    n_rows, d = xs.shape
    n_blocks = n_rows // MOE_ROWS
    f = w_d.shape[1]
    row_map = lambda i, be, nu, *_: (jnp.minimum(i, nu[0] - 1), 0)
    exp_map = lambda i, be, *_: (be[i], 0, 0)
    return pl.pallas_call(
        _expert_kernel,
        grid_spec=pltpu.PrefetchScalarGridSpec(
            num_scalar_prefetch=5,
            grid=(n_blocks,),
            in_specs=[
                pl.BlockSpec((MOE_ROWS, d), row_map),
                pl.BlockSpec((MOE_ROWS, LANES), row_map),
                pl.BlockSpec(memory_space=pl.ANY),
                pl.BlockSpec((1, 1, 2 * f), exp_map),
                pl.BlockSpec(memory_space=pl.ANY),
                pl.BlockSpec((1, 1, d), exp_map),
            ],
            out_specs=pl.BlockSpec((MOE_ROWS, d), lambda i, *_: (i, 0)),
            scratch_shapes=[
                pltpu.VMEM((2, d, 2 * f), F32), pltpu.VMEM((2, f, d), F32),
                pltpu.VMEM((d, 2 * f), BF16), pltpu.VMEM((f, d), BF16),
                pltpu.SemaphoreType.DMA((2, 2)),
            ],
        ),
        out_shape=jax.ShapeDtypeStruct((n_rows, d), F32),
        compiler_params=pltpu.CompilerParams(
            dimension_semantics=("arbitrary",), vmem_limit_bytes=VMEM_LIMIT_BYTES),
        name="moe_experts",
    )(block_e, n_used, first, slot, nxt, xs, gs, w_gu, b_gu, w_d, b_d)


def _combine_kernel(pdst_ref, np_ref, x2_ref, tr_ref, nfw_ref, ys_ref, out_ref, ybuf_ref, sem):
    j = pl.program_id(0)

    @pl.when(j == 0)
    def _():
        ybuf_ref[...] = jnp.zeros_like(ybuf_ref)

    n_pieces = np_ref[j]

    def start(q, c):
        _piece_copy(ys_ref, pdst_ref[0, 0, q], ybuf_ref, q * ROW_ALIGN, sem).start()
        return c

    def wait(q, c):
        _piece_copy(ys_ref, 0, ybuf_ref, 0, sem).wait()
        return c

    lax.fori_loop(0, n_pieces, start, 0)
    lax.fori_loop(0, n_pieces, wait, 0)

    tr = tr_ref[...]
    tile = tr.shape[0]
    acc = x2_ref[...]
    for c in range(STAGE_ROWS // tile):
        cid = lax.broadcasted_iota(jnp.int32, (tile, tile), 1) + c * tile
        hot = [cid == tr[:, k:k + 1] for k in range(TOP_K)]
        sel = jnp.where(hot[0] | hot[1] | hot[2] | hot[3], 1.0, 0.0).astype(BF16)
        acc = acc + _dot(sel, ybuf_ref[c * tile:(c + 1) * tile, :].astype(BF16))
    out_ref[...] = _rms(acc, nfw_ref[...])


def _combine(x2, tr, pdst3, n_pieces, ys, norm_final_w):
    n_tok, d = x2.shape
    return pl.pallas_call(
        _combine_kernel,
        grid=(n_tok // TOK_TILE,),
        in_specs=[
            pl.BlockSpec((1, 1, MAX_PIECES), lambda j: (j, 0, 0), memory_space=pltpu.SMEM),
            pl.BlockSpec(memory_space=pltpu.SMEM),
            pl.BlockSpec((TOK_TILE, d), lambda j: (j, 0)),
            pl.BlockSpec((TOK_TILE, TOP_K), lambda j: (j, 0)),
            pl.BlockSpec((1, d), lambda j: (0, 0)),
            pl.BlockSpec(memory_space=pl.ANY),
        ],
        out_specs=pl.BlockSpec((TOK_TILE, d), lambda j: (j, 0)),
        out_shape=jax.ShapeDtypeStruct((n_tok, d), F32),
        scratch_shapes=[pltpu.VMEM((STAGE_ROWS, d), F32), pltpu.SemaphoreType.DMA],
        compiler_params=pltpu.CompilerParams(
            dimension_semantics=("arbitrary",), vmem_limit_bytes=VMEM_LIMIT_BYTES),
        name="moe_combine",
    )(pdst3, n_pieces, x2, tr, norm_final_w, ys)


def _pad_heads(w, n_heads, width):
    lead = w.shape[:-1]
    w = w.reshape(lead + (n_heads, width))
    w = jnp.pad(w, [(0, 0)] * len(lead) + [(0, 0), (0, HEAD_W - width)])
    return w.reshape(lead + (n_heads * HEAD_W,))


def _round_up(x, m):
    return (x + m - 1) // m * m


def _routing_tables(meta, cnt, n_tiles):
    i32 = jnp.int32
    eidx = meta[:, 0:TOP_K].astype(i32)
    lrank = meta[:, TOP_K:2 * TOP_K].astype(i32)
    counts = cnt.reshape(n_tiles, 8, LANES)[:, 0, :N_EXPERTS].astype(i32)
    cnt8 = _round_up(counts, ROW_ALIGN)
    lend = jnp.cumsum(cnt8, axis=1)
    lstart = lend - cnt8
    tot = jnp.sum(cnt8, axis=0)
    padded = _round_up(tot, MOE_ROWS)
    pends = jnp.cumsum(padded)
    pstarts = pends - padded
    goff = pstarts[None, :] + jnp.cumsum(cnt8, axis=0) - cnt8

    hot = eidx.reshape(n_tiles, TOK_TILE, TOP_K, 1) == jnp.arange(N_EXPERTS, dtype=i32)
    tr = jnp.sum(jnp.where(hot, lstart[:, None, None, :], 0), axis=-1) + lrank.reshape(n_tiles, TOK_TILE, TOP_K)
    tr_t = jnp.concatenate([jnp.swapaxes(tr, 1, 2), jnp.full((n_tiles, 8 - TOP_K, TOK_TILE), -1, i32)], axis=1)
    g_t = jnp.concatenate([jnp.swapaxes(meta[:, 2 * TOP_K:3 * TOP_K].reshape(n_tiles, TOK_TILE, TOP_K), 1, 2),
                           jnp.zeros((n_tiles, 8 - TOP_K, TOK_TILE), F32)], axis=1)

    prow = jnp.arange(MAX_PIECES, dtype=i32) * ROW_ALIGN
    pe = jnp.sum(prow[None, :, None] >= lend[:, None, :], axis=-1)
    pe = jnp.minimum(pe, N_EXPERTS - 1)
    pick = pe[:, :, None] == jnp.arange(N_EXPERTS, dtype=i32)
    pdst = (jnp.sum(jnp.where(pick, (goff - lstart)[:, None, :], 0), axis=-1) + prow[None, :]).astype(i32)
    n_pieces = (lend[:, -1] // ROW_ALIGN).astype(i32)
    pdst = jnp.where(jnp.arange(MAX_PIECES, dtype=i32)[None, :] < n_pieces[:, None], pdst, 0)

    max_rows = n_tiles * (TOK_TILE * TOP_K + N_EXPERTS * (ROW_ALIGN - 1)) + N_EXPERTS * (MOE_ROWS - ROW_ALIGN)
    n_blocks = -(-max_rows // MOE_ROWS)
    n_used = (pends[-1] // MOE_ROWS).astype(i32)
    blk = jnp.arange(n_blocks, dtype=i32)
    block_e = jnp.minimum(jnp.sum(blk[:, None] * MOE_ROWS >= pends[None, :], axis=-1), N_EXPERTS - 1).astype(i32)
    block_e = jnp.where(blk < n_used, block_e, block_e[jnp.maximum(n_used - 1, 0)])
    first = ((blk < n_used) & ((blk == 0) | (block_e != jnp.roll(block_e, 1)))).astype(i32)
    has_rows = padded > 0
    slot_e = (jnp.cumsum(has_rows.astype(i32)) - 1) % 2
    later = jnp.arange(N_EXPERTS, dtype=i32)[None, :] > jnp.arange(N_EXPERTS, dtype=i32)[:, None]
    nxt_e = jnp.min(jnp.where(later & has_rows[None, :], jnp.arange(N_EXPERTS, dtype=i32)[None, :], N_EXPERTS), axis=1)
    nxt_e = jnp.where(nxt_e < N_EXPERTS, nxt_e, -1).astype(i32)
    tail_info = jnp.stack([pstarts + tot, (padded - tot) // ROW_ALIGN]).astype(i32)
    spare = jnp.stack([n_used, n_blocks - n_used]).astype(i32)
    return (tr.reshape(-1, TOP_K), tr_t, g_t, pdst.reshape(n_tiles, 1, MAX_PIECES), n_pieces, tail_info, spare,
            (block_e, n_used.reshape(1), first, slot_e[block_e], nxt_e[block_e]), n_blocks * MOE_ROWS)


def kernel(x, mem, norm_mix_w, w_in, hg_lb_logits, hg_onorm_w, gla_w_gk2, gla_b_gk, gla_onorm_w, w_out, norm_xa_w, norm_mem_w, w_xq, w_xkv, w_xo, norm_moe_w, w_router, b_router, w_gate_up, b_gate_up, w_down, b_down, norm_final_w):
    assert w_in.shape[0] == 1, "single-layer block"
    bsz, seq, d = x.shape
    n_tok = bsz * seq
    gk = GLA_HEADS * GLA_KDIM

    wi = w_in[0]
    w_in_p = jnp.concatenate([
        wi[:, :4 * _W],
        _pad_heads(wi[:, 4 * _W:4 * _W + gk], GLA_HEADS, GLA_KDIM),
        _pad_heads(wi[:, 4 * _W + gk:4 * _W + 2 * gk], GLA_HEADS, GLA_KDIM),
        wi[:, 4 * _W + 2 * gk:4 * _W + 2 * gk + 2 * _W],
        jnp.pad(wi[:, 4 * _W + 2 * gk + 2 * _W:], ((0, 0), (0, LANES - GLA_RANK))),
    ], axis=1).astype(BF16)
    w_gk2_p = jnp.pad(_pad_heads(gla_w_gk2[0], GLA_HEADS, GLA_KDIM),
                      ((0, LANES - GLA_RANK), (0, 0))).astype(BF16)
    b_gk_p = _pad_heads(gla_b_gk, GLA_HEADS, GLA_KDIM)
    wr = jnp.pad(w_router[0], ((0, 0), (0, LANES - N_EXPERTS)))
    wr_hi = wr.astype(BF16)
    wr_lo = (wr - wr_hi.astype(F32)).astype(BF16)
    br_p = jnp.pad(b_router, ((0, 0), (0, LANES - N_EXPERTS)), constant_values=-1e30)

    x1 = _token_mix(x, norm_mix_w, w_in_p, hg_lb_logits, hg_onorm_w, w_gk2_p, b_gk_p, gla_onorm_w,
                    w_out[0].astype(BF16))
    x2, hm, meta, cnt = _xattn_router(x1, mem, norm_xa_w, norm_mem_w, w_xq[0].astype(BF16),
                                      w_xkv[0].astype(BF16), w_xo[0].astype(BF16), norm_moe_w,
                                      wr_hi, wr_lo, br_p)

    n_tiles = n_tok // TOK_TILE
    tr, tr_t, g_t, pdst3, n_pieces, tail_info, spare, block_tables, n_rows = _routing_tables(meta, cnt, n_tiles)

    xs, gs = _dispatch(hm.reshape(n_tok, d), pdst3, n_pieces, tail_info, spare, tr_t, g_t, n_rows)
    ys = _expert_mlp(xs, gs, *block_tables, w_gate_up[0], b_gate_up[0][:, None, :],
                     w_down[0], b_down[0][:, None, :])
    out = _combine(x2.reshape(n_tok, d), tr, pdst3, n_pieces, ys, norm_final_w[None, :])
    return out.reshape(bsz, seq, d)
```

```python
import functools

import jax
import jax.numpy as jnp
from jax import lax
from jax.experimental import pallas as pl
from jax.experimental.pallas import tpu as pltpu

F32 = jnp.float32
BF16 = jnp.bfloat16
EPS = 1e-6
LOG2E = 1.4426950408889634

HEAD_W = 128
HG_HEADS = 4
GLA_HEADS = 4
GLA_KDIM = 64
N_HEADS = HG_HEADS + GLA_HEADS
GLA_RANK = 16
GLA_GATE_NORMALIZER = 16.0
CHUNK = 64
XA_HEADS = 4
N_EXPERTS = 32
TOP_K = 4
SWIGLU_LIMIT = 7.0
SWIGLU_ALPHA = 1.702

LANES = 128
VMEM_LIMIT_BYTES = 56 * 1024 * 1024

MIX_ROWS = 256
TOK_TILE = 256
MOE_ROWS = 512
ROW_ALIGN = 8
STAGE_ROWS = 1280
MAX_PIECES = STAGE_ROWS // ROW_ALIGN

_W = HG_HEADS * HEAD_W
COL_HQ, COL_HF, COL_HI, COL_HGATE = 0, _W, 2 * _W, 3 * _W
COL_GQ, COL_GK, COL_GV, COL_GGATE = 4 * _W, 5 * _W, 6 * _W, 7 * _W
COL_GLR = 8 * _W
PROJ_W = COL_GLR + LANES


def _rms(x, w):
    return x * lax.rsqrt(jnp.mean(x * x, axis=-1, keepdims=True) + EPS) * w


def _dot(a, b):
    return jnp.dot(a, b, preferred_element_type=F32)


def _dot_nt(a, b):
    return lax.dot_general(a, b, (((1,), (1,)), ((), ())), preferred_element_type=F32)


def _dot_tn(a, b):
    return lax.dot_general(a, b, (((0,), (0,)), ((), ())), preferred_element_type=F32)


def _head_chunk(hd, q_scr, k_scr, b2_scr, v, st_ref, sc_ref, masks):
    sl = slice(hd * HEAD_W, (hd + 1) * HEAD_W)
    q, k, b2 = q_scr[:, sl], k_scr[:, sl], b2_scr[:, sl]

    for s in range(0, CHUNK, 8):
        q_blk, b_blk = q[s:s + 8], b2[s:s + 8]
        for j in range(s, s + 8):
            e = jnp.exp2(b_blk - b2_scr[j:j + 1, sl])
            sc_ref[s:s + 8, j:j + 1] = jnp.sum(q_blk * (k_scr[j:j + 1, sl] * e), axis=-1, keepdims=True)
    scores = jnp.where(masks[0], sc_ref[...], 0.0)

    for lvl, size in enumerate((16, 32, 64)):
        mids = [b2_scr[r:r + 1, sl] for r in range(size // 2 - 1, CHUNK, size)]
        mid = jnp.concatenate([jnp.broadcast_to(m, (size, HEAD_W)) for m in mids], axis=0)
        dist = b2 - mid
        e = jnp.exp2(jnp.minimum(dist, -dist))
        r = _dot_nt((q * e).astype(BF16), (k * e).astype(BF16))
        scores = jnp.where(masks[lvl + 1], r, scores)

    b2_last = b2_scr[CHUNK - 1:CHUNK, sl]
    q_abs = (q * jnp.exp2(b2)).astype(BF16)
    k_end = (k * jnp.exp2(b2_last - b2)).astype(BF16)
    v16 = v.astype(BF16)
    st = st_ref[...]
    o = _dot(scores.astype(BF16), v16) + _dot_nt(q_abs, st.astype(BF16))
    st_ref[...] = st * jnp.exp2(b2_last) + _dot_tn(v16, k_end)
    return o


def _mix_kernel(x_ref, nw_ref, win_ref, lbl_ref, hgw_ref, wgk_ref, bgk_ref, glw_ref, wout_ref,
                o_ref, proj_ref, oall_ref, st_ref, q_scr, k_scr, b2_scr, sc_scr):
    @pl.when(pl.program_id(1) == 0)
    def _():
        st_ref[...] = jnp.zeros_like(st_ref)
        sc_scr[...] = jnp.zeros_like(sc_scr)

    x = x_ref[0]
    h = _rms(x, nw_ref[...]).astype(BF16)
    proj_ref[...] = _dot(h, win_ref[...])

    lbl = lbl_ref[...]
    e = jnp.exp(lbl - jnp.max(lbl, axis=0, keepdims=True))
    lb = e[0:1] / jnp.sum(e, axis=0, keepdims=True)

    ri = lax.broadcasted_iota(jnp.int32, (CHUNK, CHUNK), 0)
    ci = lax.broadcasted_iota(jnp.int32, (CHUNK, CHUNK), 1)
    tri = jnp.where(ri >= ci, 1.0, 0.0).astype(BF16)
    masks = [(ri // 8 == ci // 8) & (ri >= ci)]
    for size in (16, 32, 64):
        masks.append((ri // size == ci // size) & (ri // (size // 2) > ci // (size // 2)))

    def chunk(ck):
        rows = pl.ds(ck * CHUNK, CHUNK)
        f = lb + (1.0 - lb) * jax.nn.sigmoid(proj_ref[rows, COL_HF:COL_HF + _W])
        g_hg = jnp.log(f)
        z = _dot(proj_ref[rows, COL_GLR:COL_GLR + LANES].astype(BF16), wgk_ref[...]) + bgk_ref[...]
        g_gla = (jnp.minimum(z, 0.0) - jnp.log1p(jnp.exp(-jnp.abs(z)))) * (1.0 / GLA_GATE_NORMALIZER)
        g_all = jnp.concatenate([g_hg, g_gla], axis=-1)
        g_hi = g_all.astype(BF16)
        g_lo = (g_all - g_hi.astype(F32)).astype(BF16)
        b2_scr[...] = (_dot(tri, g_hi) + _dot(tri, g_lo)) * LOG2E
        hq = proj_ref[rows, COL_HQ:COL_HQ + _W]
        q_scr[:, 0:_W] = hq * jax.nn.sigmoid(hq)
        q_scr[:, _W:2 * _W] = proj_ref[rows, COL_GQ:COL_GQ + _W] * (GLA_KDIM ** -0.5)
        k_scr[:, 0:_W] = 1.0 - f
        k_scr[:, _W:2 * _W] = proj_ref[rows, COL_GK:COL_GK + _W]

        for hd in range(N_HEADS):
            lo = hd * HEAD_W
            col_v = COL_HI + lo if hd < HG_HEADS else COL_GV + lo - _W
            o = _head_chunk(hd, q_scr, k_scr, b2_scr, proj_ref[rows, col_v:col_v + HEAD_W],
                            st_ref.at[hd], sc_scr.at[hd], masks)
            oall_ref[rows, lo:lo + HEAD_W] = o

    for ck in range(MIX_ROWS // CHUNK):
        chunk(ck)

    ys = []
    for hd in range(N_HEADS):
        lo = hd * HEAD_W
        o = oall_ref[:, lo:lo + HEAD_W]
        if hd < HG_HEADS:
            w, gate = hgw_ref[...], proj_ref[:, COL_HGATE + lo:COL_HGATE + lo + HEAD_W]
        else:
            w, gate = glw_ref[...], proj_ref[:, COL_GGATE + lo - _W:COL_GGATE + lo - _W + HEAD_W]
        ys.append((_rms(o, w) * (gate * jax.nn.sigmoid(gate))).astype(BF16))
    y = jnp.concatenate(ys, axis=-1)
    o_ref[0] = x + _dot(y, wout_ref[...])


def _token_mix(x, norm_w, w_in_p, lb_logits, hg_onorm_w, w_gk2_p, b_gk_p, gla_onorm_w, w_out):
    bsz, seq, d = x.shape
    const = lambda shape: pl.BlockSpec(shape, lambda b, t: (0,) * len(shape))
    return pl.pallas_call(
        _mix_kernel,
        grid=(bsz, seq // MIX_ROWS),
        in_specs=[
            pl.BlockSpec((1, MIX_ROWS, d), lambda b, t: (b, t, 0)),
            const((1, d)),
            const((d, PROJ_W)),
            const(lb_logits.shape),
            const((1, HEAD_W)),
            const((LANES, _W)),
            const((1, _W)),
            const((1, HEAD_W)),
            const((2 * _W, d)),
        ],
        out_specs=pl.BlockSpec((1, MIX_ROWS, d), lambda b, t: (b, t, 0)),
        out_shape=jax.ShapeDtypeStruct((bsz, seq, d), F32),
        scratch_shapes=[
            pltpu.VMEM((MIX_ROWS, PROJ_W), F32),
            pltpu.VMEM((MIX_ROWS, N_HEADS * HEAD_W), F32),
            pltpu.VMEM((N_HEADS, HEAD_W, HEAD_W), F32),
            pltpu.VMEM((CHUNK, N_HEADS * HEAD_W), F32),
            pltpu.VMEM((CHUNK, N_HEADS * HEAD_W), F32),
            pltpu.VMEM((CHUNK, N_HEADS * HEAD_W), F32),
            pltpu.VMEM((N_HEADS, CHUNK, CHUNK), F32),
        ],
        compiler_params=pltpu.CompilerParams(
            dimension_semantics=("arbitrary", "arbitrary"), vmem_limit_bytes=VMEM_LIMIT_BYTES),
        name="token_mix",
    )(x, norm_w, w_in_p, lb_logits, hg_onorm_w, w_gk2_p, b_gk_p, gla_onorm_w, w_out)


def _xattn_kernel(x_ref, mem_ref, nxw_ref, nmw_ref, wq_ref, wkv_ref, wo_ref, nmoe_ref,
                  wr_hi_ref, wr_lo_ref, br_ref,
                  x2_ref, hm_ref, meta_ref, cnt_ref, k_scr, v_scr):
    d = x_ref.shape[-1]
    hdim = d // XA_HEADS

    @pl.when(pl.program_id(1) == 0)
    def _():
        m = _rms(mem_ref[0], nmw_ref[...]).astype(BF16)
        kv = _dot(m, wkv_ref[...])
        k_scr[...] = kv[:, :d].astype(BF16)
        v_scr[...] = kv[:, d:].astype(BF16)

    x = x_ref[0]
    q = _dot(_rms(x, nxw_ref[...]).astype(BF16), wq_ref[...])
    outs = []
    for h in range(XA_HEADS):
        sl = slice(h * hdim, (h + 1) * hdim)
        s = _dot_nt(q[:, sl].astype(BF16), k_scr[:, sl]) * (hdim ** -0.5)
        p = jnp.exp(s - jnp.max(s, axis=-1, keepdims=True))
        p = p / jnp.sum(p, axis=-1, keepdims=True)
        outs.append(_dot(p.astype(BF16), v_scr[:, sl]).astype(BF16))
    x2 = x + _dot(jnp.concatenate(outs, axis=-1), wo_ref[...])
    x2_ref[0] = x2

    hm = _rms(x2, nmoe_ref[...])
    hm_hi = hm.astype(BF16)
    hm_ref[0] = hm_hi

    hm_lo = (hm - hm_hi.astype(F32)).astype(BF16)
    logits = (_dot(hm_hi, wr_hi_ref[...]) + _dot(hm_lo, wr_hi_ref[...]) + _dot(hm_hi, wr_lo_ref[...])
              + br_ref[...])

    rows = logits.shape[0]
    lane = lax.broadcasted_iota(jnp.int32, (rows, LANES), 1)
    lane_f = lane.astype(F32)
    neg_inf = jnp.float32(-jnp.inf)
    top_v, top_i, hots = [], [], []
    work = logits
    for _ in range(TOP_K):
        m = jnp.max(work, axis=-1, keepdims=True)
        idx = jnp.min(jnp.where(work == m, lane_f, float(LANES)), axis=-1, keepdims=True)
        hot = lane_f == idx
        work = jnp.where(hot, neg_inf, work)
        top_v.append(m)
        top_i.append(idx)
        hots.append(hot)
    es = [jnp.exp(v - top_v[0]) for v in top_v]
    denom = es[0] + es[1] + es[2] + es[3]
    gates = [e / denom for e in es]

    chosen = jnp.where(hots[0] | hots[1] | hots[2] | hots[3], 1.0, 0.0)
    ri = lax.broadcasted_iota(jnp.int32, (rows, rows), 0)
    ci = lax.broadcasted_iota(jnp.int32, (rows, rows), 1)
    strict = jnp.where(ri > ci, 1.0, 0.0).astype(BF16)
    before = _dot(strict, chosen.astype(BF16))
    counts = jnp.sum(chosen, axis=0, keepdims=True)
    cnt_ref[...] = jnp.broadcast_to(counts, cnt_ref.shape)
    padded = jnp.floor((counts + (ROW_ALIGN - 1.0)) * (1.0 / ROW_ALIGN)) * ROW_ALIGN
    li = lax.broadcasted_iota(jnp.int32, (LANES, LANES), 0)
    lj = lax.broadcasted_iota(jnp.int32, (LANES, LANES), 1)
    group_start = _dot(jnp.broadcast_to(padded, (8, LANES)).astype(BF16),
                       jnp.where(li < lj, 1.0, 0.0).astype(BF16))[0:1]
    slot_of = before + group_start
    slots = [jnp.sum(jnp.where(hot, slot_of, 0.0), axis=-1, keepdims=True) for hot in hots]

    meta = jnp.zeros((rows, LANES), F32)
    for j, col in enumerate(top_i + slots + gates):
        meta = jnp.where(lane == j, col, meta)
    meta_ref[...] = meta


def _xattn_router(x1, mem, norm_xa_w, norm_mem_w, w_xq, w_xkv, w_xo, norm_moe_w, wr_hi, wr_lo, br_p):
    bsz, seq, d = x1.shape
    mlen = mem.shape[1]
    n_t = seq // TOK_TILE
    const = lambda shape: pl.BlockSpec(shape, lambda b, t: (0,) * len(shape))
    return pl.pallas_call(
        _xattn_kernel,
        grid=(bsz, n_t),
        in_specs=[
            pl.BlockSpec((1, TOK_TILE, d), lambda b, t: (b, t, 0)),
            pl.BlockSpec((1, mlen, d), lambda b, t: (b, 0, 0)),
            const((1, d)), const((1, d)),
            const((d, d)), const((d, 2 * d)), const((d, d)),
            const((1, d)),
            const((d, LANES)), const((d, LANES)), const((1, LANES)),
        ],
        out_specs=[
            pl.BlockSpec((1, TOK_TILE, d), lambda b, t: (b, t, 0)),
            pl.BlockSpec((1, TOK_TILE, d), lambda b, t: (b, t, 0)),
            pl.BlockSpec((TOK_TILE, LANES), lambda b, t: (b * n_t + t, 0)),
            pl.BlockSpec((8, LANES), lambda b, t: (b * n_t + t, 0)),
        ],
        out_shape=[
            jax.ShapeDtypeStruct((bsz, seq, d), F32),
            jax.ShapeDtypeStruct((bsz, seq, d), BF16),
            jax.ShapeDtypeStruct((bsz * seq, LANES), F32),
            jax.ShapeDtypeStruct((bsz * n_t * 8, LANES), F32),
        ],
        scratch_shapes=[
            pltpu.VMEM((mlen, d), BF16),
            pltpu.VMEM((mlen, d), BF16),
        ],
        compiler_params=pltpu.CompilerParams(
            dimension_semantics=("arbitrary", "arbitrary"), vmem_limit_bytes=VMEM_LIMIT_BYTES),
        name="xattn_router",
    )(x1, mem, norm_xa_w, norm_mem_w, w_xq, w_xkv, w_xo, norm_moe_w, wr_hi, wr_lo, br_p)


def _piece_copy(src_ref, src_row, dst_ref, dst_row, sem):
    return pltpu.make_async_copy(src_ref.at[pl.ds(pl.multiple_of(src_row, ROW_ALIGN), ROW_ALIGN)],
                                 dst_ref.at[pl.ds(pl.multiple_of(dst_row, ROW_ALIGN), ROW_ALIGN)], sem)


def _block_copy(src_ref, dst_ref, dst_block, sem):
    return pltpu.make_async_copy(src_ref, dst_ref.at[pl.ds(pl.multiple_of(dst_block * MOE_ROWS, MOE_ROWS), MOE_ROWS)], sem)


def _dispatch_kernel(pdst_ref, np_ref, tail_ref, cap_ref, tr_ref, g_ref, hm_ref, xs_ref, gs_ref,
                     stage_ref, gstage_ref, zx_ref, zg_ref, sem_x, sem_g, sem_z):
    s = pl.program_id(0)
    last = pl.num_programs(0) - 1

    def spare_blocks(fn):
        def body(i, c):
            fn(_block_copy(zx_ref, xs_ref, cap_ref[0] + i, sem_z))
            fn(_block_copy(zg_ref, gs_ref, cap_ref[0] + i, sem_z))
            return c
        lax.fori_loop(0, cap_ref[1], body, 0)

    @pl.when(s == 0)
    def _():
        zx_ref[...] = jnp.zeros_like(zx_ref)
        zg_ref[...] = jnp.zeros_like(zg_ref)
        spare_blocks(lambda cp: cp.start())

        def per_expert(e, carry):
            n = tail_ref[1, e]
            off = tail_ref[0, e]

            def start(i, c):
                _piece_copy(zx_ref, 0, xs_ref, off + i * ROW_ALIGN, sem_x.at[0]).start()
                _piece_copy(zg_ref, 0, gs_ref, off + i * ROW_ALIGN, sem_g.at[0]).start()
                return c

            def wait(i, c):
                _piece_copy(zx_ref, 0, xs_ref, 0, sem_x.at[0]).wait()
                _piece_copy(zg_ref, 0, gs_ref, 0, sem_g.at[0]).wait()
                return c

            lax.fori_loop(0, n, start, 0)
            lax.fori_loop(0, n, wait, 0)
            return carry

        lax.fori_loop(0, N_EXPERTS, per_expert, 0)

    def drain(u, tile):
        def wait(q, c):
            _piece_copy(stage_ref.at[u], 0, xs_ref, 0, sem_x.at[u]).wait()
            _piece_copy(gstage_ref.at[u], 0, gs_ref, 0, sem_g.at[u]).wait()
            return c
        lax.fori_loop(0, np_ref[tile], wait, 0)

    for u in range(2):
        tile = 2 * s + u

        @pl.when(s > 0)
        def _():
            drain(u, tile - 2)

        x16 = hm_ref[u * TOK_TILE:(u + 1) * TOK_TILE, :]
        tr = tr_ref[u]
        g = g_ref[u]
        for c in range(STAGE_ROWS // TOK_TILE):
            rid = lax.broadcasted_iota(jnp.int32, (TOK_TILE, TOK_TILE), 0) + c * TOK_TILE
            hot = [rid == tr[k:k + 1, :] for k in range(TOP_K)]
            sel = jnp.where(hot[0] | hot[1] | hot[2] | hot[3], 1.0, 0.0).astype(BF16)
            stage_ref[u, c * TOK_TILE:(c + 1) * TOK_TILE, :] = _dot(sel, x16)
            gsum = jnp.zeros((TOK_TILE, 1), F32)
            for k in range(TOP_K):
                gsum = gsum + jnp.sum(jnp.where(hot[k], g[k:k + 1, :], 0.0), axis=-1, keepdims=True)
            gstage_ref[u, c * TOK_TILE:(c + 1) * TOK_TILE, :] = jnp.broadcast_to(gsum, (TOK_TILE, LANES))

        def start(q, c):
            _piece_copy(stage_ref.at[u], q * ROW_ALIGN, xs_ref, pdst_ref[tile, q], sem_x.at[u]).start()
            _piece_copy(gstage_ref.at[u], q * ROW_ALIGN, gs_ref, pdst_ref[tile, q], sem_g.at[u]).start()
            return c

        lax.fori_loop(0, np_ref[tile], start, 0)

    @pl.when(s == last)
    def _():
        drain(0, 2 * s)
        drain(1, 2 * s + 1)
        spare_blocks(lambda cp: cp.wait())


def _dispatch(hm, pdst, n_pieces, tail_info, spare, tr_t, g_t, n_rows):
    n_tok, d = hm.shape
    n_tiles = n_tok // TOK_TILE
    assert n_tiles % 2 == 0
    smem = pl.BlockSpec(memory_space=pltpu.SMEM)
    return pl.pallas_call(
        _dispatch_kernel,
        grid=(n_tiles // 2,),
        in_specs=[
            smem, smem, smem, smem,
            pl.BlockSpec((2, 8, TOK_TILE), lambda s: (s, 0, 0)),
            pl.BlockSpec((2, 8, TOK_TILE), lambda s: (s, 0, 0)),
            pl.BlockSpec((2 * TOK_TILE, d), lambda s: (s, 0)),
        ],
        out_specs=[pl.BlockSpec(memory_space=pl.ANY), pl.BlockSpec(memory_space=pl.ANY)],
        out_shape=[jax.ShapeDtypeStruct((n_rows, d), F32), jax.ShapeDtypeStruct((n_rows, LANES), F32)],
        scratch_shapes=[pltpu.VMEM((2, STAGE_ROWS, d), F32), pltpu.VMEM((2, STAGE_ROWS, LANES), F32),
                        pltpu.VMEM((MOE_ROWS, d), F32), pltpu.VMEM((MOE_ROWS, LANES), F32),
                        pltpu.SemaphoreType.DMA((2,)), pltpu.SemaphoreType.DMA((2,)), pltpu.SemaphoreType.DMA],
        compiler_params=pltpu.CompilerParams(
            dimension_semantics=("arbitrary",), vmem_limit_bytes=VMEM_LIMIT_BYTES),
        name="moe_dispatch",
    )(pdst, n_pieces, tail_info, spare, tr_t, g_t, hm)


def _expert_kernel(be_ref, nu_ref, first_ref, slot_ref, nxt_ref, xs_ref, gs_ref, wgu_hbm, bgu_ref, wd_hbm,
                   bd_ref, ys_ref, wgu32_ref, wd32_ref, wgu16_ref, wd16_ref, sem):
    i = pl.program_id(0)
    f = wd16_ref.shape[0]

    def weight_copies(e, slot):
        return (pltpu.make_async_copy(wgu_hbm.at[e], wgu32_ref.at[slot], sem.at[0, slot]),
                pltpu.make_async_copy(wd_hbm.at[e], wd32_ref.at[slot], sem.at[1, slot]))

    @pl.when(i == 0)
    def _():
        for cp in weight_copies(be_ref[0], 0):
            cp.start()

    @pl.when(first_ref[i] == 1)
    def _():
        slot = slot_ref[i]
        for cp in weight_copies(be_ref[i], slot):
            cp.wait()

        @pl.when(nxt_ref[i] >= 0)
        def _():
            for cp in weight_copies(nxt_ref[i], 1 - slot):
                cp.start()

        wgu16_ref[...] = wgu32_ref[slot].astype(BF16)
        wd16_ref[...] = wd32_ref[slot].astype(BF16)

    @pl.when(i < nu_ref[0])
    def _():
        gu = _dot(xs_ref[...].astype(BF16), wgu16_ref[...]) + bgu_ref[0]
        gate = jnp.minimum(gu[:, :f], SWIGLU_LIMIT)
        up = jnp.clip(gu[:, f:], -SWIGLU_LIMIT, SWIGLU_LIMIT)
        act = (up + 1.0) * gate * jax.nn.sigmoid(SWIGLU_ALPHA * gate)
        ys_ref[...] = (_dot(act.astype(BF16), wd16_ref[...]) + bd_ref[0]) * gs_ref[:, 0:1]

    @pl.when(i >= nu_ref[0])
    def _():
        ys_ref[...] = jnp.zeros_like(ys_ref)


def _expert_mlp(xs, gs, block_e, n_used, first, slot, nxt, w_gu, b_gu, w_d, b_d):
    n_rows, d = xs.shape
    n_blocks = n_rows // MOE_ROWS
    f = w_d.shape[1]
    row_map = lambda i, be, nu, *_: (jnp.minimum(i, nu[0] - 1), 0)
    exp_map = lambda i, be, *_: (be[i], 0, 0)
    return pl.pallas_call(
        _expert_kernel,
        grid_spec=pltpu.PrefetchScalarGridSpec(
            num_scalar_prefetch=5,
            grid=(n_blocks,),
            in_specs=[
                pl.BlockSpec((MOE_ROWS, d), row_map),
                pl.BlockSpec((MOE_ROWS, LANES), row_map),
                pl.BlockSpec(memory_space=pl.ANY),
                pl.BlockSpec((1, 1, 2 * f), exp_map),
                pl.BlockSpec(memory_space=pl.ANY),
                pl.BlockSpec((1, 1, d), exp_map),
            ],
            out_specs=pl.BlockSpec((MOE_ROWS, d), lambda i, *_: (i, 0)),
            scratch_shapes=[
                pltpu.VMEM((2, d, 2 * f), F32), pltpu.VMEM((2, f, d), F32),
                pltpu.VMEM((d, 2 * f), BF16), pltpu.VMEM((f, d), BF16),
                pltpu.SemaphoreType.DMA((2, 2)),
            ],
        ),
        out_shape=jax.ShapeDtypeStruct((n_rows, d), F32),
        compiler_params=pltpu.CompilerParams(
            dimension_semantics=("arbitrary",), vmem_limit_bytes=VMEM_LIMIT_BYTES),
        name="moe_experts",
    )(block_e, n_used, first, slot, nxt, xs, gs, w_gu, b_gu, w_d, b_d)


def _combine_kernel(pdst_ref, np_ref, x2_ref, tr_ref, nfw_ref, ys_ref, out_ref, ybuf_ref, sem):
    s = pl.program_id(0)
    last = pl.num_programs(0) - 1

    def fetch(u, tile):
        def start(q, c):
            _piece_copy(ys_ref, pdst_ref[tile, q], ybuf_ref.at[u], q * ROW_ALIGN, sem.at[u]).start()
            return c
        lax.fori_loop(0, np_ref[tile], start, 0)

    def finish(u, tile):
        def wait(q, c):
            _piece_copy(ys_ref, 0, ybuf_ref.at[u], 0, sem.at[u]).wait()
            return c
        lax.fori_loop(0, np_ref[tile], wait, 0)

        rows = slice(u * TOK_TILE, (u + 1) * TOK_TILE)
        tr = tr_ref[rows, :]
        acc = x2_ref[rows, :]
        for c in range(STAGE_ROWS // TOK_TILE):
            cid = lax.broadcasted_iota(jnp.int32, (TOK_TILE, TOK_TILE), 1) + c * TOK_TILE
            hot = [cid == tr[:, k:k + 1] for k in range(TOP_K)]
            sel = jnp.where(hot[0] | hot[1] | hot[2] | hot[3], 1.0, 0.0).astype(BF16)
            acc = acc + _dot(sel, ybuf_ref[u, c * TOK_TILE:(c + 1) * TOK_TILE, :].astype(BF16))
        out_ref[rows, :] = _rms(acc, nfw_ref[...])

    @pl.when(s == 0)
    def _():
        ybuf_ref[...] = jnp.zeros_like(ybuf_ref)
        fetch(0, 0)

    fetch(1, 2 * s + 1)
    finish(0, 2 * s)

    @pl.when(s < last)
    def _():
        fetch(0, 2 * s + 2)

    finish(1, 2 * s + 1)


def _combine(x2, tr, pdst, n_pieces, ys, norm_final_w):
    n_tok, d = x2.shape
    n_tiles = n_tok // TOK_TILE
    assert n_tiles % 2 == 0
    smem = pl.BlockSpec(memory_space=pltpu.SMEM)
    return pl.pallas_call(
        _combine_kernel,
        grid=(n_tiles // 2,),
        in_specs=[
            smem, smem,
            pl.BlockSpec((2 * TOK_TILE, d), lambda s: (s, 0)),
            pl.BlockSpec((2 * TOK_TILE, TOP_K), lambda s: (s, 0)),
            pl.BlockSpec((1, d), lambda s: (0, 0)),
            pl.BlockSpec(memory_space=pl.ANY),
        ],
        out_specs=pl.BlockSpec((2 * TOK_TILE, d), lambda s: (s, 0)),
        out_shape=jax.ShapeDtypeStruct((n_tok, d), F32),
        scratch_shapes=[pltpu.VMEM((2, STAGE_ROWS, d), F32), pltpu.SemaphoreType.DMA((2,))],
        compiler_params=pltpu.CompilerParams(
            dimension_semantics=("arbitrary",), vmem_limit_bytes=VMEM_LIMIT_BYTES),
        name="moe_combine",
    )(pdst, n_pieces, x2, tr, norm_final_w, ys)


def _pad_heads(w, n_heads, width):
    lead = w.shape[:-1]
    w = w.reshape(lead + (n_heads, width))
    w = jnp.pad(w, [(0, 0)] * len(lead) + [(0, 0), (0, HEAD_W - width)])
    return w.reshape(lead + (n_heads * HEAD_W,))


def _round_up(x, m):
    return (x + m - 1) // m * m


def _routing_tables(meta, cnt, n_tiles):
    i32 = jnp.int32
    counts = cnt.reshape(n_tiles, 8, LANES)[:, 0, :N_EXPERTS].astype(i32)
    cnt8 = _round_up(counts, ROW_ALIGN)
    lend = jnp.cumsum(cnt8, axis=1)
    lstart = lend - cnt8
    tot = jnp.sum(cnt8, axis=0)
    padded = _round_up(tot, MOE_ROWS)
    pends = jnp.cumsum(padded)
    pstarts = pends - padded
    goff = pstarts[None, :] + jnp.cumsum(cnt8, axis=0) - cnt8

    tr = meta[:, TOP_K:2 * TOP_K].astype(i32).reshape(n_tiles, TOK_TILE, TOP_K)
    tr_t = jnp.concatenate([jnp.swapaxes(tr, 1, 2), jnp.full((n_tiles, 8 - TOP_K, TOK_TILE), -1, i32)], axis=1)
    g_t = jnp.concatenate([jnp.swapaxes(meta[:, 2 * TOP_K:3 * TOP_K].reshape(n_tiles, TOK_TILE, TOP_K), 1, 2),
                           jnp.zeros((n_tiles, 8 - TOP_K, TOK_TILE), F32)], axis=1)

    prow = jnp.arange(MAX_PIECES, dtype=i32) * ROW_ALIGN
    pe = jnp.sum(prow[None, :, None] >= lend[:, None, :], axis=-1)
    pe = jnp.minimum(pe, N_EXPERTS - 1)
    pick = pe[:, :, None] == jnp.arange(N_EXPERTS, dtype=i32)
    pdst = (jnp.sum(jnp.where(pick, (goff - lstart)[:, None, :], 0), axis=-1) + prow[None, :]).astype(i32)
    n_pieces = (lend[:, -1] // ROW_ALIGN).astype(i32)
    pdst = jnp.where(jnp.arange(MAX_PIECES, dtype=i32)[None, :] < n_pieces[:, None], pdst, 0)

    max_rows = n_tiles * (TOK_TILE * TOP_K + N_EXPERTS * (ROW_ALIGN - 1)) + N_EXPERTS * (MOE_ROWS - ROW_ALIGN)
    n_blocks = -(-max_rows // MOE_ROWS)
    n_used = (pends[-1] // MOE_ROWS).astype(i32)
    blk = jnp.arange(n_blocks, dtype=i32)
    block_e = jnp.minimum(jnp.sum(blk[:, None] * MOE_ROWS >= pends[None, :], axis=-1), N_EXPERTS - 1).astype(i32)
    block_e = jnp.where(blk < n_used, block_e, block_e[jnp.maximum(n_used - 1, 0)])
    first = ((blk < n_used) & ((blk == 0) | (block_e != jnp.roll(block_e, 1)))).astype(i32)
    has_rows = padded > 0
    slot_e = (jnp.cumsum(has_rows.astype(i32)) - 1) % 2
    later = jnp.arange(N_EXPERTS, dtype=i32)[None, :] > jnp.arange(N_EXPERTS, dtype=i32)[:, None]
    nxt_e = jnp.min(jnp.where(later & has_rows[None, :], jnp.arange(N_EXPERTS, dtype=i32)[None, :], N_EXPERTS), axis=1)
    nxt_e = jnp.where(nxt_e < N_EXPERTS, nxt_e, -1).astype(i32)
    tail_info = jnp.stack([pstarts + tot, (padded - tot) // ROW_ALIGN]).astype(i32)
    spare = jnp.stack([n_used, n_blocks - n_used]).astype(i32)
    return (tr.reshape(-1, TOP_K), tr_t, g_t, pdst, n_pieces, tail_info, spare,
            (block_e, n_used.reshape(1), first, slot_e[block_e], nxt_e[block_e]), n_blocks * MOE_ROWS)


def kernel(x, mem, norm_mix_w, w_in, hg_lb_logits, hg_onorm_w, gla_w_gk2, gla_b_gk, gla_onorm_w, w_out, norm_xa_w, norm_mem_w, w_xq, w_xkv, w_xo, norm_moe_w, w_router, b_router, w_gate_up, b_gate_up, w_down, b_down, norm_final_w):
    assert w_in.shape[0] == 1, "single-layer block"
    bsz, seq, d = x.shape
    n_tok = bsz * seq
    gk = GLA_HEADS * GLA_KDIM

    wi = w_in[0]
    w_in_p = jnp.concatenate([
        wi[:, :4 * _W],
        _pad_heads(wi[:, 4 * _W:4 * _W + gk], GLA_HEADS, GLA_KDIM),
        _pad_heads(wi[:, 4 * _W + gk:4 * _W + 2 * gk], GLA_HEADS, GLA_KDIM),
        wi[:, 4 * _W + 2 * gk:4 * _W + 2 * gk + 2 * _W],
        jnp.pad(wi[:, 4 * _W + 2 * gk + 2 * _W:], ((0, 0), (0, LANES - GLA_RANK))),
    ], axis=1).astype(BF16)
    w_gk2_p = jnp.pad(_pad_heads(gla_w_gk2[0], GLA_HEADS, GLA_KDIM),
                      ((0, LANES - GLA_RANK), (0, 0))).astype(BF16)
    b_gk_p = _pad_heads(gla_b_gk, GLA_HEADS, GLA_KDIM)
    wr = jnp.pad(w_router[0], ((0, 0), (0, LANES - N_EXPERTS)))
    wr_hi = wr.astype(BF16)
    wr_lo = (wr - wr_hi.astype(F32)).astype(BF16)
    br_p = jnp.pad(b_router, ((0, 0), (0, LANES - N_EXPERTS)), constant_values=-1e30)

    x1 = _token_mix(x, norm_mix_w, w_in_p, hg_lb_logits, hg_onorm_w, w_gk2_p, b_gk_p, gla_onorm_w,
                    w_out[0].astype(BF16))
    x2, hm, meta, cnt = _xattn_router(x1, mem, norm_xa_w, norm_mem_w, w_xq[0].astype(BF16),
                                      w_xkv[0].astype(BF16), w_xo[0].astype(BF16), norm_moe_w,
                                      wr_hi, wr_lo, br_p)

    n_tiles = n_tok // TOK_TILE
    tr, tr_t, g_t, pdst, n_pieces, tail_info, spare, block_tables, n_rows = _routing_tables(meta, cnt, n_tiles)

    xs, gs = _dispatch(hm.reshape(n_tok, d), pdst, n_pieces, tail_info, spare, tr_t, g_t, n_rows)
    ys = _expert_mlp(xs, gs, *block_tables, w_gate_up[0], b_gate_up[0][:, None, :],
                     w_down[0], b_down[0][:, None, :])
    out = _combine(x2.reshape(n_tok, d), tr, pdst, n_pieces, ys, norm_final_w[None, :])
    return out.reshape(bsz, seq, d)
```

```python
import functools

import jax
import jax.numpy as jnp
from jax import lax
from jax.experimental import pallas as pl
from jax.experimental.pallas import tpu as pltpu

F32 = jnp.float32
BF16 = jnp.bfloat16
EPS = 1e-6
LOG2E = 1.4426950408889634

HEAD_W = 128
HG_HEADS = 4
GLA_HEADS = 4
GLA_KDIM = 64
N_HEADS = HG_HEADS + GLA_HEADS
GLA_RANK = 16
GLA_GATE_NORMALIZER = 16.0
CHUNK = 64
XA_HEADS = 4
N_EXPERTS = 32
TOP_K = 4
SWIGLU_LIMIT = 7.0
SWIGLU_ALPHA = 1.702

LANES = 128
VMEM_LIMIT_BYTES = 56 * 1024 * 1024

MIX_ROWS = 256
TOK_TILE = 256
MOE_ROWS = 512
ROW_ALIGN = 8
STAGE_ROWS = 1280
MAX_PIECES = STAGE_ROWS // ROW_ALIGN
META_ROWS = 16

_W = HG_HEADS * HEAD_W
COL_HQ, COL_HF, COL_HI, COL_HGATE = 0, _W, 2 * _W, 3 * _W
COL_GQ, COL_GK, COL_GV, COL_GGATE = 4 * _W, 5 * _W, 6 * _W, 7 * _W
COL_GLR = 8 * _W
PROJ_W = COL_GLR + LANES


def _rms(x, w):
    return x * lax.rsqrt(jnp.mean(x * x, axis=-1, keepdims=True) + EPS) * w


def _dot(a, b):
    return jnp.dot(a, b, preferred_element_type=F32)


def _dot_nt(a, b):
    return lax.dot_general(a, b, (((1,), (1,)), ((), ())), preferred_element_type=F32)


def _dot_tn(a, b):
    return lax.dot_general(a, b, (((0,), (0,)), ((), ())), preferred_element_type=F32)


def _head_chunk(hd, q_scr, k_scr, b2_scr, v, st_ref, sc_ref, masks):
    sl = slice(hd * HEAD_W, (hd + 1) * HEAD_W)
    q, k, b2 = q_scr[:, sl], k_scr[:, sl], b2_scr[:, sl]

    for s in range(0, CHUNK, 8):
        q_blk, b_blk = q[s:s + 8], b2[s:s + 8]
        for j in range(s, s + 8):
            e = jnp.exp2(b_blk - b2_scr[j:j + 1, sl])
            sc_ref[s:s + 8, j:j + 1] = jnp.sum(q_blk * (k_scr[j:j + 1, sl] * e), axis=-1, keepdims=True)
    scores = jnp.where(masks[0], sc_ref[...], 0.0)

    for lvl, size in enumerate((16, 32, 64)):
        mids = [b2_scr[r:r + 1, sl] for r in range(size // 2 - 1, CHUNK, size)]
        mid = jnp.concatenate([jnp.broadcast_to(m, (size, HEAD_W)) for m in mids], axis=0)
        dist = b2 - mid
        e = jnp.exp2(jnp.minimum(dist, -dist))
        r = _dot_nt((q * e).astype(BF16), (k * e).astype(BF16))
        scores = jnp.where(masks[lvl + 1], r, scores)

    b2_last = b2_scr[CHUNK - 1:CHUNK, sl]
    q_abs = (q * jnp.exp2(b2)).astype(BF16)
    k_end = (k * jnp.exp2(b2_last - b2)).astype(BF16)
    v16 = v.astype(BF16)
    st = st_ref[...]
    o = _dot(scores.astype(BF16), v16) + _dot_nt(q_abs, st.astype(BF16))
    st_ref[...] = st * jnp.exp2(b2_last) + _dot_tn(v16, k_end)
    return o


def _mix_kernel(x_ref, nw_ref, win_ref, lbl_ref, hgw_ref, wgk_ref, bgk_ref, glw_ref, wout_ref,
                o_ref, proj_ref, oall_ref, st_ref, q_scr, k_scr, b2_scr, sc_scr):
    @pl.when(pl.program_id(1) == 0)
    def _():
        st_ref[...] = jnp.zeros_like(st_ref)
        sc_scr[...] = jnp.zeros_like(sc_scr)

    x = x_ref[0]
    h = _rms(x, nw_ref[...]).astype(BF16)
    proj_ref[...] = _dot(h, win_ref[...])

    lbl = lbl_ref[...]
    e = jnp.exp(lbl - jnp.max(lbl, axis=0, keepdims=True))
    lb = e[0:1] / jnp.sum(e, axis=0, keepdims=True)

    ri = lax.broadcasted_iota(jnp.int32, (CHUNK, CHUNK), 0)
    ci = lax.broadcasted_iota(jnp.int32, (CHUNK, CHUNK), 1)
    tri = jnp.where(ri >= ci, 1.0, 0.0).astype(BF16)
    masks = [(ri // 8 == ci // 8) & (ri >= ci)]
    for size in (16, 32, 64):
        masks.append((ri // size == ci // size) & (ri // (size // 2) > ci // (size // 2)))

    def chunk(ck):
        rows = pl.ds(ck * CHUNK, CHUNK)
        f = lb + (1.0 - lb) * jax.nn.sigmoid(proj_ref[rows, COL_HF:COL_HF + _W])
        g_hg = jnp.log(f)
        z = _dot(proj_ref[rows, COL_GLR:COL_GLR + LANES].astype(BF16), wgk_ref[...]) + bgk_ref[...]
        g_gla = (jnp.minimum(z, 0.0) - jnp.log1p(jnp.exp(-jnp.abs(z)))) * (1.0 / GLA_GATE_NORMALIZER)
        g_all = jnp.concatenate([g_hg, g_gla], axis=-1)
        g_hi = g_all.astype(BF16)
        g_lo = (g_all - g_hi.astype(F32)).astype(BF16)
        b2_scr[...] = (_dot(tri, g_hi) + _dot(tri, g_lo)) * LOG2E
        hq = proj_ref[rows, COL_HQ:COL_HQ + _W]
        q_scr[:, 0:_W] = hq * jax.nn.sigmoid(hq)
        q_scr[:, _W:2 * _W] = proj_ref[rows, COL_GQ:COL_GQ + _W] * (GLA_KDIM ** -0.5)
        k_scr[:, 0:_W] = 1.0 - f
        k_scr[:, _W:2 * _W] = proj_ref[rows, COL_GK:COL_GK + _W]

        for hd in range(N_HEADS):
            lo = hd * HEAD_W
            col_v = COL_HI + lo if hd < HG_HEADS else COL_GV + lo - _W
            o = _head_chunk(hd, q_scr, k_scr, b2_scr, proj_ref[rows, col_v:col_v + HEAD_W],
                            st_ref.at[hd], sc_scr.at[hd], masks)
            oall_ref[rows, lo:lo + HEAD_W] = o

    for ck in range(MIX_ROWS // CHUNK):
        chunk(ck)

    ys = []
    for hd in range(N_HEADS):
        lo = hd * HEAD_W
        o = oall_ref[:, lo:lo + HEAD_W]
        if hd < HG_HEADS:
            w, gate = hgw_ref[...], proj_ref[:, COL_HGATE + lo:COL_HGATE + lo + HEAD_W]
        else:
            w, gate = glw_ref[...], proj_ref[:, COL_GGATE + lo - _W:COL_GGATE + lo - _W + HEAD_W]
        ys.append((_rms(o, w) * (gate * jax.nn.sigmoid(gate))).astype(BF16))
    y = jnp.concatenate(ys, axis=-1)
    o_ref[0] = x + _dot(y, wout_ref[...])


def _token_mix(x, norm_w, w_in_p, lb_logits, hg_onorm_w, w_gk2_p, b_gk_p, gla_onorm_w, w_out):
    bsz, seq, d = x.shape
    const = lambda shape: pl.BlockSpec(shape, lambda b, t: (0,) * len(shape))
    return pl.pallas_call(
        _mix_kernel,
        grid=(bsz, seq // MIX_ROWS),
        in_specs=[
            pl.BlockSpec((1, MIX_ROWS, d), lambda b, t: (b, t, 0)),
            const((1, d)),
            const((d, PROJ_W)),
            const(lb_logits.shape),
            const((1, HEAD_W)),
            const((LANES, _W)),
            const((1, _W)),
            const((1, HEAD_W)),
            const((2 * _W, d)),
        ],
        out_specs=pl.BlockSpec((1, MIX_ROWS, d), lambda b, t: (b, t, 0)),
        out_shape=jax.ShapeDtypeStruct((bsz, seq, d), F32),
        scratch_shapes=[
            pltpu.VMEM((MIX_ROWS, PROJ_W), F32),
            pltpu.VMEM((MIX_ROWS, N_HEADS * HEAD_W), F32),
            pltpu.VMEM((N_HEADS, HEAD_W, HEAD_W), F32),
            pltpu.VMEM((CHUNK, N_HEADS * HEAD_W), F32),
            pltpu.VMEM((CHUNK, N_HEADS * HEAD_W), F32),
            pltpu.VMEM((CHUNK, N_HEADS * HEAD_W), F32),
            pltpu.VMEM((N_HEADS, CHUNK, CHUNK), F32),
        ],
        compiler_params=pltpu.CompilerParams(
            dimension_semantics=("arbitrary", "arbitrary"), vmem_limit_bytes=VMEM_LIMIT_BYTES),
        name="token_mix",
    )(x, norm_w, w_in_p, lb_logits, hg_onorm_w, w_gk2_p, b_gk_p, gla_onorm_w, w_out)


def _xattn_tile(x, k_scr, v_scr, nxw_ref, wq_ref, wo_ref, nmoe_ref, wr_hi_ref, wr_lo_ref, br_ref):
    d = x.shape[-1]
    hdim = d // XA_HEADS
    q = _dot(_rms(x, nxw_ref[...]).astype(BF16), wq_ref[...])
    outs = []
    for h in range(XA_HEADS):
        sl = slice(h * hdim, (h + 1) * hdim)
        s = _dot_nt(q[:, sl].astype(BF16), k_scr[:, sl]) * (hdim ** -0.5)
        p = jnp.exp(s - jnp.max(s, axis=-1, keepdims=True))
        p = p / jnp.sum(p, axis=-1, keepdims=True)
        outs.append(_dot(p.astype(BF16), v_scr[:, sl]).astype(BF16))
    x2 = x + _dot(jnp.concatenate(outs, axis=-1), wo_ref[...])

    hm = _rms(x2, nmoe_ref[...])
    hm_hi = hm.astype(BF16)

    hm_lo = (hm - hm_hi.astype(F32)).astype(BF16)
    logits = (_dot(hm_hi, wr_hi_ref[...]) + _dot(hm_lo, wr_hi_ref[...]) + _dot(hm_hi, wr_lo_ref[...])
              + br_ref[...])

    rows = logits.shape[0]
    lane = lax.broadcasted_iota(jnp.int32, (rows, LANES), 1)
    lane_f = lane.astype(F32)
    neg_inf = jnp.float32(-jnp.inf)
    top_v, top_i, hots = [], [], []
    work = logits
    for _ in range(TOP_K):
        m = jnp.max(work, axis=-1, keepdims=True)
        idx = jnp.min(jnp.where(work == m, lane_f, float(LANES)), axis=-1, keepdims=True)
        hot = lane_f == idx
        work = jnp.where(hot, neg_inf, work)
        top_v.append(m)
        top_i.append(idx)
        hots.append(hot)
    es = [jnp.exp(v - top_v[0]) for v in top_v]
    denom = es[0] + es[1] + es[2] + es[3]
    gates = [e / denom for e in es]

    chosen = jnp.where(hots[0] | hots[1] | hots[2] | hots[3], 1.0, 0.0)
    ri = lax.broadcasted_iota(jnp.int32, (rows, rows), 0)
    ci = lax.broadcasted_iota(jnp.int32, (rows, rows), 1)
    strict = jnp.where(ri > ci, 1.0, 0.0).astype(BF16)
    before = _dot(strict, chosen.astype(BF16))
    counts = jnp.sum(chosen, axis=0, keepdims=True)
    padded = jnp.floor((counts + (ROW_ALIGN - 1.0)) * (1.0 / ROW_ALIGN)) * ROW_ALIGN
    li = lax.broadcasted_iota(jnp.int32, (LANES, LANES), 0)
    lj = lax.broadcasted_iota(jnp.int32, (LANES, LANES), 1)
    group_start = _dot(jnp.broadcast_to(padded, (8, LANES)).astype(BF16),
                       jnp.where(li < lj, 1.0, 0.0).astype(BF16))[0:1]
    slot_of = before + group_start
    slots = [jnp.sum(jnp.where(hot, slot_of, 0.0), axis=-1, keepdims=True) for hot in hots]

    meta = jnp.zeros((rows, LANES), F32)
    for j, col in enumerate(top_i + slots + gates):
        meta = jnp.where(lane == j, col, meta)
    return x2, hm_hi, meta, counts


def _xattn_kernel(x_ref, mem_ref, nxw_ref, nmw_ref, wq_ref, wkv_ref, wo_ref, nmoe_ref,
                  wr_hi_ref, wr_lo_ref, br_ref,
                  x2_ref, hm_ref, meta_ref, meta_t_ref, cnt_ref, k_scr, v_scr):
    d = x_ref.shape[-1]

    @pl.when(pl.program_id(1) == 0)
    def _():
        m = _rms(mem_ref[0], nmw_ref[...]).astype(BF16)
        kv = _dot(m, wkv_ref[...])
        k_scr[...] = kv[:, :d].astype(BF16)
        v_scr[...] = kv[:, d:].astype(BF16)

    for u in range(2):
        rows = slice(u * TOK_TILE, (u + 1) * TOK_TILE)
        x2, hm16, meta, counts = _xattn_tile(x_ref[0, rows, :], k_scr, v_scr, nxw_ref, wq_ref, wo_ref, nmoe_ref,
                                             wr_hi_ref, wr_lo_ref, br_ref)
        x2_ref[0, rows, :] = x2
        hm_ref[0, rows, :] = hm16
        meta_ref[rows, :] = meta
        meta_t_ref[u] = meta.T[0:META_ROWS, :]
        cnt_ref[u * 8:(u + 1) * 8, :] = jnp.broadcast_to(counts, (8, LANES))


def _xattn_router(x1, mem, norm_xa_w, norm_mem_w, w_xq, w_xkv, w_xo, norm_moe_w, wr_hi, wr_lo, br_p):
    bsz, seq, d = x1.shape
    mlen = mem.shape[1]
    n_t = seq // (2 * TOK_TILE)
    const = lambda shape: pl.BlockSpec(shape, lambda b, t: (0,) * len(shape))
    return pl.pallas_call(
        _xattn_kernel,
        grid=(bsz, n_t),
        in_specs=[
            pl.BlockSpec((1, 2 * TOK_TILE, d), lambda b, t: (b, t, 0)),
            pl.BlockSpec((1, mlen, d), lambda b, t: (b, 0, 0)),
            const((1, d)), const((1, d)),
            const((d, d)), const((d, 2 * d)), const((d, d)),
            const((1, d)),
            const((d, LANES)), const((d, LANES)), const((1, LANES)),
        ],
        out_specs=[
            pl.BlockSpec((1, 2 * TOK_TILE, d), lambda b, t: (b, t, 0)),
            pl.BlockSpec((1, 2 * TOK_TILE, d), lambda b, t: (b, t, 0)),
            pl.BlockSpec((2 * TOK_TILE, LANES), lambda b, t: (b * n_t + t, 0)),
            pl.BlockSpec((2, META_ROWS, TOK_TILE), lambda b, t: (b * n_t + t, 0, 0)),
            pl.BlockSpec((16, LANES), lambda b, t: (b * n_t + t, 0)),
        ],
        out_shape=[
            jax.ShapeDtypeStruct((bsz, seq, d), F32),
            jax.ShapeDtypeStruct((bsz, seq, d), BF16),
            jax.ShapeDtypeStruct((bsz * seq, LANES), F32),
            jax.ShapeDtypeStruct((bsz * seq // TOK_TILE, META_ROWS, TOK_TILE), F32),
            jax.ShapeDtypeStruct((bsz * seq // TOK_TILE * 8, LANES), F32),
        ],
        scratch_shapes=[
            pltpu.VMEM((mlen, d), BF16),
            pltpu.VMEM((mlen, d), BF16),
        ],
        compiler_params=pltpu.CompilerParams(
            dimension_semantics=("arbitrary", "arbitrary"), vmem_limit_bytes=VMEM_LIMIT_BYTES),
        name="xattn_router",
    )(x1, mem, norm_xa_w, norm_mem_w, w_xq, w_xkv, w_xo, norm_moe_w, wr_hi, wr_lo, br_p)


def _piece_copy(src_ref, src_row, dst_ref, dst_row, sem):
    return pltpu.make_async_copy(src_ref.at[pl.ds(pl.multiple_of(src_row, ROW_ALIGN), ROW_ALIGN)],
                                 dst_ref.at[pl.ds(pl.multiple_of(dst_row, ROW_ALIGN), ROW_ALIGN)], sem)


def _block_copy(src_ref, dst_ref, dst_block, sem):
    return pltpu.make_async_copy(src_ref, dst_ref.at[pl.ds(pl.multiple_of(dst_block * MOE_ROWS, MOE_ROWS), MOE_ROWS)], sem)


def _dispatch_kernel(pdst_ref, np_ref, tail_ref, cap_ref, mt_ref, hm_ref, xs_ref, stage_ref, zero_ref, sem, sem_z):
    s = pl.program_id(0)
    last = pl.num_programs(0) - 1
    d = hm_ref.shape[1]

    def spare_blocks(fn):
        def body(i, c):
            fn(_block_copy(zero_ref, xs_ref, cap_ref[0] + i, sem_z))
            return c
        lax.fori_loop(0, cap_ref[1], body, 0)

    @pl.when(s == 0)
    def _():
        zero_ref[...] = jnp.zeros_like(zero_ref)
        spare_blocks(lambda cp: cp.start())

        def per_expert(e, carry):
            n = tail_ref[1, e]
            off = tail_ref[0, e]

            def start(i, c):
                _piece_copy(zero_ref, 0, xs_ref, off + i * ROW_ALIGN, sem.at[0]).start()
                return c

            def wait(i, c):
                _piece_copy(zero_ref, 0, xs_ref, 0, sem.at[0]).wait()
                return c

            lax.fori_loop(0, n, start, 0)
            lax.fori_loop(0, n, wait, 0)
            return carry

        lax.fori_loop(0, N_EXPERTS, per_expert, 0)

    def drain(u, tile):
        def wait(q, c):
            _piece_copy(stage_ref.at[u], 0, xs_ref, 0, sem.at[u]).wait()
            return c
        lax.fori_loop(0, np_ref[tile], wait, 0)

    for u in range(2):
        tile = 2 * s + u

        @pl.when(s > 0)
        def _():
            drain(u, tile - 2)

        x16 = hm_ref[u * TOK_TILE:(u + 1) * TOK_TILE, :]
        tr = mt_ref[u, TOP_K:2 * TOP_K, :].astype(jnp.int32)
        g = mt_ref[u, 2 * TOP_K:3 * TOP_K, :]
        for c in range(STAGE_ROWS // TOK_TILE):
            rows = slice(c * TOK_TILE, (c + 1) * TOK_TILE)
            rid = lax.broadcasted_iota(jnp.int32, (TOK_TILE, TOK_TILE), 0) + c * TOK_TILE
            hot = [rid == tr[k:k + 1, :] for k in range(TOP_K)]
            sel = jnp.where(hot[0] | hot[1] | hot[2] | hot[3], 1.0, 0.0).astype(BF16)
            stage_ref[u, rows, 0:d] = _dot(sel, x16)
            gsum = jnp.zeros((TOK_TILE, 1), F32)
            for k in range(TOP_K):
                gsum = gsum + jnp.sum(jnp.where(hot[k], g[k:k + 1, :], 0.0), axis=-1, keepdims=True)
            stage_ref[u, rows, d:d + LANES] = jnp.broadcast_to(gsum, (TOK_TILE, LANES))

        def start(q, c):
            _piece_copy(stage_ref.at[u], q * ROW_ALIGN, xs_ref, pdst_ref[tile, q], sem.at[u]).start()
            return c

        lax.fori_loop(0, np_ref[tile], start, 0)

    @pl.when(s == last)
    def _():
        drain(0, 2 * s)
        drain(1, 2 * s + 1)
        spare_blocks(lambda cp: cp.wait())


def _dispatch(hm, pdst, n_pieces, tail_info, spare, meta_t, n_rows):
    n_tok, d = hm.shape
    n_tiles = n_tok // TOK_TILE
    assert n_tiles % 2 == 0
    smem = pl.BlockSpec(memory_space=pltpu.SMEM)
    return pl.pallas_call(
        _dispatch_kernel,
        grid=(n_tiles // 2,),
        in_specs=[
            smem, smem, smem, smem,
            pl.BlockSpec((2, META_ROWS, TOK_TILE), lambda s: (s, 0, 0)),
            pl.BlockSpec((2 * TOK_TILE, d), lambda s: (s, 0)),
        ],
        out_specs=pl.BlockSpec(memory_space=pl.ANY),
        out_shape=jax.ShapeDtypeStruct((n_rows, d + LANES), F32),
        scratch_shapes=[pltpu.VMEM((2, STAGE_ROWS, d + LANES), F32), pltpu.VMEM((MOE_ROWS, d + LANES), F32),
                        pltpu.SemaphoreType.DMA((2,)), pltpu.SemaphoreType.DMA],
        compiler_params=pltpu.CompilerParams(
            dimension_semantics=("arbitrary",), vmem_limit_bytes=VMEM_LIMIT_BYTES),
        name="moe_dispatch",
    )(pdst, n_pieces, tail_info, spare, meta_t, hm)


def _expert_kernel(be_ref, nu_ref, first_ref, slot_ref, nxt_ref, xs_ref, wgu_hbm, bgu_ref, wd_hbm,
                   bd_ref, ys_ref, wgu32_ref, wd32_ref, wgu16_ref, wd16_ref, sem):
    i = pl.program_id(0)
    f, d = wd16_ref.shape

    def weight_copies(e, slot):
        return (pltpu.make_async_copy(wgu_hbm.at[e], wgu32_ref.at[slot], sem.at[0, slot]),
                pltpu.make_async_copy(wd_hbm.at[e], wd32_ref.at[slot], sem.at[1, slot]))

    @pl.when(i == 0)
    def _():
        for cp in weight_copies(be_ref[0], 0):
            cp.start()

    @pl.when(first_ref[i] == 1)
    def _():
        slot = slot_ref[i]
        for cp in weight_copies(be_ref[i], slot):
            cp.wait()

        @pl.when(nxt_ref[i] >= 0)
        def _():
            for cp in weight_copies(nxt_ref[i], 1 - slot):
                cp.start()

        wgu16_ref[...] = wgu32_ref[slot].astype(BF16)
        wd16_ref[...] = wd32_ref[slot].astype(BF16)

    @pl.when(i < nu_ref[0])
    def _():
        gu = _dot(xs_ref[:, 0:d].astype(BF16), wgu16_ref[...]) + bgu_ref[0]
        gate = jnp.minimum(gu[:, :f], SWIGLU_LIMIT)
        up = jnp.clip(gu[:, f:], -SWIGLU_LIMIT, SWIGLU_LIMIT)
        act = (up + 1.0) * gate * jax.nn.sigmoid(SWIGLU_ALPHA * gate)
        ys_ref[...] = (_dot(act.astype(BF16), wd16_ref[...]) + bd_ref[0]) * xs_ref[:, d:d + 1]

    @pl.when(i >= nu_ref[0])
    def _():
        ys_ref[...] = jnp.zeros_like(ys_ref)


def _expert_mlp(xs, block_e, n_used, first, slot, nxt, w_gu, b_gu, w_d, b_d):
    n_rows = xs.shape[0]
    d = w_d.shape[2]
    n_blocks = n_rows // MOE_ROWS
    f = w_d.shape[1]
    row_map = lambda i, be, nu, *_: (jnp.minimum(i, nu[0] - 1), 0)
    exp_map = lambda i, be, *_: (be[i], 0, 0)
    return pl.pallas_call(
        _expert_kernel,
        grid_spec=pltpu.PrefetchScalarGridSpec(
            num_scalar_prefetch=5,
            grid=(n_blocks,),
            in_specs=[
                pl.BlockSpec((MOE_ROWS, d + LANES), row_map),
                pl.BlockSpec(memory_space=pl.ANY),
                pl.BlockSpec((1, 1, 2 * f), exp_map),
                pl.BlockSpec(memory_space=pl.ANY),
                pl.BlockSpec((1, 1, d), exp_map),
            ],
            out_specs=pl.BlockSpec((MOE_ROWS, d), lambda i, *_: (i, 0)),
            scratch_shapes=[
                pltpu.VMEM((2, d, 2 * f), F32), pltpu.VMEM((2, f, d), F32),
                pltpu.VMEM((d, 2 * f), BF16), pltpu.VMEM((f, d), BF16),
                pltpu.SemaphoreType.DMA((2, 2)),
            ],
        ),
        out_shape=jax.ShapeDtypeStruct((n_rows, d), F32),
        compiler_params=pltpu.CompilerParams(
            dimension_semantics=("arbitrary",), vmem_limit_bytes=VMEM_LIMIT_BYTES),
        name="moe_experts",
    )(block_e, n_used, first, slot, nxt, xs, w_gu, b_gu, w_d, b_d)


def _combine_kernel(pdst_ref, np_ref, x2_ref, meta_ref, nfw_ref, ys_ref, out_ref, ybuf_ref, sem):
    s = pl.program_id(0)
    last = pl.num_programs(0) - 1

    def fetch(u, tile):
        def start(q, c):
            _piece_copy(ys_ref, pdst_ref[tile, q], ybuf_ref.at[u], q * ROW_ALIGN, sem.at[u]).start()
            return c
        lax.fori_loop(0, np_ref[tile], start, 0)

    def finish(u, tile):
        def wait(q, c):
            _piece_copy(ys_ref, 0, ybuf_ref.at[u], 0, sem.at[u]).wait()
            return c
        lax.fori_loop(0, np_ref[tile], wait, 0)

        rows = slice(u * TOK_TILE, (u + 1) * TOK_TILE)
        tr = meta_ref[rows, TOP_K:2 * TOP_K].astype(jnp.int32)
        acc = x2_ref[rows, :]
        for c in range(STAGE_ROWS // TOK_TILE):
            cid = lax.broadcasted_iota(jnp.int32, (TOK_TILE, TOK_TILE), 1) + c * TOK_TILE
            hot = [cid == tr[:, k:k + 1] for k in range(TOP_K)]
            sel = jnp.where(hot[0] | hot[1] | hot[2] | hot[3], 1.0, 0.0).astype(BF16)
            acc = acc + _dot(sel, ybuf_ref[u, c * TOK_TILE:(c + 1) * TOK_TILE, :].astype(BF16))
        out_ref[rows, :] = _rms(acc, nfw_ref[...])

    @pl.when(s == 0)
    def _():
        ybuf_ref[...] = jnp.zeros_like(ybuf_ref)
        fetch(0, 0)

    fetch(1, 2 * s + 1)
    finish(0, 2 * s)

    @pl.when(s < last)
    def _():
        fetch(0, 2 * s + 2)

    finish(1, 2 * s + 1)


def _combine(x2, meta, pdst, n_pieces, ys, norm_final_w):
    n_tok, d = x2.shape
    n_tiles = n_tok // TOK_TILE
    assert n_tiles % 2 == 0
    smem = pl.BlockSpec(memory_space=pltpu.SMEM)
    return pl.pallas_call(
        _combine_kernel,
        grid=(n_tiles // 2,),
        in_specs=[
            smem, smem,
            pl.BlockSpec((2 * TOK_TILE, d), lambda s: (s, 0)),
            pl.BlockSpec((2 * TOK_TILE, LANES), lambda s: (s, 0)),
            pl.BlockSpec((1, d), lambda s: (0, 0)),
            pl.BlockSpec(memory_space=pl.ANY),
        ],
        out_specs=pl.BlockSpec((2 * TOK_TILE, d), lambda s: (s, 0)),
        out_shape=jax.ShapeDtypeStruct((n_tok, d), F32),
        scratch_shapes=[pltpu.VMEM((2, STAGE_ROWS, d), F32), pltpu.SemaphoreType.DMA((2,))],
        compiler_params=pltpu.CompilerParams(
            dimension_semantics=("arbitrary",), vmem_limit_bytes=VMEM_LIMIT_BYTES),
        name="moe_combine",
    )(pdst, n_pieces, x2, meta, norm_final_w, ys)


def _pad_heads(w, n_heads, width):
    lead = w.shape[:-1]
    w = w.reshape(lead + (n_heads, width))
    w = jnp.pad(w, [(0, 0)] * len(lead) + [(0, 0), (0, HEAD_W - width)])
    return w.reshape(lead + (n_heads * HEAD_W,))


def _round_up(x, m):
    return (x + m - 1) // m * m


def _routing_tables(cnt, n_tiles):
    i32 = jnp.int32
    counts = cnt.reshape(n_tiles, 8, LANES)[:, 0, :N_EXPERTS].astype(i32)
    cnt8 = _round_up(counts, ROW_ALIGN)
    lend = jnp.cumsum(cnt8, axis=1)
    lstart = lend - cnt8
    tot = jnp.sum(cnt8, axis=0)
    padded = _round_up(tot, MOE_ROWS)
    pends = jnp.cumsum(padded)
    pstarts = pends - padded
    goff = pstarts[None, :] + jnp.cumsum(cnt8, axis=0) - cnt8

    prow = jnp.arange(MAX_PIECES, dtype=i32) * ROW_ALIGN
    pe = jnp.sum(prow[None, :, None] >= lend[:, None, :], axis=-1)
    pe = jnp.minimum(pe, N_EXPERTS - 1)
    pick = pe[:, :, None] == jnp.arange(N_EXPERTS, dtype=i32)
    pdst = (jnp.sum(jnp.where(pick, (goff - lstart)[:, None, :], 0), axis=-1) + prow[None, :]).astype(i32)
    n_pieces = (lend[:, -1] // ROW_ALIGN).astype(i32)
    pdst = jnp.where(jnp.arange(MAX_PIECES, dtype=i32)[None, :] < n_pieces[:, None], pdst, 0)

    max_rows = n_tiles * (TOK_TILE * TOP_K + N_EXPERTS * (ROW_ALIGN - 1)) + N_EXPERTS * (MOE_ROWS - ROW_ALIGN)
    n_blocks = -(-max_rows // MOE_ROWS)
    n_used = (pends[-1] // MOE_ROWS).astype(i32)
    blk = jnp.arange(n_blocks, dtype=i32)
    block_e = jnp.minimum(jnp.sum(blk[:, None] * MOE_ROWS >= pends[None, :], axis=-1), N_EXPERTS - 1).astype(i32)
    block_e = jnp.where(blk < n_used, block_e, block_e[jnp.maximum(n_used - 1, 0)])
    first = ((blk < n_used) & ((blk == 0) | (block_e != jnp.roll(block_e, 1)))).astype(i32)
    has_rows = padded > 0
    slot_e = (jnp.cumsum(has_rows.astype(i32)) - 1) % 2
    later = jnp.arange(N_EXPERTS, dtype=i32)[None, :] > jnp.arange(N_EXPERTS, dtype=i32)[:, None]
    nxt_e = jnp.min(jnp.where(later & has_rows[None, :], jnp.arange(N_EXPERTS, dtype=i32)[None, :], N_EXPERTS), axis=1)
    nxt_e = jnp.where(nxt_e < N_EXPERTS, nxt_e, -1).astype(i32)
    tail_info = jnp.stack([pstarts + tot, (padded - tot) // ROW_ALIGN]).astype(i32)
    spare = jnp.stack([n_used, n_blocks - n_used]).astype(i32)
    return (pdst, n_pieces, tail_info, spare,
            (block_e, n_used.reshape(1), first, slot_e[block_e], nxt_e[block_e]), n_blocks * MOE_ROWS)


def kernel(x, mem, norm_mix_w, w_in, hg_lb_logits, hg_onorm_w, gla_w_gk2, gla_b_gk, gla_onorm_w, w_out, norm_xa_w, norm_mem_w, w_xq, w_xkv, w_xo, norm_moe_w, w_router, b_router, w_gate_up, b_gate_up, w_down, b_down, norm_final_w):
    assert w_in.shape[0] == 1, "single-layer block"
    bsz, seq, d = x.shape
    n_tok = bsz * seq
    gk = GLA_HEADS * GLA_KDIM

    wi = w_in[0]
    w_in_p = jnp.concatenate([
        wi[:, :4 * _W],
        _pad_heads(wi[:, 4 * _W:4 * _W + gk], GLA_HEADS, GLA_KDIM),
        _pad_heads(wi[:, 4 * _W + gk:4 * _W + 2 * gk], GLA_HEADS, GLA_KDIM),
        wi[:, 4 * _W + 2 * gk:4 * _W + 2 * gk + 2 * _W],
        jnp.pad(wi[:, 4 * _W + 2 * gk + 2 * _W:], ((0, 0), (0, LANES - GLA_RANK))),
    ], axis=1).astype(BF16)
    w_gk2_p = jnp.pad(_pad_heads(gla_w_gk2[0], GLA_HEADS, GLA_KDIM),
                      ((0, LANES - GLA_RANK), (0, 0))).astype(BF16)
    b_gk_p = _pad_heads(gla_b_gk, GLA_HEADS, GLA_KDIM)
    wr = jnp.pad(w_router[0], ((0, 0), (0, LANES - N_EXPERTS)))
    wr_hi = wr.astype(BF16)
    wr_lo = (wr - wr_hi.astype(F32)).astype(BF16)
    br_p = jnp.pad(b_router, ((0, 0), (0, LANES - N_EXPERTS)), constant_values=-1e30)

    x1 = _token_mix(x, norm_mix_w, w_in_p, hg_lb_logits, hg_onorm_w, w_gk2_p, b_gk_p, gla_onorm_w,
                    w_out[0].astype(BF16))
    x2, hm, meta, meta_t, cnt = _xattn_router(x1, mem, norm_xa_w, norm_mem_w, w_xq[0].astype(BF16),
                                              w_xkv[0].astype(BF16), w_xo[0].astype(BF16), norm_moe_w,
                                              wr_hi, wr_lo, br_p)

    n_tiles = n_tok // TOK_TILE
    pdst, n_pieces, tail_info, spare, block_tables, n_rows = _routing_tables(cnt, n_tiles)

    xs = _dispatch(hm.reshape(n_tok, d), pdst, n_pieces, tail_info, spare, meta_t, n_rows)
    ys = _expert_mlp(xs, *block_tables, w_gate_up[0], b_gate_up[0][:, None, :],
                     w_down[0], b_down[0][:, None, :])
    out = _combine(x2.reshape(n_tok, d), meta, pdst, n_pieces, ys, norm_final_w[None, :])
    return out.reshape(bsz, seq, d)
```

```python
import functools

import jax
import jax.numpy as jnp
from jax import lax
from jax.experimental import pallas as pl
from jax.experimental.pallas import tpu as pltpu

F32 = jnp.float32
BF16 = jnp.bfloat16
EPS = 1e-6
LOG2E = 1.4426950408889634

HEAD_W = 128
HG_HEADS = 4
GLA_HEADS = 4
GLA_KDIM = 64
N_HEADS = HG_HEADS + GLA_HEADS
GLA_RANK = 16
GLA_GATE_NORMALIZER = 16.0
CHUNK = 64
XA_HEADS = 4
N_EXPERTS = 32
TOP_K = 4
SWIGLU_LIMIT = 7.0
SWIGLU_ALPHA = 1.702

LANES = 128
VMEM_LIMIT_BYTES = 56 * 1024 * 1024

MIX_ROWS = 256
TOK_TILE = 256
MOE_ROWS = 512
ROW_ALIGN = 8
STAGE_ROWS = 1280
MAX_PIECES = STAGE_ROWS // ROW_ALIGN
META_ROWS = 16

_W = HG_HEADS * HEAD_W
COL_HQ, COL_HF, COL_HI, COL_HGATE = 0, _W, 2 * _W, 3 * _W
COL_GQ, COL_GK, COL_GV, COL_GGATE = 4 * _W, 5 * _W, 6 * _W, 7 * _W
COL_GLR = 8 * _W
PROJ_W = COL_GLR + LANES


def _rms(x, w):
    return x * lax.rsqrt(jnp.mean(x * x, axis=-1, keepdims=True) + EPS) * w


def _dot(a, b):
    return jnp.dot(a, b, preferred_element_type=F32)


def _dot_nt(a, b):
    return lax.dot_general(a, b, (((1,), (1,)), ((), ())), preferred_element_type=F32)


def _dot_tn(a, b):
    return lax.dot_general(a, b, (((0,), (0,)), ((), ())), preferred_element_type=F32)


_HI16 = 0xFFFF0000


def _pack_halves(x, rounded):
    n = x.shape[1] // 2
    u = lax.bitcast_convert_type(x, jnp.uint32)
    if not rounded:
        u = u + jnp.uint32(0x7FFF) + ((u >> 16) & jnp.uint32(1))
    return (u[:, n:] & jnp.uint32(_HI16)) | (u[:, :n] >> 16)


def _unpack_halves(u):
    lo = lax.bitcast_convert_type(u << 16, F32).astype(BF16)
    hi = lax.bitcast_convert_type(u & jnp.uint32(_HI16), F32).astype(BF16)
    return lo, hi


def _head_chunk(hd, q_scr, k_scr, b2_scr, v, st_ref, sc_ref, masks):
    sl = slice(hd * HEAD_W, (hd + 1) * HEAD_W)
    q, k, b2 = q_scr[:, sl], k_scr[:, sl], b2_scr[:, sl]

    for s in range(0, CHUNK, 8):
        q_blk, b_blk = q[s:s + 8], b2[s:s + 8]
        for j in range(s, s + 8):
            e = jnp.exp2(b_blk - b2_scr[j:j + 1, sl])
            sc_ref[s:s + 8, j:j + 1] = jnp.sum(q_blk * (k_scr[j:j + 1, sl] * e), axis=-1, keepdims=True)
    scores = jnp.where(masks[0], sc_ref[...], 0.0)

    for lvl, size in enumerate((16, 32, 64)):
        mids = [b2_scr[r:r + 1, sl] for r in range(size // 2 - 1, CHUNK, size)]
        mid = jnp.concatenate([jnp.broadcast_to(m, (size, HEAD_W)) for m in mids], axis=0)
        dist = b2 - mid
        e = jnp.exp2(jnp.minimum(dist, -dist))
        r = _dot_nt((q * e).astype(BF16), (k * e).astype(BF16))
        scores = jnp.where(masks[lvl + 1], r, scores)

    b2_last = b2_scr[CHUNK - 1:CHUNK, sl]
    q_abs = (q * jnp.exp2(b2)).astype(BF16)
    k_end = (k * jnp.exp2(b2_last - b2)).astype(BF16)
    v16 = v.astype(BF16)
    st = st_ref[...]
    o = _dot(scores.astype(BF16), v16) + _dot_nt(q_abs, st.astype(BF16))
    st_ref[...] = st * jnp.exp2(b2_last) + _dot_tn(v16, k_end)
    return o


def _mix_kernel(x_ref, nw_ref, win_ref, lbl_ref, hgw_ref, wgk_ref, bgk_ref, glw_ref, wout_ref,
                o_ref, proj_ref, oall_ref, st_ref, q_scr, k_scr, b2_scr, sc_scr):
    @pl.when(pl.program_id(1) == 0)
    def _():
        st_ref[...] = jnp.zeros_like(st_ref)
        sc_scr[...] = jnp.zeros_like(sc_scr)

    x = x_ref[0]
    h = _rms(x, nw_ref[...]).astype(BF16)
    proj_ref[...] = _dot(h, win_ref[...])

    lbl = lbl_ref[...]
    e = jnp.exp(lbl - jnp.max(lbl, axis=0, keepdims=True))
    lb = e[0:1] / jnp.sum(e, axis=0, keepdims=True)

    ri = lax.broadcasted_iota(jnp.int32, (CHUNK, CHUNK), 0)
    ci = lax.broadcasted_iota(jnp.int32, (CHUNK, CHUNK), 1)
    tri = jnp.where(ri >= ci, 1.0, 0.0).astype(BF16)
    masks = [(ri // 8 == ci // 8) & (ri >= ci)]
    for size in (16, 32, 64):
        masks.append((ri // size == ci // size) & (ri // (size // 2) > ci // (size // 2)))

    def chunk(ck):
        rows = pl.ds(ck * CHUNK, CHUNK)
        f = lb + (1.0 - lb) * jax.nn.sigmoid(proj_ref[rows, COL_HF:COL_HF + _W])
        g_hg = jnp.log(f)
        z = _dot(proj_ref[rows, COL_GLR:COL_GLR + LANES].astype(BF16), wgk_ref[...]) + bgk_ref[...]
        g_gla = (jnp.minimum(z, 0.0) - jnp.log1p(jnp.exp(-jnp.abs(z)))) * (1.0 / GLA_GATE_NORMALIZER)
        g_all = jnp.concatenate([g_hg, g_gla], axis=-1)
        g_hi = g_all.astype(BF16)
        g_lo = (g_all - g_hi.astype(F32)).astype(BF16)
        b2_scr[...] = (_dot(tri, g_hi) + _dot(tri, g_lo)) * LOG2E
        hq = proj_ref[rows, COL_HQ:COL_HQ + _W]
        q_scr[:, 0:_W] = hq * jax.nn.sigmoid(hq)
        q_scr[:, _W:2 * _W] = proj_ref[rows, COL_GQ:COL_GQ + _W] * (GLA_KDIM ** -0.5)
        k_scr[:, 0:_W] = 1.0 - f
        k_scr[:, _W:2 * _W] = proj_ref[rows, COL_GK:COL_GK + _W]

        for hd in range(N_HEADS):
            lo = hd * HEAD_W
            col_v = COL_HI + lo if hd < HG_HEADS else COL_GV + lo - _W
            o = _head_chunk(hd, q_scr, k_scr, b2_scr, proj_ref[rows, col_v:col_v + HEAD_W],
                            st_ref.at[hd], sc_scr.at[hd], masks)
            oall_ref[rows, lo:lo + HEAD_W] = o

    for ck in range(MIX_ROWS // CHUNK):
        chunk(ck)

    ys = []
    for hd in range(N_HEADS):
        lo = hd * HEAD_W
        o = oall_ref[:, lo:lo + HEAD_W]
        if hd < HG_HEADS:
            w, gate = hgw_ref[...], proj_ref[:, COL_HGATE + lo:COL_HGATE + lo + HEAD_W]
        else:
            w, gate = glw_ref[...], proj_ref[:, COL_GGATE + lo - _W:COL_GGATE + lo - _W + HEAD_W]
        ys.append((_rms(o, w) * (gate * jax.nn.sigmoid(gate))).astype(BF16))
    y = jnp.concatenate(ys, axis=-1)
    o_ref[0] = x + _dot(y, wout_ref[...])


def _token_mix(x, norm_w, w_in_p, lb_logits, hg_onorm_w, w_gk2_p, b_gk_p, gla_onorm_w, w_out):
    bsz, seq, d = x.shape
    const = lambda shape: pl.BlockSpec(shape, lambda b, t: (0,) * len(shape))
    return pl.pallas_call(
        _mix_kernel,
        grid=(bsz, seq // MIX_ROWS),
        in_specs=[
            pl.BlockSpec((1, MIX_ROWS, d), lambda b, t: (b, t, 0)),
            const((1, d)),
            const((d, PROJ_W)),
            const(lb_logits.shape),
            const((1, HEAD_W)),
            const((LANES, _W)),
            const((1, _W)),
            const((1, HEAD_W)),
            const((2 * _W, d)),
        ],
        out_specs=pl.BlockSpec((1, MIX_ROWS, d), lambda b, t: (b, t, 0)),
        out_shape=jax.ShapeDtypeStruct((bsz, seq, d), F32),
        scratch_shapes=[
            pltpu.VMEM((MIX_ROWS, PROJ_W), F32),
            pltpu.VMEM((MIX_ROWS, N_HEADS * HEAD_W), F32),
            pltpu.VMEM((N_HEADS, HEAD_W, HEAD_W), F32),
            pltpu.VMEM((CHUNK, N_HEADS * HEAD_W), F32),
            pltpu.VMEM((CHUNK, N_HEADS * HEAD_W), F32),
            pltpu.VMEM((CHUNK, N_HEADS * HEAD_W), F32),
            pltpu.VMEM((N_HEADS, CHUNK, CHUNK), F32),
        ],
        compiler_params=pltpu.CompilerParams(
            dimension_semantics=("arbitrary", "arbitrary"), vmem_limit_bytes=VMEM_LIMIT_BYTES),
        name="token_mix",
    )(x, norm_w, w_in_p, lb_logits, hg_onorm_w, w_gk2_p, b_gk_p, gla_onorm_w, w_out)


def _xattn_tile(x, k_scr, v_scr, nxw_ref, wq_ref, wo_ref, nmoe_ref, wr_hi_ref, wr_lo_ref, br_ref):
    d = x.shape[-1]
    hdim = d // XA_HEADS
    q = _dot(_rms(x, nxw_ref[...]).astype(BF16), wq_ref[...])
    outs = []
    for h in range(XA_HEADS):
        sl = slice(h * hdim, (h + 1) * hdim)
        s = _dot_nt(q[:, sl].astype(BF16), k_scr[:, sl]) * (hdim ** -0.5)
        p = jnp.exp(s - jnp.max(s, axis=-1, keepdims=True))
        p = p / jnp.sum(p, axis=-1, keepdims=True)
        outs.append(_dot(p.astype(BF16), v_scr[:, sl]).astype(BF16))
    x2 = x + _dot(jnp.concatenate(outs, axis=-1), wo_ref[...])

    hm = _rms(x2, nmoe_ref[...])
    hm_hi = hm.astype(BF16)

    hm_lo = (hm - hm_hi.astype(F32)).astype(BF16)
    logits = (_dot(hm_hi, wr_hi_ref[...]) + _dot(hm_lo, wr_hi_ref[...]) + _dot(hm_hi, wr_lo_ref[...])
              + br_ref[...])

    rows = logits.shape[0]
    lane = lax.broadcasted_iota(jnp.int32, (rows, LANES), 1)
    lane_f = lane.astype(F32)
    neg_inf = jnp.float32(-jnp.inf)
    top_v, top_i, hots = [], [], []
    work = logits
    for _ in range(TOP_K):
        m = jnp.max(work, axis=-1, keepdims=True)
        idx = jnp.min(jnp.where(work == m, lane_f, float(LANES)), axis=-1, keepdims=True)
        hot = lane_f == idx
        work = jnp.where(hot, neg_inf, work)
        top_v.append(m)
        top_i.append(idx)
        hots.append(hot)
    es = [jnp.exp(v - top_v[0]) for v in top_v]
    denom = es[0] + es[1] + es[2] + es[3]
    gates = [e / denom for e in es]

    chosen = jnp.where(hots[0] | hots[1] | hots[2] | hots[3], 1.0, 0.0)
    ri = lax.broadcasted_iota(jnp.int32, (rows, rows), 0)
    ci = lax.broadcasted_iota(jnp.int32, (rows, rows), 1)
    strict = jnp.where(ri > ci, 1.0, 0.0).astype(BF16)
    before = _dot(strict, chosen.astype(BF16))
    counts = jnp.sum(chosen, axis=0, keepdims=True)
    padded = jnp.floor((counts + (ROW_ALIGN - 1.0)) * (1.0 / ROW_ALIGN)) * ROW_ALIGN
    li = lax.broadcasted_iota(jnp.int32, (LANES, LANES), 0)
    lj = lax.broadcasted_iota(jnp.int32, (LANES, LANES), 1)
    group_start = _dot(jnp.broadcast_to(padded, (8, LANES)).astype(BF16),
                       jnp.where(li < lj, 1.0, 0.0).astype(BF16))[0:1]
    slot_of = before + group_start
    slots = [jnp.sum(jnp.where(hot, slot_of, 0.0), axis=-1, keepdims=True) for hot in hots]

    meta = jnp.zeros((rows, LANES), F32)
    for j, col in enumerate(top_i + slots + gates):
        meta = jnp.where(lane == j, col, meta)
    return x2, hm_hi, meta, counts


def _xattn_kernel(x_ref, mem_ref, nxw_ref, nmw_ref, wq_ref, wkv_ref, wo_ref, nmoe_ref,
                  wr_hi_ref, wr_lo_ref, br_ref,
                  x2_ref, hm_ref, meta_ref, meta_t_ref, cnt_ref, k_scr, v_scr):
    d = x_ref.shape[-1]

    @pl.when(pl.program_id(1) == 0)
    def _():
        m = _rms(mem_ref[0], nmw_ref[...]).astype(BF16)
        kv = _dot(m, wkv_ref[...])
        k_scr[...] = kv[:, :d].astype(BF16)
        v_scr[...] = kv[:, d:].astype(BF16)

    for u in range(2):
        rows = slice(u * TOK_TILE, (u + 1) * TOK_TILE)
        x2, hm16, meta, counts = _xattn_tile(x_ref[0, rows, :], k_scr, v_scr, nxw_ref, wq_ref, wo_ref, nmoe_ref,
                                             wr_hi_ref, wr_lo_ref, br_ref)
        x2_ref[0, rows, :] = x2
        hm_ref[0, rows, :] = hm16
        meta_ref[rows, :] = meta
        meta_t_ref[u] = meta.T[0:META_ROWS, :]
        cnt_ref[u * 8:(u + 1) * 8, :] = jnp.broadcast_to(counts, (8, LANES))


def _xattn_router(x1, mem, norm_xa_w, norm_mem_w, w_xq, w_xkv, w_xo, norm_moe_w, wr_hi, wr_lo, br_p):
    bsz, seq, d = x1.shape
    mlen = mem.shape[1]
    n_t = seq // (2 * TOK_TILE)
    const = lambda shape: pl.BlockSpec(shape, lambda b, t: (0,) * len(shape))
    return pl.pallas_call(
        _xattn_kernel,
        grid=(bsz, n_t),
        in_specs=[
            pl.BlockSpec((1, 2 * TOK_TILE, d), lambda b, t: (b, t, 0)),
            pl.BlockSpec((1, mlen, d), lambda b, t: (b, 0, 0)),
            const((1, d)), const((1, d)),
            const((d, d)), const((d, 2 * d)), const((d, d)),
            const((1, d)),
            const((d, LANES)), const((d, LANES)), const((1, LANES)),
        ],
        out_specs=[
            pl.BlockSpec((1, 2 * TOK_TILE, d), lambda b, t: (b, t, 0)),
            pl.BlockSpec((1, 2 * TOK_TILE, d), lambda b, t: (b, t, 0)),
            pl.BlockSpec((2 * TOK_TILE, LANES), lambda b, t: (b * n_t + t, 0)),
            pl.BlockSpec((2, META_ROWS, TOK_TILE), lambda b, t: (b * n_t + t, 0, 0)),
            pl.BlockSpec((16, LANES), lambda b, t: (b * n_t + t, 0)),
        ],
        out_shape=[
            jax.ShapeDtypeStruct((bsz, seq, d), F32),
            jax.ShapeDtypeStruct((bsz, seq, d), BF16),
            jax.ShapeDtypeStruct((bsz * seq, LANES), F32),
            jax.ShapeDtypeStruct((bsz * seq // TOK_TILE, META_ROWS, TOK_TILE), F32),
            jax.ShapeDtypeStruct((bsz * seq // TOK_TILE * 8, LANES), F32),
        ],
        scratch_shapes=[
            pltpu.VMEM((mlen, d), BF16),
            pltpu.VMEM((mlen, d), BF16),
        ],
        compiler_params=pltpu.CompilerParams(
            dimension_semantics=("arbitrary", "arbitrary"), vmem_limit_bytes=VMEM_LIMIT_BYTES),
        name="xattn_router",
    )(x1, mem, norm_xa_w, norm_mem_w, w_xq, w_xkv, w_xo, norm_moe_w, wr_hi, wr_lo, br_p)


def _piece_copy(src_ref, src_row, dst_ref, dst_row, sem):
    return pltpu.make_async_copy(src_ref.at[pl.ds(pl.multiple_of(src_row, ROW_ALIGN), ROW_ALIGN)],
                                 dst_ref.at[pl.ds(pl.multiple_of(dst_row, ROW_ALIGN), ROW_ALIGN)], sem)


def _block_copy(src_ref, dst_ref, dst_block, sem):
    return pltpu.make_async_copy(src_ref, dst_ref.at[pl.ds(pl.multiple_of(dst_block * MOE_ROWS, MOE_ROWS), MOE_ROWS)], sem)


def _dispatch_kernel(pdst_ref, np_ref, tail_ref, cap_ref, mt_ref, hm_ref, xs_ref, stage_ref, zero_ref, sem, sem_z):
    s = pl.program_id(0)
    last = pl.num_programs(0) - 1
    half = hm_ref.shape[1] // 2

    def spare_blocks(fn):
        def body(i, c):
            fn(_block_copy(zero_ref, xs_ref, cap_ref[0] + i, sem_z))
            return c
        lax.fori_loop(0, cap_ref[1], body, 0)

    @pl.when(s == 0)
    def _():
        zero_ref[...] = jnp.zeros_like(zero_ref)
        spare_blocks(lambda cp: cp.start())

        def per_expert(e, carry):
            n = tail_ref[1, e]
            off = tail_ref[0, e]

            def start(i, c):
                _piece_copy(zero_ref, 0, xs_ref, off + i * ROW_ALIGN, sem.at[0]).start()
                return c

            def wait(i, c):
                _piece_copy(zero_ref, 0, xs_ref, 0, sem.at[0]).wait()
                return c

            lax.fori_loop(0, n, start, 0)
            lax.fori_loop(0, n, wait, 0)
            return carry

        lax.fori_loop(0, N_EXPERTS, per_expert, 0)

    def drain(u, tile):
        def wait(q, c):
            _piece_copy(stage_ref.at[u], 0, xs_ref, 0, sem.at[u]).wait()
            return c
        lax.fori_loop(0, np_ref[tile], wait, 0)

    for u in range(2):
        tile = 2 * s + u

        @pl.when(s > 0)
        def _():
            drain(u, tile - 2)

        x16 = hm_ref[u * TOK_TILE:(u + 1) * TOK_TILE, :]
        tr = mt_ref[u, TOP_K:2 * TOP_K, :].astype(jnp.int32)
        g = mt_ref[u, 2 * TOP_K:3 * TOP_K, :]
        for c in range(STAGE_ROWS // TOK_TILE):
            rows = slice(c * TOK_TILE, (c + 1) * TOK_TILE)
            rid = lax.broadcasted_iota(jnp.int32, (TOK_TILE, TOK_TILE), 0) + c * TOK_TILE
            hot = [rid == tr[k:k + 1, :] for k in range(TOP_K)]
            sel = jnp.where(hot[0] | hot[1] | hot[2] | hot[3], 1.0, 0.0).astype(BF16)
            stage_ref[u, rows, 0:half] = _pack_halves(_dot(sel, x16), rounded=True)
            gsum = jnp.zeros((TOK_TILE, 1), F32)
            for k in range(TOP_K):
                gsum = gsum + jnp.sum(jnp.where(hot[k], g[k:k + 1, :], 0.0), axis=-1, keepdims=True)
            stage_ref[u, rows, half:half + LANES] = lax.bitcast_convert_type(
                jnp.broadcast_to(gsum, (TOK_TILE, LANES)), jnp.uint32)

        def start(q, c):
            _piece_copy(stage_ref.at[u], q * ROW_ALIGN, xs_ref, pdst_ref[tile, q], sem.at[u]).start()
            return c

        lax.fori_loop(0, np_ref[tile], start, 0)

    @pl.when(s == last)
    def _():
        drain(0, 2 * s)
        drain(1, 2 * s + 1)
        spare_blocks(lambda cp: cp.wait())


def _dispatch(hm, pdst, n_pieces, tail_info, spare, meta_t, n_rows):
    n_tok, d = hm.shape
    n_tiles = n_tok // TOK_TILE
    assert n_tiles % 2 == 0
    smem = pl.BlockSpec(memory_space=pltpu.SMEM)
    return pl.pallas_call(
        _dispatch_kernel,
        grid=(n_tiles // 2,),
        in_specs=[
            smem, smem, smem, smem,
            pl.BlockSpec((2, META_ROWS, TOK_TILE), lambda s: (s, 0, 0)),
            pl.BlockSpec((2 * TOK_TILE, d), lambda s: (s, 0)),
        ],
        out_specs=pl.BlockSpec(memory_space=pl.ANY),
        out_shape=jax.ShapeDtypeStruct((n_rows, d // 2 + LANES), jnp.uint32),
        scratch_shapes=[pltpu.VMEM((2, STAGE_ROWS, d // 2 + LANES), jnp.uint32),
                        pltpu.VMEM((MOE_ROWS, d // 2 + LANES), jnp.uint32),
                        pltpu.SemaphoreType.DMA((2,)), pltpu.SemaphoreType.DMA],
        compiler_params=pltpu.CompilerParams(
            dimension_semantics=("arbitrary",), vmem_limit_bytes=VMEM_LIMIT_BYTES),
        name="moe_dispatch",
    )(pdst, n_pieces, tail_info, spare, meta_t, hm)


def _expert_kernel(be_ref, nu_ref, first_ref, slot_ref, nxt_ref, xs_ref, wgu_hbm, bgu_ref, wd_hbm,
                   bd_ref, ys_ref, wgu32_ref, wd32_ref, wgu16_ref, wd16_ref, sem):
    i = pl.program_id(0)
    f, d = wd16_ref.shape

    def weight_copies(e, slot):
        return (pltpu.make_async_copy(wgu_hbm.at[e], wgu32_ref.at[slot], sem.at[0, slot]),
                pltpu.make_async_copy(wd_hbm.at[e], wd32_ref.at[slot], sem.at[1, slot]))

    @pl.when(i == 0)
    def _():
        for cp in weight_copies(be_ref[0], 0):
            cp.start()

    @pl.when(first_ref[i] == 1)
    def _():
        slot = slot_ref[i]
        for cp in weight_copies(be_ref[i], slot):
            cp.wait()

        @pl.when(nxt_ref[i] >= 0)
        def _():
            for cp in weight_copies(nxt_ref[i], 1 - slot):
                cp.start()

        wgu16_ref[...] = wgu32_ref[slot].astype(BF16)
        wd16_ref[...] = wd32_ref[slot].astype(BF16)

    @pl.when(i < nu_ref[0])
    def _():
        x_lo, x_hi = _unpack_halves(xs_ref[:, 0:d // 2])
        gu = _dot(x_lo, wgu16_ref[0:d // 2, :]) + _dot(x_hi, wgu16_ref[d // 2:d, :]) + bgu_ref[0]
        gate = jnp.minimum(gu[:, :f], SWIGLU_LIMIT)
        up = jnp.clip(gu[:, f:], -SWIGLU_LIMIT, SWIGLU_LIMIT)
        act = (up + 1.0) * gate * jax.nn.sigmoid(SWIGLU_ALPHA * gate)
        row_gate = lax.bitcast_convert_type(xs_ref[:, d // 2:d // 2 + 1], F32)
        ys_ref[...] = _pack_halves((_dot(act.astype(BF16), wd16_ref[...]) + bd_ref[0]) * row_gate, rounded=False)

    @pl.when(i >= nu_ref[0])
    def _():
        ys_ref[...] = jnp.zeros_like(ys_ref)


def _expert_mlp(xs, block_e, n_used, first, slot, nxt, w_gu, b_gu, w_d, b_d):
    n_rows = xs.shape[0]
    d = w_d.shape[2]
    n_blocks = n_rows // MOE_ROWS
    f = w_d.shape[1]
    row_map = lambda i, be, nu, *_: (jnp.minimum(i, nu[0] - 1), 0)
    exp_map = lambda i, be, *_: (be[i], 0, 0)
    return pl.pallas_call(
        _expert_kernel,
        grid_spec=pltpu.PrefetchScalarGridSpec(
            num_scalar_prefetch=5,
            grid=(n_blocks,),
            in_specs=[
                pl.BlockSpec((MOE_ROWS, d // 2 + LANES), row_map),
                pl.BlockSpec(memory_space=pl.ANY),
                pl.BlockSpec((1, 1, 2 * f), exp_map),
                pl.BlockSpec(memory_space=pl.ANY),
                pl.BlockSpec((1, 1, d), exp_map),
            ],
            out_specs=pl.BlockSpec((MOE_ROWS, d // 2), lambda i, *_: (i, 0)),
            scratch_shapes=[
                pltpu.VMEM((2, d, 2 * f), F32), pltpu.VMEM((2, f, d), F32),
                pltpu.VMEM((d, 2 * f), BF16), pltpu.VMEM((f, d), BF16),
                pltpu.SemaphoreType.DMA((2, 2)),
            ],
        ),
        out_shape=jax.ShapeDtypeStruct((n_rows, d // 2), jnp.uint32),
        compiler_params=pltpu.CompilerParams(
            dimension_semantics=("arbitrary",), vmem_limit_bytes=VMEM_LIMIT_BYTES),
        name="moe_experts",
    )(block_e, n_used, first, slot, nxt, xs, w_gu, b_gu, w_d, b_d)


def _combine_kernel(pdst_ref, np_ref, x2_ref, meta_ref, nfw_ref, ys_ref, out_ref, ybuf_ref, sem):
    s = pl.program_id(0)
    last = pl.num_programs(0) - 1

    def fetch(u, tile):
        def start(q, c):
            _piece_copy(ys_ref, pdst_ref[tile, q], ybuf_ref.at[u], q * ROW_ALIGN, sem.at[u]).start()
            return c
        lax.fori_loop(0, np_ref[tile], start, 0)

    def finish(u, tile):
        def wait(q, c):
            _piece_copy(ys_ref, 0, ybuf_ref.at[u], 0, sem.at[u]).wait()
            return c
        lax.fori_loop(0, np_ref[tile], wait, 0)

        rows = slice(u * TOK_TILE, (u + 1) * TOK_TILE)
        tr = meta_ref[rows, TOP_K:2 * TOP_K].astype(jnp.int32)
        half = x2_ref.shape[1] // 2
        acc_lo = x2_ref[rows, 0:half]
        acc_hi = x2_ref[rows, half:2 * half]
        for c in range(STAGE_ROWS // TOK_TILE):
            cid = lax.broadcasted_iota(jnp.int32, (TOK_TILE, TOK_TILE), 1) + c * TOK_TILE
            hot = [cid == tr[:, k:k + 1] for k in range(TOP_K)]
            sel = jnp.where(hot[0] | hot[1] | hot[2] | hot[3], 1.0, 0.0).astype(BF16)
            y_lo, y_hi = _unpack_halves(ybuf_ref[u, c * TOK_TILE:(c + 1) * TOK_TILE, :])
            acc_lo = acc_lo + _dot(sel, y_lo)
            acc_hi = acc_hi + _dot(sel, y_hi)
        out_ref[rows, :] = _rms(jnp.concatenate([acc_lo, acc_hi], axis=-1), nfw_ref[...])

    @pl.when(s == 0)
    def _():
        ybuf_ref[...] = jnp.zeros_like(ybuf_ref)
        fetch(0, 0)

    fetch(1, 2 * s + 1)
    finish(0, 2 * s)

    @pl.when(s < last)
    def _():
        fetch(0, 2 * s + 2)

    finish(1, 2 * s + 1)


def _combine(x2, meta, pdst, n_pieces, ys, norm_final_w):
    n_tok, d = x2.shape
    n_tiles = n_tok // TOK_TILE
    assert n_tiles % 2 == 0
    smem = pl.BlockSpec(memory_space=pltpu.SMEM)
    return pl.pallas_call(
        _combine_kernel,
        grid=(n_tiles // 2,),
        in_specs=[
            smem, smem,
            pl.BlockSpec((2 * TOK_TILE, d), lambda s: (s, 0)),
            pl.BlockSpec((2 * TOK_TILE, LANES), lambda s: (s, 0)),
            pl.BlockSpec((1, d), lambda s: (0, 0)),
            pl.BlockSpec(memory_space=pl.ANY),
        ],
        out_specs=pl.BlockSpec((2 * TOK_TILE, d), lambda s: (s, 0)),
        out_shape=jax.ShapeDtypeStruct((n_tok, d), F32),
        scratch_shapes=[pltpu.VMEM((2, STAGE_ROWS, d // 2), jnp.uint32), pltpu.SemaphoreType.DMA((2,))],
        compiler_params=pltpu.CompilerParams(
            dimension_semantics=("arbitrary",), vmem_limit_bytes=VMEM_LIMIT_BYTES),
        name="moe_combine",
    )(pdst, n_pieces, x2, meta, norm_final_w, ys)


def _pad_heads(w, n_heads, width):
    lead = w.shape[:-1]
    w = w.reshape(lead + (n_heads, width))
    w = jnp.pad(w, [(0, 0)] * len(lead) + [(0, 0), (0, HEAD_W - width)])
    return w.reshape(lead + (n_heads * HEAD_W,))


def _round_up(x, m):
    return (x + m - 1) // m * m


def _routing_tables(cnt, n_tiles):
    i32 = jnp.int32
    counts = cnt.reshape(n_tiles, 8, LANES)[:, 0, :N_EXPERTS].astype(i32)
    cnt8 = _round_up(counts, ROW_ALIGN)
    lend = jnp.cumsum(cnt8, axis=1)
    lstart = lend - cnt8
    tot = jnp.sum(cnt8, axis=0)
    padded = _round_up(tot, MOE_ROWS)
    pends = jnp.cumsum(padded)
    pstarts = pends - padded
    goff = pstarts[None, :] + jnp.cumsum(cnt8, axis=0) - cnt8

    prow = jnp.arange(MAX_PIECES, dtype=i32) * ROW_ALIGN
    pe = jnp.sum(prow[None, :, None] >= lend[:, None, :], axis=-1)
    pe = jnp.minimum(pe, N_EXPERTS - 1)
    pick = pe[:, :, None] == jnp.arange(N_EXPERTS, dtype=i32)
    pdst = (jnp.sum(jnp.where(pick, (goff - lstart)[:, None, :], 0), axis=-1) + prow[None, :]).astype(i32)
    n_pieces = (lend[:, -1] // ROW_ALIGN).astype(i32)
    pdst = jnp.where(jnp.arange(MAX_PIECES, dtype=i32)[None, :] < n_pieces[:, None], pdst, 0)

    max_rows = n_tiles * (TOK_TILE * TOP_K + N_EXPERTS * (ROW_ALIGN - 1)) + N_EXPERTS * (MOE_ROWS - ROW_ALIGN)
    n_blocks = -(-max_rows // MOE_ROWS)
    n_used = (pends[-1] // MOE_ROWS).astype(i32)
    blk = jnp.arange(n_blocks, dtype=i32)
    block_e = jnp.minimum(jnp.sum(blk[:, None] * MOE_ROWS >= pends[None, :], axis=-1), N_EXPERTS - 1).astype(i32)
    block_e = jnp.where(blk < n_used, block_e, block_e[jnp.maximum(n_used - 1, 0)])
    first = ((blk < n_used) & ((blk == 0) | (block_e != jnp.roll(block_e, 1)))).astype(i32)
    has_rows = padded > 0
    slot_e = (jnp.cumsum(has_rows.astype(i32)) - 1) % 2
    later = jnp.arange(N_EXPERTS, dtype=i32)[None, :] > jnp.arange(N_EXPERTS, dtype=i32)[:, None]
    nxt_e = jnp.min(jnp.where(later & has_rows[None, :], jnp.arange(N_EXPERTS, dtype=i32)[None, :], N_EXPERTS), axis=1)
    nxt_e = jnp.where(nxt_e < N_EXPERTS, nxt_e, -1).astype(i32)
    tail_info = jnp.stack([pstarts + tot, (padded - tot) // ROW_ALIGN]).astype(i32)
    spare = jnp.stack([n_used, n_blocks - n_used]).astype(i32)
    of_block = block_e[:, None] == jnp.arange(N_EXPERTS, dtype=i32)
    slot_b = jnp.sum(jnp.where(of_block, slot_e[None, :], 0), axis=1).astype(i32)
    nxt_b = jnp.sum(jnp.where(of_block, nxt_e[None, :], 0), axis=1).astype(i32)
    return (pdst, n_pieces, tail_info, spare,
            (block_e, n_used.reshape(1), first, slot_b, nxt_b), n_blocks * MOE_ROWS)


def kernel(x, mem, norm_mix_w, w_in, hg_lb_logits, hg_onorm_w, gla_w_gk2, gla_b_gk, gla_onorm_w, w_out, norm_xa_w, norm_mem_w, w_xq, w_xkv, w_xo, norm_moe_w, w_router, b_router, w_gate_up, b_gate_up, w_down, b_down, norm_final_w):
    assert w_in.shape[0] == 1, "single-layer block"
    bsz, seq, d = x.shape
    n_tok = bsz * seq
    gk = GLA_HEADS * GLA_KDIM

    wi = w_in[0]
    w_in_p = jnp.concatenate([
        wi[:, :4 * _W],
        _pad_heads(wi[:, 4 * _W:4 * _W + gk], GLA_HEADS, GLA_KDIM),
        _pad_heads(wi[:, 4 * _W + gk:4 * _W + 2 * gk], GLA_HEADS, GLA_KDIM),
        wi[:, 4 * _W + 2 * gk:4 * _W + 2 * gk + 2 * _W],
        jnp.pad(wi[:, 4 * _W + 2 * gk + 2 * _W:], ((0, 0), (0, LANES - GLA_RANK))),
    ], axis=1).astype(BF16)
    w_gk2_p = jnp.pad(_pad_heads(gla_w_gk2[0], GLA_HEADS, GLA_KDIM),
                      ((0, LANES - GLA_RANK), (0, 0))).astype(BF16)
    b_gk_p = _pad_heads(gla_b_gk, GLA_HEADS, GLA_KDIM)
    wr = jnp.pad(w_router[0], ((0, 0), (0, LANES - N_EXPERTS)))
    wr_hi = wr.astype(BF16)
    wr_lo = (wr - wr_hi.astype(F32)).astype(BF16)
    br_p = jnp.pad(b_router, ((0, 0), (0, LANES - N_EXPERTS)), constant_values=-1e30)

    x1 = _token_mix(x, norm_mix_w, w_in_p, hg_lb_logits, hg_onorm_w, w_gk2_p, b_gk_p, gla_onorm_w,
                    w_out[0].astype(BF16))
    x2, hm, meta, meta_t, cnt = _xattn_router(x1, mem, norm_xa_w, norm_mem_w, w_xq[0].astype(BF16),
                                              w_xkv[0].astype(BF16), w_xo[0].astype(BF16), norm_moe_w,
                                              wr_hi, wr_lo, br_p)

    n_tiles = n_tok // TOK_TILE
    pdst, n_pieces, tail_info, spare, block_tables, n_rows = _routing_tables(cnt, n_tiles)

    xs = _dispatch(hm.reshape(n_tok, d), pdst, n_pieces, tail_info, spare, meta_t, n_rows)
    ys = _expert_mlp(xs, *block_tables, w_gate_up[0], b_gate_up[0][:, None, :],
                     w_down[0], b_down[0][:, None, :])
    out = _combine(x2.reshape(n_tok, d), meta, pdst, n_pieces, ys, norm_final_w[None, :])
    return out.reshape(bsz, seq, d)
```

```python
import functools

import jax
import jax.numpy as jnp
from jax import lax
from jax.experimental import pallas as pl
from jax.experimental.pallas import tpu as pltpu

F32 = jnp.float32
BF16 = jnp.bfloat16
EPS = 1e-6
LOG2E = 1.4426950408889634

HEAD_W = 128
HG_HEADS = 4
GLA_HEADS = 4
GLA_KDIM = 64
N_HEADS = HG_HEADS + GLA_HEADS
GLA_RANK = 16
GLA_GATE_NORMALIZER = 16.0
CHUNK = 64
XA_HEADS = 4
N_EXPERTS = 32
TOP_K = 4
SWIGLU_LIMIT = 7.0
SWIGLU_ALPHA = 1.702

LANES = 128
VMEM_LIMIT_BYTES = 56 * 1024 * 1024

MIX_ROWS = 256
TOK_TILE = 256
MOE_ROWS = 512
ROW_ALIGN = 8
STAGE_ROWS = 1280
MAX_PIECES = STAGE_ROWS // ROW_ALIGN
META_ROWS = 16

_W = HG_HEADS * HEAD_W
COL_HQ, COL_HF, COL_HI, COL_HGATE = 0, _W, 2 * _W, 3 * _W
COL_GQ, COL_GK, COL_GV, COL_GGATE = 4 * _W, 5 * _W, 6 * _W, 7 * _W
COL_GLR = 8 * _W
PROJ_W = COL_GLR + LANES


def _rms(x, w):
    return x * lax.rsqrt(jnp.mean(x * x, axis=-1, keepdims=True) + EPS) * w


def _dot(a, b):
    return jnp.dot(a, b, preferred_element_type=F32)


def _dot_nt(a, b):
    return lax.dot_general(a, b, (((1,), (1,)), ((), ())), preferred_element_type=F32)


def _dot_tn(a, b):
    return lax.dot_general(a, b, (((0,), (0,)), ((), ())), preferred_element_type=F32)


def _head_chunk(hd, q_scr, k_scr, b2_scr, v, st_ref, sc_ref, masks):
    sl = slice(hd * HEAD_W, (hd + 1) * HEAD_W)
    q, k, b2 = q_scr[:, sl], k_scr[:, sl], b2_scr[:, sl]

    for s in range(0, CHUNK, 8):
        q_blk, b_blk = q[s:s + 8], b2[s:s + 8]
        for j in range(s, s + 8):
            e = jnp.exp2(b_blk - b2_scr[j:j + 1, sl])
            sc_ref[s:s + 8, j:j + 1] = jnp.sum(q_blk * (k_scr[j:j + 1, sl] * e), axis=-1, keepdims=True)
    scores = jnp.where(masks[0], sc_ref[...], 0.0)

    for lvl, size in enumerate((16, 32, 64)):
        mids = [b2_scr[r:r + 1, sl] for r in range(size // 2 - 1, CHUNK, size)]
        mid = jnp.concatenate([jnp.broadcast_to(m, (size, HEAD_W)) for m in mids], axis=0)
        dist = b2 - mid
        e = jnp.exp2(jnp.minimum(dist, -dist))
        r = _dot_nt((q * e).astype(BF16), (k * e).astype(BF16))
        scores = jnp.where(masks[lvl + 1], r, scores)

    b2_last = b2_scr[CHUNK - 1:CHUNK, sl]
    q_abs = (q * jnp.exp2(b2)).astype(BF16)
    k_end = (k * jnp.exp2(b2_last - b2)).astype(BF16)
    v16 = v.astype(BF16)
    st = st_ref[...]
    o = _dot(scores.astype(BF16), v16) + _dot_nt(q_abs, st.astype(BF16))
    st_ref[...] = st * jnp.exp2(b2_last) + _dot_tn(v16, k_end)
    return o


def _mix_kernel(x_ref, nw_ref, win_ref, lbl_ref, hgw_ref, wgk_ref, bgk_ref, glw_ref, wout_ref,
                o_ref, proj_ref, oall_ref, st_ref, q_scr, k_scr, b2_scr, sc_scr):
    @pl.when(pl.program_id(1) == 0)
    def _():
        st_ref[...] = jnp.zeros_like(st_ref)
        sc_scr[...] = jnp.zeros_like(sc_scr)

    x = x_ref[0]
    h = _rms(x, nw_ref[...]).astype(BF16)
    proj_ref[...] = _dot(h, win_ref[...])

    lbl = lbl_ref[...]
    e = jnp.exp(lbl - jnp.max(lbl, axis=0, keepdims=True))
    lb = e[0:1] / jnp.sum(e, axis=0, keepdims=True)

    ri = lax.broadcasted_iota(jnp.int32, (CHUNK, CHUNK), 0)
    ci = lax.broadcasted_iota(jnp.int32, (CHUNK, CHUNK), 1)
    tri = jnp.where(ri >= ci, 1.0, 0.0).astype(BF16)
    masks = [(ri // 8 == ci // 8) & (ri >= ci)]
    for size in (16, 32, 64):
        masks.append((ri // size == ci // size) & (ri // (size // 2) > ci // (size // 2)))

    def chunk(ck):
        rows = pl.ds(ck * CHUNK, CHUNK)
        f = lb + (1.0 - lb) * jax.nn.sigmoid(proj_ref[rows, COL_HF:COL_HF + _W])
        g_hg = jnp.log(f)
        z = _dot(proj_ref[rows, COL_GLR:COL_GLR + LANES].astype(BF16), wgk_ref[...]) + bgk_ref[...]
        g_gla = (jnp.minimum(z, 0.0) - jnp.log1p(jnp.exp(-jnp.abs(z)))) * (1.0 / GLA_GATE_NORMALIZER)
        g_all = jnp.concatenate([g_hg, g_gla], axis=-1)
        g_hi = g_all.astype(BF16)
        g_lo = (g_all - g_hi.astype(F32)).astype(BF16)
        b2_scr[...] = (_dot(tri, g_hi) + _dot(tri, g_lo)) * LOG2E
        hq = proj_ref[rows, COL_HQ:COL_HQ + _W]
        q_scr[:, 0:_W] = hq * jax.nn.sigmoid(hq)
        q_scr[:, _W:2 * _W] = proj_ref[rows, COL_GQ:COL_GQ + _W] * (GLA_KDIM ** -0.5)
        k_scr[:, 0:_W] = 1.0 - f
        k_scr[:, _W:2 * _W] = proj_ref[rows, COL_GK:COL_GK + _W]

        for hd in range(N_HEADS):
            lo = hd * HEAD_W
            col_v = COL_HI + lo if hd < HG_HEADS else COL_GV + lo - _W
            o = _head_chunk(hd, q_scr, k_scr, b2_scr, proj_ref[rows, col_v:col_v + HEAD_W],
                            st_ref.at[hd], sc_scr.at[hd], masks)
            oall_ref[rows, lo:lo + HEAD_W] = o

    for ck in range(MIX_ROWS // CHUNK):
        chunk(ck)

    ys = []
    for hd in range(N_HEADS):
        lo = hd * HEAD_W
        o = oall_ref[:, lo:lo + HEAD_W]
        if hd < HG_HEADS:
            w, gate = hgw_ref[...], proj_ref[:, COL_HGATE + lo:COL_HGATE + lo + HEAD_W]
        else:
            w, gate = glw_ref[...], proj_ref[:, COL_GGATE + lo - _W:COL_GGATE + lo - _W + HEAD_W]
        ys.append((_rms(o, w) * (gate * jax.nn.sigmoid(gate))).astype(BF16))
    y = jnp.concatenate(ys, axis=-1)
    o_ref[0] = x + _dot(y, wout_ref[...])


def _token_mix(x, norm_w, w_in_p, lb_logits, hg_onorm_w, w_gk2_p, b_gk_p, gla_onorm_w, w_out):
    bsz, seq, d = x.shape
    const = lambda shape: pl.BlockSpec(shape, lambda b, t: (0,) * len(shape))
    return pl.pallas_call(
        _mix_kernel,
        grid=(bsz, seq // MIX_ROWS),
        in_specs=[
            pl.BlockSpec((1, MIX_ROWS, d), lambda b, t: (b, t, 0)),
            const((1, d)),
            const((d, PROJ_W)),
            const(lb_logits.shape),
            const((1, HEAD_W)),
            const((LANES, _W)),
            const((1, _W)),
            const((1, HEAD_W)),
            const((2 * _W, d)),
        ],
        out_specs=pl.BlockSpec((1, MIX_ROWS, d), lambda b, t: (b, t, 0)),
        out_shape=jax.ShapeDtypeStruct((bsz, seq, d), F32),
        scratch_shapes=[
            pltpu.VMEM((MIX_ROWS, PROJ_W), F32),
            pltpu.VMEM((MIX_ROWS, N_HEADS * HEAD_W), F32),
            pltpu.VMEM((N_HEADS, HEAD_W, HEAD_W), F32),
            pltpu.VMEM((CHUNK, N_HEADS * HEAD_W), F32),
            pltpu.VMEM((CHUNK, N_HEADS * HEAD_W), F32),
            pltpu.VMEM((CHUNK, N_HEADS * HEAD_W), F32),
            pltpu.VMEM((N_HEADS, CHUNK, CHUNK), F32),
        ],
        compiler_params=pltpu.CompilerParams(
            dimension_semantics=("arbitrary", "arbitrary"), vmem_limit_bytes=VMEM_LIMIT_BYTES),
        name="token_mix",
    )(x, norm_w, w_in_p, lb_logits, hg_onorm_w, w_gk2_p, b_gk_p, gla_onorm_w, w_out)


def _xattn_tile(x, k_scr, v_scr, nxw_ref, wq_ref, wo_ref, nmoe_ref, wr_hi_ref, wr_lo_ref, br_ref):
    d = x.shape[-1]
    hdim = d // XA_HEADS
    q = _dot(_rms(x, nxw_ref[...]).astype(BF16), wq_ref[...])
    outs = []
    for h in range(XA_HEADS):
        sl = slice(h * hdim, (h + 1) * hdim)
        s = _dot_nt(q[:, sl].astype(BF16), k_scr[:, sl]) * (hdim ** -0.5)
        p = jnp.exp(s - jnp.max(s, axis=-1, keepdims=True))
        p = p / jnp.sum(p, axis=-1, keepdims=True)
        outs.append(_dot(p.astype(BF16), v_scr[:, sl]).astype(BF16))
    x2 = x + _dot(jnp.concatenate(outs, axis=-1), wo_ref[...])

    hm = _rms(x2, nmoe_ref[...])
    hm_hi = hm.astype(BF16)

    hm_lo = (hm - hm_hi.astype(F32)).astype(BF16)
    logits = (_dot(hm_hi, wr_hi_ref[...]) + _dot(hm_lo, wr_hi_ref[...]) + _dot(hm_hi, wr_lo_ref[...])
              + br_ref[...])

    rows = logits.shape[0]
    lane = lax.broadcasted_iota(jnp.int32, (rows, LANES), 1)
    lane_f = lane.astype(F32)
    neg_inf = jnp.float32(-jnp.inf)
    top_v, top_i, hots = [], [], []
    work = logits
    for _ in range(TOP_K):
        m = jnp.max(work, axis=-1, keepdims=True)
        idx = jnp.min(jnp.where(work == m, lane_f, float(LANES)), axis=-1, keepdims=True)
        hot = lane_f == idx
        work = jnp.where(hot, neg_inf, work)
        top_v.append(m)
        top_i.append(idx)
        hots.append(hot)
    es = [jnp.exp(v - top_v[0]) for v in top_v]
    denom = es[0] + es[1] + es[2] + es[3]
    gates = [e / denom for e in es]

    chosen = jnp.where(hots[0] | hots[1] | hots[2] | hots[3], 1.0, 0.0)
    ri = lax.broadcasted_iota(jnp.int32, (rows, rows), 0)
    ci = lax.broadcasted_iota(jnp.int32, (rows, rows), 1)
    strict = jnp.where(ri > ci, 1.0, 0.0).astype(BF16)
    before = _dot(strict, chosen.astype(BF16))
    counts = jnp.sum(chosen, axis=0, keepdims=True)
    padded = jnp.floor((counts + (ROW_ALIGN - 1.0)) * (1.0 / ROW_ALIGN)) * ROW_ALIGN
    li = lax.broadcasted_iota(jnp.int32, (LANES, LANES), 0)
    lj = lax.broadcasted_iota(jnp.int32, (LANES, LANES), 1)
    group_start = _dot(jnp.broadcast_to(padded, (8, LANES)).astype(BF16),
                       jnp.where(li < lj, 1.0, 0.0).astype(BF16))[0:1]
    slot_of = before + group_start
    slots = [jnp.sum(jnp.where(hot, slot_of, 0.0), axis=-1, keepdims=True) for hot in hots]

    meta = jnp.zeros((rows, LANES), F32)
    for j, col in enumerate(top_i + slots + gates):
        meta = jnp.where(lane == j, col, meta)
    return x2, hm_hi, meta, counts


def _xattn_kernel(x_ref, mem_ref, nxw_ref, nmw_ref, wq_ref, wkv_ref, wo_ref, nmoe_ref,
                  wr_hi_ref, wr_lo_ref, br_ref,
                  x2_ref, hm_ref, meta_ref, meta_t_ref, cnt_ref, k_scr, v_scr):
    d = x_ref.shape[-1]

    @pl.when(pl.program_id(1) == 0)
    def _():
        m = _rms(mem_ref[0], nmw_ref[...]).astype(BF16)
        kv = _dot(m, wkv_ref[...])
        k_scr[...] = kv[:, :d].astype(BF16)
        v_scr[...] = kv[:, d:].astype(BF16)

    for u in range(2):
        rows = slice(u * TOK_TILE, (u + 1) * TOK_TILE)
        x2, hm16, meta, counts = _xattn_tile(x_ref[0, rows, :], k_scr, v_scr, nxw_ref, wq_ref, wo_ref, nmoe_ref,
                                             wr_hi_ref, wr_lo_ref, br_ref)
        x2_ref[0, rows, :] = x2
        hm_ref[0, rows, :] = hm16
        meta_ref[rows, :] = meta
        meta_t_ref[u] = meta.T[0:META_ROWS, :]
        cnt_ref[u * 8:(u + 1) * 8, :] = jnp.broadcast_to(counts, (8, LANES))


def _xattn_router(x1, mem, norm_xa_w, norm_mem_w, w_xq, w_xkv, w_xo, norm_moe_w, wr_hi, wr_lo, br_p):
    bsz, seq, d = x1.shape
    mlen = mem.shape[1]
    n_t = seq // (2 * TOK_TILE)
    const = lambda shape: pl.BlockSpec(shape, lambda b, t: (0,) * len(shape))
    return pl.pallas_call(
        _xattn_kernel,
        grid=(bsz, n_t),
        in_specs=[
            pl.BlockSpec((1, 2 * TOK_TILE, d), lambda b, t: (b, t, 0)),
            pl.BlockSpec((1, mlen, d), lambda b, t: (b, 0, 0)),
            const((1, d)), const((1, d)),
            const((d, d)), const((d, 2 * d)), const((d, d)),
            const((1, d)),
            const((d, LANES)), const((d, LANES)), const((1, LANES)),
        ],
        out_specs=[
            pl.BlockSpec((1, 2 * TOK_TILE, d), lambda b, t: (b, t, 0)),
            pl.BlockSpec((1, 2 * TOK_TILE, d), lambda b, t: (b, t, 0)),
            pl.BlockSpec((2 * TOK_TILE, LANES), lambda b, t: (b * n_t + t, 0)),
            pl.BlockSpec((2, META_ROWS, TOK_TILE), lambda b, t: (b * n_t + t, 0, 0)),
            pl.BlockSpec((16, LANES), lambda b, t: (b * n_t + t, 0)),
        ],
        out_shape=[
            jax.ShapeDtypeStruct((bsz, seq, d), F32),
            jax.ShapeDtypeStruct((bsz, seq, d), BF16),
            jax.ShapeDtypeStruct((bsz * seq, LANES), F32),
            jax.ShapeDtypeStruct((bsz * seq // TOK_TILE, META_ROWS, TOK_TILE), F32),
            jax.ShapeDtypeStruct((bsz * seq // TOK_TILE * 8, LANES), F32),
        ],
        scratch_shapes=[
            pltpu.VMEM((mlen, d), BF16),
            pltpu.VMEM((mlen, d), BF16),
        ],
        compiler_params=pltpu.CompilerParams(
            dimension_semantics=("arbitrary", "arbitrary"), vmem_limit_bytes=VMEM_LIMIT_BYTES),
        name="xattn_router",
    )(x1, mem, norm_xa_w, norm_mem_w, w_xq, w_xkv, w_xo, norm_moe_w, wr_hi, wr_lo, br_p)


def _piece_copy(src_ref, src_row, dst_ref, dst_row, sem):
    return pltpu.make_async_copy(src_ref.at[pl.ds(pl.multiple_of(src_row, ROW_ALIGN), ROW_ALIGN)],
                                 dst_ref.at[pl.ds(pl.multiple_of(dst_row, ROW_ALIGN), ROW_ALIGN)], sem)


def _wait_pieces(src_ref, dst_ref, n, sem):
    size = MAX_PIECES
    while size & (size - 1):
        size &= size - 1
    while size >= 1:
        @pl.when((n & size) != 0)
        def _(size=size):
            pltpu.make_async_copy(src_ref.at[pl.ds(0, size * ROW_ALIGN)], dst_ref.at[pl.ds(0, size * ROW_ALIGN)], sem).wait()
        size //= 2


def _block_copy(src_ref, dst_ref, dst_block, sem):
    return pltpu.make_async_copy(src_ref, dst_ref.at[pl.ds(pl.multiple_of(dst_block * MOE_ROWS, MOE_ROWS), MOE_ROWS)], sem)


def _dispatch_kernel(pdst_ref, np_ref, tail_ref, cap_ref, mt_ref, hm_ref, xs_ref, stage_ref, zero_ref, sem, sem_z):
    s = pl.program_id(0)
    last = pl.num_programs(0) - 1
    d = hm_ref.shape[1]

    def spare_blocks(fn):
        def body(i, c):
            fn(_block_copy(zero_ref, xs_ref, cap_ref[0] + i, sem_z))
            return c
        lax.fori_loop(0, cap_ref[1], body, 0)

    @pl.when(s == 0)
    def _():
        zero_ref[...] = jnp.zeros_like(zero_ref)
        spare_blocks(lambda cp: cp.start())

        def per_expert(e, carry):
            n = tail_ref[1, e]
            off = tail_ref[0, e]

            def start(i, c):
                _piece_copy(zero_ref, 0, xs_ref, off + i * ROW_ALIGN, sem.at[0]).start()
                return c

            def wait(i, c):
                _piece_copy(zero_ref, 0, xs_ref, 0, sem.at[0]).wait()
                return c

            lax.fori_loop(0, n, start, 0)
            lax.fori_loop(0, n, wait, 0)
            return carry

        lax.fori_loop(0, N_EXPERTS, per_expert, 0)

    def drain(u, tile):
        _wait_pieces(stage_ref.at[u], xs_ref, np_ref[tile], sem.at[u])

    for u in range(2):
        tile = 2 * s + u

        @pl.when(s > 0)
        def _():
            drain(u, tile - 2)

        x16 = hm_ref[u * TOK_TILE:(u + 1) * TOK_TILE, :]
        tr = mt_ref[u, TOP_K:2 * TOP_K, :].astype(jnp.int32)
        g = mt_ref[u, 2 * TOP_K:3 * TOP_K, :]
        for c in range(STAGE_ROWS // TOK_TILE):
            rows = slice(c * TOK_TILE, (c + 1) * TOK_TILE)
            rid = lax.broadcasted_iota(jnp.int32, (TOK_TILE, TOK_TILE), 0) + c * TOK_TILE
            hot = [rid == tr[k:k + 1, :] for k in range(TOP_K)]
            sel = jnp.where(hot[0] | hot[1] | hot[2] | hot[3], 1.0, 0.0).astype(BF16)
            stage_ref[u, rows, 0:d] = _dot(sel, x16)
            gsum = jnp.zeros((TOK_TILE, 1), F32)
            for k in range(TOP_K):
                gsum = gsum + jnp.sum(jnp.where(hot[k], g[k:k + 1, :], 0.0), axis=-1, keepdims=True)
            stage_ref[u, rows, d:d + LANES] = jnp.broadcast_to(gsum, (TOK_TILE, LANES))

        def start(q, c):
            _piece_copy(stage_ref.at[u], q * ROW_ALIGN, xs_ref, pdst_ref[tile, q], sem.at[u]).start()
            return c

        lax.fori_loop(0, np_ref[tile], start, 0)

    @pl.when(s == last)
    def _():
        drain(0, 2 * s)
        drain(1, 2 * s + 1)
        spare_blocks(lambda cp: cp.wait())


def _dispatch(hm, pdst, n_pieces, tail_info, spare, meta_t, n_rows):
    n_tok, d = hm.shape
    n_tiles = n_tok // TOK_TILE
    assert n_tiles % 2 == 0
    smem = pl.BlockSpec(memory_space=pltpu.SMEM)
    return pl.pallas_call(
        _dispatch_kernel,
        grid=(n_tiles // 2,),
        in_specs=[
            smem, smem, smem, smem,
            pl.BlockSpec((2, META_ROWS, TOK_TILE), lambda s: (s, 0, 0)),
            pl.BlockSpec((2 * TOK_TILE, d), lambda s: (s, 0)),
        ],
        out_specs=pl.BlockSpec(memory_space=pl.ANY),
        out_shape=jax.ShapeDtypeStruct((n_rows, d + LANES), F32),
        scratch_shapes=[pltpu.VMEM((2, STAGE_ROWS, d + LANES), F32), pltpu.VMEM((MOE_ROWS, d + LANES), F32),
                        pltpu.SemaphoreType.DMA((2,)), pltpu.SemaphoreType.DMA],
        compiler_params=pltpu.CompilerParams(
            dimension_semantics=("arbitrary",), vmem_limit_bytes=VMEM_LIMIT_BYTES),
        name="moe_dispatch",
    )(pdst, n_pieces, tail_info, spare, meta_t, hm)


def _expert_kernel(be_ref, nu_ref, first_ref, slot_ref, nxt_ref, xs_ref, wgu_hbm, bgu_ref, wd_hbm,
                   bd_ref, ys_ref, wgu32_ref, wd32_ref, wgu16_ref, wd16_ref, sem):
    i = pl.program_id(0)
    f, d = wd16_ref.shape

    def weight_copies(e, slot):
        return (pltpu.make_async_copy(wgu_hbm.at[e], wgu32_ref.at[slot], sem.at[0, slot]),
                pltpu.make_async_copy(wd_hbm.at[e], wd32_ref.at[slot], sem.at[1, slot]))

    @pl.when(i == 0)
    def _():
        for cp in weight_copies(be_ref[0], 0):
            cp.start()

    @pl.when(first_ref[i] == 1)
    def _():
        slot = slot_ref[i]
        for cp in weight_copies(be_ref[i], slot):
            cp.wait()

        @pl.when(nxt_ref[i] >= 0)
        def _():
            for cp in weight_copies(nxt_ref[i], 1 - slot):
                cp.start()

        wgu16_ref[...] = wgu32_ref[slot].astype(BF16)
        wd16_ref[...] = wd32_ref[slot].astype(BF16)

    @pl.when(i < nu_ref[0])
    def _():
        gu = _dot(xs_ref[:, 0:d].astype(BF16), wgu16_ref[...]) + bgu_ref[0]
        gate = jnp.minimum(gu[:, :f], SWIGLU_LIMIT)
        up = jnp.clip(gu[:, f:], -SWIGLU_LIMIT, SWIGLU_LIMIT)
        act = (up + 1.0) * gate * jax.nn.sigmoid(SWIGLU_ALPHA * gate)
        ys_ref[...] = (_dot(act.astype(BF16), wd16_ref[...]) + bd_ref[0]) * xs_ref[:, d:d + 1]

    @pl.when(i >= nu_ref[0])
    def _():
        ys_ref[...] = jnp.zeros_like(ys_ref)


def _expert_mlp(xs, block_e, n_used, first, slot, nxt, w_gu, b_gu, w_d, b_d):
    n_rows = xs.shape[0]
    d = w_d.shape[2]
    n_blocks = n_rows // MOE_ROWS
    f = w_d.shape[1]
    row_map = lambda i, be, nu, *_: (jnp.minimum(i, nu[0] - 1), 0)
    exp_map = lambda i, be, *_: (be[i], 0, 0)
    return pl.pallas_call(
        _expert_kernel,
        grid_spec=pltpu.PrefetchScalarGridSpec(
            num_scalar_prefetch=5,
            grid=(n_blocks,),
            in_specs=[
                pl.BlockSpec((MOE_ROWS, d + LANES), row_map),
                pl.BlockSpec(memory_space=pl.ANY),
                pl.BlockSpec((1, 1, 2 * f), exp_map),
                pl.BlockSpec(memory_space=pl.ANY),
                pl.BlockSpec((1, 1, d), exp_map),
            ],
            out_specs=pl.BlockSpec((MOE_ROWS, d), lambda i, *_: (i, 0)),
            scratch_shapes=[
                pltpu.VMEM((2, d, 2 * f), F32), pltpu.VMEM((2, f, d), F32),
                pltpu.VMEM((d, 2 * f), BF16), pltpu.VMEM((f, d), BF16),
                pltpu.SemaphoreType.DMA((2, 2)),
            ],
        ),
        out_shape=jax.ShapeDtypeStruct((n_rows, d), F32),
        compiler_params=pltpu.CompilerParams(
            dimension_semantics=("arbitrary",), vmem_limit_bytes=VMEM_LIMIT_BYTES),
        name="moe_experts",
    )(block_e, n_used, first, slot, nxt, xs, w_gu, b_gu, w_d, b_d)


def _combine_kernel(pdst_ref, np_ref, x2_ref, meta_ref, nfw_ref, ys_ref, out_ref, ybuf_ref, sem):
    s = pl.program_id(0)
    last = pl.num_programs(0) - 1

    def fetch(u, tile):
        def start(q, c):
            _piece_copy(ys_ref, pdst_ref[tile, q], ybuf_ref.at[u], q * ROW_ALIGN, sem.at[u]).start()
            return c
        lax.fori_loop(0, np_ref[tile], start, 0)

    def finish(u, tile):
        _wait_pieces(ys_ref, ybuf_ref.at[u], np_ref[tile], sem.at[u])

        rows = slice(u * TOK_TILE, (u + 1) * TOK_TILE)
        tr = meta_ref[rows, TOP_K:2 * TOP_K].astype(jnp.int32)
        acc = x2_ref[rows, :]
        for c in range(STAGE_ROWS // TOK_TILE):
            cid = lax.broadcasted_iota(jnp.int32, (TOK_TILE, TOK_TILE), 1) + c * TOK_TILE
            hot = [cid == tr[:, k:k + 1] for k in range(TOP_K)]
            sel = jnp.where(hot[0] | hot[1] | hot[2] | hot[3], 1.0, 0.0).astype(BF16)
            acc = acc + _dot(sel, ybuf_ref[u, c * TOK_TILE:(c + 1) * TOK_TILE, :].astype(BF16))
        out_ref[rows, :] = _rms(acc, nfw_ref[...])

    @pl.when(s == 0)
    def _():
        ybuf_ref[...] = jnp.zeros_like(ybuf_ref)
        fetch(0, 0)

    fetch(1, 2 * s + 1)
    finish(0, 2 * s)

    @pl.when(s < last)
    def _():
        fetch(0, 2 * s + 2)

    finish(1, 2 * s + 1)


def _combine(x2, meta, pdst, n_pieces, ys, norm_final_w):
    n_tok, d = x2.shape
    n_tiles = n_tok // TOK_TILE
    assert n_tiles % 2 == 0
    smem = pl.BlockSpec(memory_space=pltpu.SMEM)
    return pl.pallas_call(
        _combine_kernel,
        grid=(n_tiles // 2,),
        in_specs=[
            smem, smem,
            pl.BlockSpec((2 * TOK_TILE, d), lambda s: (s, 0)),
            pl.BlockSpec((2 * TOK_TILE, LANES), lambda s: (s, 0)),
            pl.BlockSpec((1, d), lambda s: (0, 0)),
            pl.BlockSpec(memory_space=pl.ANY),
        ],
        out_specs=pl.BlockSpec((2 * TOK_TILE, d), lambda s: (s, 0)),
        out_shape=jax.ShapeDtypeStruct((n_tok, d), F32),
        scratch_shapes=[pltpu.VMEM((2, STAGE_ROWS, d), F32), pltpu.SemaphoreType.DMA((2,))],
        compiler_params=pltpu.CompilerParams(
            dimension_semantics=("arbitrary",), vmem_limit_bytes=VMEM_LIMIT_BYTES),
        name="moe_combine",
    )(pdst, n_pieces, x2, meta, norm_final_w, ys)


def _pad_heads(w, n_heads, width):
    lead = w.shape[:-1]
    w = w.reshape(lead + (n_heads, width))
    w = jnp.pad(w, [(0, 0)] * len(lead) + [(0, 0), (0, HEAD_W - width)])
    return w.reshape(lead + (n_heads * HEAD_W,))


def _round_up(x, m):
    return (x + m - 1) // m * m


def _routing_tables(cnt, n_tiles):
    i32 = jnp.int32
    counts = cnt.reshape(n_tiles, 8, LANES)[:, 0, :N_EXPERTS].astype(i32)
    cnt8 = _round_up(counts, ROW_ALIGN)
    lend = jnp.cumsum(cnt8, axis=1)
    lstart = lend - cnt8
    tot = jnp.sum(cnt8, axis=0)
    padded = _round_up(tot, MOE_ROWS)
    pends = jnp.cumsum(padded)
    pstarts = pends - padded
    goff = pstarts[None, :] + jnp.cumsum(cnt8, axis=0) - cnt8

    prow = jnp.arange(MAX_PIECES, dtype=i32) * ROW_ALIGN
    pe = jnp.sum(prow[None, :, None] >= lend[:, None, :], axis=-1)
    pe = jnp.minimum(pe, N_EXPERTS - 1)
    pick = pe[:, :, None] == jnp.arange(N_EXPERTS, dtype=i32)
    pdst = (jnp.sum(jnp.where(pick, (goff - lstart)[:, None, :], 0), axis=-1) + prow[None, :]).astype(i32)
    n_pieces = (lend[:, -1] // ROW_ALIGN).astype(i32)
    pdst = jnp.where(jnp.arange(MAX_PIECES, dtype=i32)[None, :] < n_pieces[:, None], pdst, 0)

    max_rows = n_tiles * (TOK_TILE * TOP_K + N_EXPERTS * (ROW_ALIGN - 1)) + N_EXPERTS * (MOE_ROWS - ROW_ALIGN)
    n_blocks = -(-max_rows // MOE_ROWS)
    n_used = (pends[-1] // MOE_ROWS).astype(i32)
    blk = jnp.arange(n_blocks, dtype=i32)
    block_e = jnp.minimum(jnp.sum(blk[:, None] * MOE_ROWS >= pends[None, :], axis=-1), N_EXPERTS - 1).astype(i32)
    block_e = jnp.where(blk < n_used, block_e, block_e[jnp.maximum(n_used - 1, 0)])
    first = ((blk < n_used) & ((blk == 0) | (block_e != jnp.roll(block_e, 1)))).astype(i32)
    has_rows = padded > 0
    slot_e = (jnp.cumsum(has_rows.astype(i32)) - 1) % 2
    later = jnp.arange(N_EXPERTS, dtype=i32)[None, :] > jnp.arange(N_EXPERTS, dtype=i32)[:, None]
    nxt_e = jnp.min(jnp.where(later & has_rows[None, :], jnp.arange(N_EXPERTS, dtype=i32)[None, :], N_EXPERTS), axis=1)
    nxt_e = jnp.where(nxt_e < N_EXPERTS, nxt_e, -1).astype(i32)
    tail_info = jnp.stack([pstarts + tot, (padded - tot) // ROW_ALIGN]).astype(i32)
    spare = jnp.stack([n_used, n_blocks - n_used]).astype(i32)
    of_block = block_e[:, None] == jnp.arange(N_EXPERTS, dtype=i32)
    slot_b = jnp.sum(jnp.where(of_block, slot_e[None, :], 0), axis=1).astype(i32)
    nxt_b = jnp.sum(jnp.where(of_block, nxt_e[None, :], 0), axis=1).astype(i32)
    return (pdst, n_pieces, tail_info, spare,
            (block_e, n_used.reshape(1), first, slot_b, nxt_b), n_blocks * MOE_ROWS)


def kernel(x, mem, norm_mix_w, w_in, hg_lb_logits, hg_onorm_w, gla_w_gk2, gla_b_gk, gla_onorm_w, w_out, norm_xa_w, norm_mem_w, w_xq, w_xkv, w_xo, norm_moe_w, w_router, b_router, w_gate_up, b_gate_up, w_down, b_down, norm_final_w):
    assert w_in.shape[0] == 1, "single-layer block"
    bsz, seq, d = x.shape
    n_tok = bsz * seq
    gk = GLA_HEADS * GLA_KDIM

    wi = w_in[0]
    w_in_p = jnp.concatenate([
        wi[:, :4 * _W],
        _pad_heads(wi[:, 4 * _W:4 * _W + gk], GLA_HEADS, GLA_KDIM),
        _pad_heads(wi[:, 4 * _W + gk:4 * _W + 2 * gk], GLA_HEADS, GLA_KDIM),
        wi[:, 4 * _W + 2 * gk:4 * _W + 2 * gk + 2 * _W],
        jnp.pad(wi[:, 4 * _W + 2 * gk + 2 * _W:], ((0, 0), (0, LANES - GLA_RANK))),
    ], axis=1).astype(BF16)
    w_gk2_p = jnp.pad(_pad_heads(gla_w_gk2[0], GLA_HEADS, GLA_KDIM),
                      ((0, LANES - GLA_RANK), (0, 0))).astype(BF16)
    b_gk_p = _pad_heads(gla_b_gk, GLA_HEADS, GLA_KDIM)
    wr = jnp.pad(w_router[0], ((0, 0), (0, LANES - N_EXPERTS)))
    wr_hi = wr.astype(BF16)
    wr_lo = (wr - wr_hi.astype(F32)).astype(BF16)
    br_p = jnp.pad(b_router, ((0, 0), (0, LANES - N_EXPERTS)), constant_values=-1e30)

    x1 = _token_mix(x, norm_mix_w, w_in_p, hg_lb_logits, hg_onorm_w, w_gk2_p, b_gk_p, gla_onorm_w,
                    w_out[0].astype(BF16))
    x2, hm, meta, meta_t, cnt = _xattn_router(x1, mem, norm_xa_w, norm_mem_w, w_xq[0].astype(BF16),
                                              w_xkv[0].astype(BF16), w_xo[0].astype(BF16), norm_moe_w,
                                              wr_hi, wr_lo, br_p)

    n_tiles = n_tok // TOK_TILE
    pdst, n_pieces, tail_info, spare, block_tables, n_rows = _routing_tables(cnt, n_tiles)

    xs = _dispatch(hm.reshape(n_tok, d), pdst, n_pieces, tail_info, spare, meta_t, n_rows)
    ys = _expert_mlp(xs, *block_tables, w_gate_up[0], b_gate_up[0][:, None, :],
                     w_down[0], b_down[0][:, None, :])
    out = _combine(x2.reshape(n_tok, d), meta, pdst, n_pieces, ys, norm_final_w[None, :])
    return out.reshape(bsz, seq, d)
```

```python
import functools

import jax
import jax.numpy as jnp
from jax import lax
from jax.experimental import pallas as pl
from jax.experimental.pallas import tpu as pltpu

F32 = jnp.float32
BF16 = jnp.bfloat16
EPS = 1e-6
LOG2E = 1.4426950408889634

HEAD_W = 128
HG_HEADS = 4
GLA_HEADS = 4
GLA_KDIM = 64
N_HEADS = HG_HEADS + GLA_HEADS
GLA_RANK = 16
GLA_GATE_NORMALIZER = 16.0
CHUNK = 64
XA_HEADS = 4
N_EXPERTS = 32
TOP_K = 4
SWIGLU_LIMIT = 7.0
SWIGLU_ALPHA = 1.702

LANES = 128
VMEM_LIMIT_BYTES = 56 * 1024 * 1024

MIX_ROWS = 256
TOK_TILE = 256
MOE_ROWS = 512
ROW_ALIGN = 8
STAGE_ROWS = 1280
MAX_PIECES = STAGE_ROWS // ROW_ALIGN
MOE_TILES = 4
META_ROWS = 16

_W = HG_HEADS * HEAD_W
COL_HQ, COL_HF, COL_HI, COL_HGATE = 0, _W, 2 * _W, 3 * _W
COL_GQ, COL_GK, COL_GV, COL_GGATE = 4 * _W, 5 * _W, 6 * _W, 7 * _W
COL_GLR = 8 * _W
PROJ_W = COL_GLR + LANES


def _rms(x, w):
    return x * lax.rsqrt(jnp.mean(x * x, axis=-1, keepdims=True) + EPS) * w


def _dot(a, b):
    return jnp.dot(a, b, preferred_element_type=F32)


def _dot_nt(a, b):
    return lax.dot_general(a, b, (((1,), (1,)), ((), ())), preferred_element_type=F32)


def _dot_tn(a, b):
    return lax.dot_general(a, b, (((0,), (0,)), ((), ())), preferred_element_type=F32)


def _head_chunk(hd, q_scr, k_scr, b2_scr, v, st_ref, sc_ref, masks):
    sl = slice(hd * HEAD_W, (hd + 1) * HEAD_W)
    q, k, b2 = q_scr[:, sl], k_scr[:, sl], b2_scr[:, sl]

    for s in range(0, CHUNK, 8):
        q_blk, b_blk = q[s:s + 8], b2[s:s + 8]
        for j in range(s, s + 8):
            e = jnp.exp2(b_blk - b2_scr[j:j + 1, sl])
            sc_ref[s:s + 8, j:j + 1] = jnp.sum(q_blk * (k_scr[j:j + 1, sl] * e), axis=-1, keepdims=True)
    scores = jnp.where(masks[0], sc_ref[...], 0.0)

    for lvl, size in enumerate((16, 32, 64)):
        mids = [b2_scr[r:r + 1, sl] for r in range(size // 2 - 1, CHUNK, size)]
        mid = jnp.concatenate([jnp.broadcast_to(m, (size, HEAD_W)) for m in mids], axis=0)
        dist = b2 - mid
        e = jnp.exp2(jnp.minimum(dist, -dist))
        r = _dot_nt((q * e).astype(BF16), (k * e).astype(BF16))
        scores = jnp.where(masks[lvl + 1], r, scores)

    b2_last = b2_scr[CHUNK - 1:CHUNK, sl]
    q_abs = (q * jnp.exp2(b2)).astype(BF16)
    k_end = (k * jnp.exp2(b2_last - b2)).astype(BF16)
    v16 = v.astype(BF16)
    st = st_ref[...]
    o = _dot(scores.astype(BF16), v16) + _dot_nt(q_abs, st.astype(BF16))
    st_ref[...] = st * jnp.exp2(b2_last) + _dot_tn(v16, k_end)
    return o


def _mix_kernel(x_ref, nw_ref, win_ref, lbl_ref, hgw_ref, wgk_ref, bgk_ref, glw_ref, wout_ref,
                o_ref, proj_ref, oall_ref, st_ref, q_scr, k_scr, b2_scr, sc_scr):
    @pl.when(pl.program_id(1) == 0)
    def _():
        st_ref[...] = jnp.zeros_like(st_ref)
        sc_scr[...] = jnp.zeros_like(sc_scr)

    x = x_ref[0]
    h = _rms(x, nw_ref[...]).astype(BF16)
    proj_ref[...] = _dot(h, win_ref[...])

    lbl = lbl_ref[...]
    e = jnp.exp(lbl - jnp.max(lbl, axis=0, keepdims=True))
    lb = e[0:1] / jnp.sum(e, axis=0, keepdims=True)

    ri = lax.broadcasted_iota(jnp.int32, (CHUNK, CHUNK), 0)
    ci = lax.broadcasted_iota(jnp.int32, (CHUNK, CHUNK), 1)
    tri = jnp.where(ri >= ci, 1.0, 0.0).astype(BF16)
    masks = [(ri // 8 == ci // 8) & (ri >= ci)]
    for size in (16, 32, 64):
        masks.append((ri // size == ci // size) & (ri // (size // 2) > ci // (size // 2)))

    def chunk(ck):
        rows = pl.ds(ck * CHUNK, CHUNK)
        f = lb + (1.0 - lb) * jax.nn.sigmoid(proj_ref[rows, COL_HF:COL_HF + _W])
        g_hg = jnp.log(f)
        z = _dot(proj_ref[rows, COL_GLR:COL_GLR + LANES].astype(BF16), wgk_ref[...]) + bgk_ref[...]
        g_gla = (jnp.minimum(z, 0.0) - jnp.log1p(jnp.exp(-jnp.abs(z)))) * (1.0 / GLA_GATE_NORMALIZER)
        g_all = jnp.concatenate([g_hg, g_gla], axis=-1)
        g_hi = g_all.astype(BF16)
        g_lo = (g_all - g_hi.astype(F32)).astype(BF16)
        b2_scr[...] = (_dot(tri, g_hi) + _dot(tri, g_lo)) * LOG2E
        hq = proj_ref[rows, COL_HQ:COL_HQ + _W]
        q_scr[:, 0:_W] = hq * jax.nn.sigmoid(hq)
        q_scr[:, _W:2 * _W] = proj_ref[rows, COL_GQ:COL_GQ + _W] * (GLA_KDIM ** -0.5)
        k_scr[:, 0:_W] = 1.0 - f
        k_scr[:, _W:2 * _W] = proj_ref[rows, COL_GK:COL_GK + _W]

        for hd in range(N_HEADS):
            lo = hd * HEAD_W
            col_v = COL_HI + lo if hd < HG_HEADS else COL_GV + lo - _W
            o = _head_chunk(hd, q_scr, k_scr, b2_scr, proj_ref[rows, col_v:col_v + HEAD_W],
                            st_ref.at[hd], sc_scr.at[hd], masks)
            oall_ref[rows, lo:lo + HEAD_W] = o

    for ck in range(MIX_ROWS // CHUNK):
        chunk(ck)

    ys = []
    for hd in range(N_HEADS):
        lo = hd * HEAD_W
        o = oall_ref[:, lo:lo + HEAD_W]
        if hd < HG_HEADS:
            w, gate = hgw_ref[...], proj_ref[:, COL_HGATE + lo:COL_HGATE + lo + HEAD_W]
        else:
            w, gate = glw_ref[...], proj_ref[:, COL_GGATE + lo - _W:COL_GGATE + lo - _W + HEAD_W]
        ys.append((_rms(o, w) * (gate * jax.nn.sigmoid(gate))).astype(BF16))
    y = jnp.concatenate(ys, axis=-1)
    o_ref[0] = x + _dot(y, wout_ref[...])


def _token_mix(x, norm_w, w_in_p, lb_logits, hg_onorm_w, w_gk2_p, b_gk_p, gla_onorm_w, w_out):
    bsz, seq, d = x.shape
    const = lambda shape: pl.BlockSpec(shape, lambda b, t: (0,) * len(shape))
    return pl.pallas_call(
        _mix_kernel,
        grid=(bsz, seq // MIX_ROWS),
        in_specs=[
            pl.BlockSpec((1, MIX_ROWS, d), lambda b, t: (b, t, 0)),
            const((1, d)),
            const((d, PROJ_W)),
            const(lb_logits.shape),
            const((1, HEAD_W)),
            const((LANES, _W)),
            const((1, _W)),
            const((1, HEAD_W)),
            const((2 * _W, d)),
        ],
        out_specs=pl.BlockSpec((1, MIX_ROWS, d), lambda b, t: (b, t, 0)),
        out_shape=jax.ShapeDtypeStruct((bsz, seq, d), F32),
        scratch_shapes=[
            pltpu.VMEM((MIX_ROWS, PROJ_W), F32),
            pltpu.VMEM((MIX_ROWS, N_HEADS * HEAD_W), F32),
            pltpu.VMEM((N_HEADS, HEAD_W, HEAD_W), F32),
            pltpu.VMEM((CHUNK, N_HEADS * HEAD_W), F32),
            pltpu.VMEM((CHUNK, N_HEADS * HEAD_W), F32),
            pltpu.VMEM((CHUNK, N_HEADS * HEAD_W), F32),
            pltpu.VMEM((N_HEADS, CHUNK, CHUNK), F32),
        ],
        compiler_params=pltpu.CompilerParams(
            dimension_semantics=("arbitrary", "arbitrary"), vmem_limit_bytes=VMEM_LIMIT_BYTES),
        name="token_mix",
    )(x, norm_w, w_in_p, lb_logits, hg_onorm_w, w_gk2_p, b_gk_p, gla_onorm_w, w_out)


def _xattn_tile(x, k_scr, v_scr, nxw_ref, wq_ref, wo_ref, nmoe_ref, wr_hi_ref, wr_lo_ref, br_ref):
    d = x.shape[-1]
    hdim = d // XA_HEADS
    q = _dot(_rms(x, nxw_ref[...]).astype(BF16), wq_ref[...])
    outs = []
    for h in range(XA_HEADS):
        sl = slice(h * hdim, (h + 1) * hdim)
        s = _dot_nt(q[:, sl].astype(BF16), k_scr[:, sl]) * (hdim ** -0.5)
        p = jnp.exp(s - jnp.max(s, axis=-1, keepdims=True))
        p = p / jnp.sum(p, axis=-1, keepdims=True)
        outs.append(_dot(p.astype(BF16), v_scr[:, sl]).astype(BF16))
    x2 = x + _dot(jnp.concatenate(outs, axis=-1), wo_ref[...])

    hm = _rms(x2, nmoe_ref[...])
    hm_hi = hm.astype(BF16)

    hm_lo = (hm - hm_hi.astype(F32)).astype(BF16)
    logits = (_dot(hm_hi, wr_hi_ref[...]) + _dot(hm_lo, wr_hi_ref[...]) + _dot(hm_hi, wr_lo_ref[...])
              + br_ref[...])

    rows = logits.shape[0]
    lane = lax.broadcasted_iota(jnp.int32, (rows, LANES), 1)
    lane_f = lane.astype(F32)
    neg_inf = jnp.float32(-jnp.inf)
    top_v, top_i, hots = [], [], []
    work = logits
    for _ in range(TOP_K):
        m = jnp.max(work, axis=-1, keepdims=True)
        idx = jnp.min(jnp.where(work == m, lane_f, float(LANES)), axis=-1, keepdims=True)
        hot = lane_f == idx
        work = jnp.where(hot, neg_inf, work)
        top_v.append(m)
        top_i.append(idx)
        hots.append(hot)
    es = [jnp.exp(v - top_v[0]) for v in top_v]
    denom = es[0] + es[1] + es[2] + es[3]
    gates = [e / denom for e in es]

    chosen = jnp.where(hots[0] | hots[1] | hots[2] | hots[3], 1.0, 0.0)
    ri = lax.broadcasted_iota(jnp.int32, (rows, rows), 0)
    ci = lax.broadcasted_iota(jnp.int32, (rows, rows), 1)
    strict = jnp.where(ri > ci, 1.0, 0.0).astype(BF16)
    before = _dot(strict, chosen.astype(BF16))
    counts = jnp.sum(chosen, axis=0, keepdims=True)
    padded = jnp.floor((counts + (ROW_ALIGN - 1.0)) * (1.0 / ROW_ALIGN)) * ROW_ALIGN
    li = lax.broadcasted_iota(jnp.int32, (LANES, LANES), 0)
    lj = lax.broadcasted_iota(jnp.int32, (LANES, LANES), 1)
    group_start = _dot(jnp.broadcast_to(padded, (8, LANES)).astype(BF16),
                       jnp.where(li < lj, 1.0, 0.0).astype(BF16))[0:1]
    slot_of = before + group_start
    slots = [jnp.sum(jnp.where(hot, slot_of, 0.0), axis=-1, keepdims=True) for hot in hots]

    meta = jnp.zeros((rows, LANES), F32)
    for j, col in enumerate(top_i + slots + gates):
        meta = jnp.where(lane == j, col, meta)
    return x2, hm_hi, meta, counts


def _xattn_kernel(x_ref, mem_ref, nxw_ref, nmw_ref, wq_ref, wkv_ref, wo_ref, nmoe_ref,
                  wr_hi_ref, wr_lo_ref, br_ref,
                  x2_ref, hm_ref, meta_ref, meta_t_ref, cnt_ref, k_scr, v_scr):
    d = x_ref.shape[-1]

    @pl.when(pl.program_id(1) == 0)
    def _():
        m = _rms(mem_ref[0], nmw_ref[...]).astype(BF16)
        kv = _dot(m, wkv_ref[...])
        k_scr[...] = kv[:, :d].astype(BF16)
        v_scr[...] = kv[:, d:].astype(BF16)

    for u in range(2):
        rows = slice(u * TOK_TILE, (u + 1) * TOK_TILE)
        x2, hm16, meta, counts = _xattn_tile(x_ref[0, rows, :], k_scr, v_scr, nxw_ref, wq_ref, wo_ref, nmoe_ref,
                                             wr_hi_ref, wr_lo_ref, br_ref)
        x2_ref[0, rows, :] = x2
        hm_ref[0, rows, :] = hm16
        meta_ref[rows, :] = meta
        meta_t_ref[u] = meta.T[0:META_ROWS, :]
        cnt_ref[u * 8:(u + 1) * 8, :] = jnp.broadcast_to(counts, (8, LANES))


def _xattn_router(x1, mem, norm_xa_w, norm_mem_w, w_xq, w_xkv, w_xo, norm_moe_w, wr_hi, wr_lo, br_p):
    bsz, seq, d = x1.shape
    mlen = mem.shape[1]
    n_t = seq // (2 * TOK_TILE)
    const = lambda shape: pl.BlockSpec(shape, lambda b, t: (0,) * len(shape))
    return pl.pallas_call(
        _xattn_kernel,
        grid=(bsz, n_t),
        in_specs=[
            pl.BlockSpec((1, 2 * TOK_TILE, d), lambda b, t: (b, t, 0)),
            pl.BlockSpec((1, mlen, d), lambda b, t: (b, 0, 0)),
            const((1, d)), const((1, d)),
            const((d, d)), const((d, 2 * d)), const((d, d)),
            const((1, d)),
            const((d, LANES)), const((d, LANES)), const((1, LANES)),
        ],
        out_specs=[
            pl.BlockSpec((1, 2 * TOK_TILE, d), lambda b, t: (b, t, 0)),
            pl.BlockSpec((1, 2 * TOK_TILE, d), lambda b, t: (b, t, 0)),
            pl.BlockSpec((2 * TOK_TILE, LANES), lambda b, t: (b * n_t + t, 0)),
            pl.BlockSpec((2, META_ROWS, TOK_TILE), lambda b, t: (b * n_t + t, 0, 0)),
            pl.BlockSpec((16, LANES), lambda b, t: (b * n_t + t, 0)),
        ],
        out_shape=[
            jax.ShapeDtypeStruct((bsz, seq, d), F32),
            jax.ShapeDtypeStruct((bsz, seq, d), BF16),
            jax.ShapeDtypeStruct((bsz * seq, LANES), F32),
            jax.ShapeDtypeStruct((bsz * seq // TOK_TILE, META_ROWS, TOK_TILE), F32),
            jax.ShapeDtypeStruct((bsz * seq // TOK_TILE * 8, LANES), F32),
        ],
        scratch_shapes=[
            pltpu.VMEM((mlen, d), BF16),
            pltpu.VMEM((mlen, d), BF16),
        ],
        compiler_params=pltpu.CompilerParams(
            dimension_semantics=("arbitrary", "arbitrary"), vmem_limit_bytes=VMEM_LIMIT_BYTES),
        name="xattn_router",
    )(x1, mem, norm_xa_w, norm_mem_w, w_xq, w_xkv, w_xo, norm_moe_w, wr_hi, wr_lo, br_p)


def _piece_copy(src_ref, src_row, dst_ref, dst_row, sem):
    return pltpu.make_async_copy(src_ref.at[pl.ds(pl.multiple_of(src_row, ROW_ALIGN), ROW_ALIGN)],
                                 dst_ref.at[pl.ds(pl.multiple_of(dst_row, ROW_ALIGN), ROW_ALIGN)], sem)


def _wait_pieces(src_ref, dst_ref, n, sem):
    size = MAX_PIECES
    while size & (size - 1):
        size &= size - 1
    while size >= 1:
        @pl.when((n & size) != 0)
        def _(size=size):
            pltpu.make_async_copy(src_ref.at[pl.ds(0, size * ROW_ALIGN)], dst_ref.at[pl.ds(0, size * ROW_ALIGN)], sem).wait()
        size //= 2


def _block_copy(src_ref, dst_ref, dst_block, sem):
    return pltpu.make_async_copy(src_ref, dst_ref.at[pl.ds(pl.multiple_of(dst_block * MOE_ROWS, MOE_ROWS), MOE_ROWS)], sem)


def _dispatch_kernel(pdst_ref, np_ref, tail_ref, cap_ref, mt_ref, hm_ref, xs_ref, stage_ref, zero_ref, sem, sem_z):
    s = pl.program_id(0)
    last = pl.num_programs(0) - 1
    d = hm_ref.shape[1]

    def spare_blocks(fn):
        def body(i, c):
            fn(_block_copy(zero_ref, xs_ref, cap_ref[0] + i, sem_z))
            return c
        lax.fori_loop(0, cap_ref[1], body, 0)

    @pl.when(s == 0)
    def _():
        zero_ref[...] = jnp.zeros_like(zero_ref)
        spare_blocks(lambda cp: cp.start())

        def per_expert(e, carry):
            n = tail_ref[1, e]
            off = tail_ref[0, e]

            def start(i, c):
                _piece_copy(zero_ref, 0, xs_ref, off + i * ROW_ALIGN, sem.at[0]).start()
                return c

            def wait(i, c):
                _piece_copy(zero_ref, 0, xs_ref, 0, sem.at[0]).wait()
                return c

            lax.fori_loop(0, n, start, 0)
            lax.fori_loop(0, n, wait, 0)
            return carry

        lax.fori_loop(0, N_EXPERTS, per_expert, 0)

    def drain(u, tile):
        _wait_pieces(stage_ref.at[u], xs_ref, np_ref[tile], sem.at[u])

    for u in range(MOE_TILES):
        tile = MOE_TILES * s + u

        @pl.when(s > 0)
        def _():
            drain(u, tile - MOE_TILES)

        x16 = hm_ref[u * TOK_TILE:(u + 1) * TOK_TILE, :]
        tr = mt_ref[u, TOP_K:2 * TOP_K, :].astype(jnp.int32)
        g = mt_ref[u, 2 * TOP_K:3 * TOP_K, :]
        for c in range(STAGE_ROWS // TOK_TILE):
            rows = slice(c * TOK_TILE, (c + 1) * TOK_TILE)
            rid = lax.broadcasted_iota(jnp.int32, (TOK_TILE, TOK_TILE), 0) + c * TOK_TILE
            hot = [rid == tr[k:k + 1, :] for k in range(TOP_K)]
            sel = jnp.where(hot[0] | hot[1] | hot[2] | hot[3], 1.0, 0.0).astype(BF16)
            stage_ref[u, rows, 0:d] = _dot(sel, x16)
            gsum = jnp.zeros((TOK_TILE, 1), F32)
            for k in range(TOP_K):
                gsum = gsum + jnp.sum(jnp.where(hot[k], g[k:k + 1, :], 0.0), axis=-1, keepdims=True)
            stage_ref[u, rows, d:d + LANES] = jnp.broadcast_to(gsum, (TOK_TILE, LANES))

        def start(q, c):
            _piece_copy(stage_ref.at[u], q * ROW_ALIGN, xs_ref, pdst_ref[tile, q], sem.at[u]).start()
            return c

        lax.fori_loop(0, np_ref[tile], start, 0)

    @pl.when(s == last)
    def _():
        for u in range(MOE_TILES):
            drain(u, MOE_TILES * s + u)
        spare_blocks(lambda cp: cp.wait())


def _dispatch(hm, pdst, n_pieces, tail_info, spare, meta_t, n_rows):
    n_tok, d = hm.shape
    n_tiles = n_tok // TOK_TILE
    assert n_tiles % MOE_TILES == 0
    smem = pl.BlockSpec(memory_space=pltpu.SMEM)
    return pl.pallas_call(
        _dispatch_kernel,
        grid=(n_tiles // MOE_TILES,),
        in_specs=[
            smem, smem, smem, smem,
            pl.BlockSpec((MOE_TILES, META_ROWS, TOK_TILE), lambda s: (s, 0, 0)),
            pl.BlockSpec((MOE_TILES * TOK_TILE, d), lambda s: (s, 0)),
        ],
        out_specs=pl.BlockSpec(memory_space=pl.ANY),
        out_shape=jax.ShapeDtypeStruct((n_rows, d + LANES), F32),
        scratch_shapes=[pltpu.VMEM((MOE_TILES, STAGE_ROWS, d + LANES), F32), pltpu.VMEM((MOE_ROWS, d + LANES), F32),
                        pltpu.SemaphoreType.DMA((MOE_TILES,)), pltpu.SemaphoreType.DMA],
        compiler_params=pltpu.CompilerParams(
            dimension_semantics=("arbitrary",), vmem_limit_bytes=VMEM_LIMIT_BYTES),
        name="moe_dispatch",
    )(pdst, n_pieces, tail_info, spare, meta_t, hm)


def _expert_kernel(be_ref, nu_ref, first_ref, slot_ref, nxt_ref, xs_ref, wgu_hbm, bgu_ref, wd_hbm,
                   bd_ref, ys_ref, wgu32_ref, wd32_ref, wgu16_ref, wd16_ref, sem):
    i = pl.program_id(0)
    f, d = wd16_ref.shape

    def weight_copies(e, slot):
        return (pltpu.make_async_copy(wgu_hbm.at[e], wgu32_ref.at[slot], sem.at[0, slot]),
                pltpu.make_async_copy(wd_hbm.at[e], wd32_ref.at[slot], sem.at[1, slot]))

    @pl.when(i == 0)
    def _():
        for cp in weight_copies(be_ref[0], 0):
            cp.start()

    @pl.when(first_ref[i] == 1)
    def _():
        slot = slot_ref[i]
        for cp in weight_copies(be_ref[i], slot):
            cp.wait()

        @pl.when(nxt_ref[i] >= 0)
        def _():
            for cp in weight_copies(nxt_ref[i], 1 - slot):
                cp.start()

        wgu16_ref[...] = wgu32_ref[slot].astype(BF16)
        wd16_ref[...] = wd32_ref[slot].astype(BF16)

    @pl.when(i < nu_ref[0])
    def _():
        gu = _dot(xs_ref[:, 0:d].astype(BF16), wgu16_ref[...]) + bgu_ref[0]
        gate = jnp.minimum(gu[:, :f], SWIGLU_LIMIT)
        up = jnp.clip(gu[:, f:], -SWIGLU_LIMIT, SWIGLU_LIMIT)
        act = (up + 1.0) * gate * jax.nn.sigmoid(SWIGLU_ALPHA * gate)
        ys_ref[...] = (_dot(act.astype(BF16), wd16_ref[...]) + bd_ref[0]) * xs_ref[:, d:d + 1]

    @pl.when(i >= nu_ref[0])
    def _():
        ys_ref[...] = jnp.zeros_like(ys_ref)


def _expert_mlp(xs, block_e, n_used, first, slot, nxt, w_gu, b_gu, w_d, b_d):
    n_rows = xs.shape[0]
    d = w_d.shape[2]
    n_blocks = n_rows // MOE_ROWS
    f = w_d.shape[1]
    row_map = lambda i, be, nu, *_: (jnp.minimum(i, nu[0] - 1), 0)
    exp_map = lambda i, be, *_: (be[i], 0, 0)
    return pl.pallas_call(
        _expert_kernel,
        grid_spec=pltpu.PrefetchScalarGridSpec(
            num_scalar_prefetch=5,
            grid=(n_blocks,),
            in_specs=[
                pl.BlockSpec((MOE_ROWS, d + LANES), row_map),
                pl.BlockSpec(memory_space=pl.ANY),
                pl.BlockSpec((1, 1, 2 * f), exp_map),
                pl.BlockSpec(memory_space=pl.ANY),
                pl.BlockSpec((1, 1, d), exp_map),
            ],
            out_specs=pl.BlockSpec((MOE_ROWS, d), lambda i, *_: (i, 0)),
            scratch_shapes=[
                pltpu.VMEM((2, d, 2 * f), F32), pltpu.VMEM((2, f, d), F32),
                pltpu.VMEM((d, 2 * f), BF16), pltpu.VMEM((f, d), BF16),
                pltpu.SemaphoreType.DMA((2, 2)),
            ],
        ),
        out_shape=jax.ShapeDtypeStruct((n_rows, d), F32),
        compiler_params=pltpu.CompilerParams(
            dimension_semantics=("arbitrary",), vmem_limit_bytes=VMEM_LIMIT_BYTES),
        name="moe_experts",
    )(block_e, n_used, first, slot, nxt, xs, w_gu, b_gu, w_d, b_d)


def _combine_kernel(pdst_ref, np_ref, x2_ref, meta_ref, nfw_ref, ys_ref, out_ref, ybuf_ref, sem):
    s = pl.program_id(0)
    last = pl.num_programs(0) - 1

    def fetch(u, tile):
        def start(q, c):
            _piece_copy(ys_ref, pdst_ref[tile, q], ybuf_ref.at[u], q * ROW_ALIGN, sem.at[u]).start()
            return c
        lax.fori_loop(0, np_ref[tile], start, 0)

    def finish(u, tile):
        _wait_pieces(ys_ref, ybuf_ref.at[u], np_ref[tile], sem.at[u])

        rows = slice(u * TOK_TILE, (u + 1) * TOK_TILE)
        tr = meta_ref[rows, TOP_K:2 * TOP_K].astype(jnp.int32)
        acc = x2_ref[rows, :]
        for c in range(STAGE_ROWS // TOK_TILE):
            cid = lax.broadcasted_iota(jnp.int32, (TOK_TILE, TOK_TILE), 1) + c * TOK_TILE
            hot = [cid == tr[:, k:k + 1] for k in range(TOP_K)]
            sel = jnp.where(hot[0] | hot[1] | hot[2] | hot[3], 1.0, 0.0).astype(BF16)
            acc = acc + _dot(sel, ybuf_ref[u, c * TOK_TILE:(c + 1) * TOK_TILE, :].astype(BF16))
        out_ref[rows, :] = _rms(acc, nfw_ref[...])

    @pl.when(s == 0)
    def _():
        ybuf_ref[...] = jnp.zeros_like(ybuf_ref)
        for u in range(MOE_TILES - 1):
            fetch(u, u)

    for u in range(MOE_TILES):
        tile = MOE_TILES * s + u
        ahead = u + MOE_TILES - 1
        if ahead < MOE_TILES:
            fetch(ahead, MOE_TILES * s + ahead)
        else:
            @pl.when(s < last)
            def _():
                fetch(ahead - MOE_TILES, MOE_TILES * s + ahead)
        finish(u, tile)


def _combine(x2, meta, pdst, n_pieces, ys, norm_final_w):
    n_tok, d = x2.shape
    n_tiles = n_tok // TOK_TILE
    assert n_tiles % MOE_TILES == 0
    smem = pl.BlockSpec(memory_space=pltpu.SMEM)
    return pl.pallas_call(
        _combine_kernel,
        grid=(n_tiles // MOE_TILES,),
        in_specs=[
            smem, smem,
            pl.BlockSpec((MOE_TILES * TOK_TILE, d), lambda s: (s, 0)),
            pl.BlockSpec((MOE_TILES * TOK_TILE, LANES), lambda s: (s, 0)),
            pl.BlockSpec((1, d), lambda s: (0, 0)),
            pl.BlockSpec(memory_space=pl.ANY),
        ],
        out_specs=pl.BlockSpec((MOE_TILES * TOK_TILE, d), lambda s: (s, 0)),
        out_shape=jax.ShapeDtypeStruct((n_tok, d), F32),
        scratch_shapes=[pltpu.VMEM((MOE_TILES, STAGE_ROWS, d), F32), pltpu.SemaphoreType.DMA((MOE_TILES,))],
        compiler_params=pltpu.CompilerParams(
            dimension_semantics=("arbitrary",), vmem_limit_bytes=VMEM_LIMIT_BYTES),
        name="moe_combine",
    )(pdst, n_pieces, x2, meta, norm_final_w, ys)


def _pad_heads(w, n_heads, width):
    lead = w.shape[:-1]
    w = w.reshape(lead + (n_heads, width))
    w = jnp.pad(w, [(0, 0)] * len(lead) + [(0, 0), (0, HEAD_W - width)])
    return w.reshape(lead + (n_heads * HEAD_W,))


def _round_up(x, m):
    return (x + m - 1) // m * m


def _routing_tables(cnt, n_tiles):
    i32 = jnp.int32
    counts = cnt.reshape(n_tiles, 8, LANES)[:, 0, :N_EXPERTS].astype(i32)
    cnt8 = _round_up(counts, ROW_ALIGN)
    lend = jnp.cumsum(cnt8, axis=1)
    lstart = lend - cnt8
    tot = jnp.sum(cnt8, axis=0)
    padded = _round_up(tot, MOE_ROWS)
    pends = jnp.cumsum(padded)
    pstarts = pends - padded
    goff = pstarts[None, :] + jnp.cumsum(cnt8, axis=0) - cnt8

    prow = jnp.arange(MAX_PIECES, dtype=i32) * ROW_ALIGN
    pe = jnp.sum(prow[None, :, None] >= lend[:, None, :], axis=-1)
    pe = jnp.minimum(pe, N_EXPERTS - 1)
    pick = pe[:, :, None] == jnp.arange(N_EXPERTS, dtype=i32)
    pdst = (jnp.sum(jnp.where(pick, (goff - lstart)[:, None, :], 0), axis=-1) + prow[None, :]).astype(i32)
    n_pieces = (lend[:, -1] // ROW_ALIGN).astype(i32)
    pdst = jnp.where(jnp.arange(MAX_PIECES, dtype=i32)[None, :] < n_pieces[:, None], pdst, 0)

    max_rows = n_tiles * (TOK_TILE * TOP_K + N_EXPERTS * (ROW_ALIGN - 1)) + N_EXPERTS * (MOE_ROWS - ROW_ALIGN)
    n_blocks = -(-max_rows // MOE_ROWS)
    n_used = (pends[-1] // MOE_ROWS).astype(i32)
    blk = jnp.arange(n_blocks, dtype=i32)
    block_e = jnp.minimum(jnp.sum(blk[:, None] * MOE_ROWS >= pends[None, :], axis=-1), N_EXPERTS - 1).astype(i32)
    block_e = jnp.where(blk < n_used, block_e, block_e[jnp.maximum(n_used - 1, 0)])
    first = ((blk < n_used) & ((blk == 0) | (block_e != jnp.roll(block_e, 1)))).astype(i32)
    has_rows = padded > 0
    slot_e = (jnp.cumsum(has_rows.astype(i32)) - 1) % 2
    later = jnp.arange(N_EXPERTS, dtype=i32)[None, :] > jnp.arange(N_EXPERTS, dtype=i32)[:, None]
    nxt_e = jnp.min(jnp.where(later & has_rows[None, :], jnp.arange(N_EXPERTS, dtype=i32)[None, :], N_EXPERTS), axis=1)
    nxt_e = jnp.where(nxt_e < N_EXPERTS, nxt_e, -1).astype(i32)
    tail_info = jnp.stack([pstarts + tot, (padded - tot) // ROW_ALIGN]).astype(i32)
    spare = jnp.stack([n_used, n_blocks - n_used]).astype(i32)
    of_block = block_e[:, None] == jnp.arange(N_EXPERTS, dtype=i32)
    slot_b = jnp.sum(jnp.where(of_block, slot_e[None, :], 0), axis=1).astype(i32)
    nxt_b = jnp.sum(jnp.where(of_block, nxt_e[None, :], 0), axis=1).astype(i32)
    return (pdst, n_pieces, tail_info, spare,
            (block_e, n_used.reshape(1), first, slot_b, nxt_b), n_blocks * MOE_ROWS)


def kernel(x, mem, norm_mix_w, w_in, hg_lb_logits, hg_onorm_w, gla_w_gk2, gla_b_gk, gla_onorm_w, w_out, norm_xa_w, norm_mem_w, w_xq, w_xkv, w_xo, norm_moe_w, w_router, b_router, w_gate_up, b_gate_up, w_down, b_down, norm_final_w):
    assert w_in.shape[0] == 1, "single-layer block"
    bsz, seq, d = x.shape
    n_tok = bsz * seq
    gk = GLA_HEADS * GLA_KDIM

    wi = w_in[0]
    w_in_p = jnp.concatenate([
        wi[:, :4 * _W],
        _pad_heads(wi[:, 4 * _W:4 * _W + gk], GLA_HEADS, GLA_KDIM),
        _pad_heads(wi[:, 4 * _W + gk:4 * _W + 2 * gk], GLA_HEADS, GLA_KDIM),
        wi[:, 4 * _W + 2 * gk:4 * _W + 2 * gk + 2 * _W],
        jnp.pad(wi[:, 4 * _W + 2 * gk + 2 * _W:], ((0, 0), (0, LANES - GLA_RANK))),
    ], axis=1).astype(BF16)
    w_gk2_p = jnp.pad(_pad_heads(gla_w_gk2[0], GLA_HEADS, GLA_KDIM),
                      ((0, LANES - GLA_RANK), (0, 0))).astype(BF16)
    b_gk_p = _pad_heads(gla_b_gk, GLA_HEADS, GLA_KDIM)
    wr = jnp.pad(w_router[0], ((0, 0), (0, LANES - N_EXPERTS)))
    wr_hi = wr.astype(BF16)
    wr_lo = (wr - wr_hi.astype(F32)).astype(BF16)
    br_p = jnp.pad(b_router, ((0, 0), (0, LANES - N_EXPERTS)), constant_values=-1e30)

    x1 = _token_mix(x, norm_mix_w, w_in_p, hg_lb_logits, hg_onorm_w, w_gk2_p, b_gk_p, gla_onorm_w,
                    w_out[0].astype(BF16))
    x2, hm, meta, meta_t, cnt = _xattn_router(x1, mem, norm_xa_w, norm_mem_w, w_xq[0].astype(BF16),
                                              w_xkv[0].astype(BF16), w_xo[0].astype(BF16), norm_moe_w,
                                              wr_hi, wr_lo, br_p)

    n_tiles = n_tok // TOK_TILE
    pdst, n_pieces, tail_info, spare, block_tables, n_rows = _routing_tables(cnt, n_tiles)

    xs = _dispatch(hm.reshape(n_tok, d), pdst, n_pieces, tail_info, spare, meta_t, n_rows)
    ys = _expert_mlp(xs, *block_tables, w_gate_up[0], b_gate_up[0][:, None, :],
                     w_down[0], b_down[0][:, None, :])
    out = _combine(x2.reshape(n_tok, d), meta, pdst, n_pieces, ys, norm_final_w[None, :])
    return out.reshape(bsz, seq, d)
```

```python
import jax
import jax.numpy as jnp
from jax import lax
from jax.experimental import pallas as pl
from jax.experimental.pallas import tpu as pltpu

F32 = jnp.float32
BF16 = jnp.bfloat16
EPS = 1e-6
LOG2E = 1.4426950408889634

HEAD_W = 128
HG_HEADS = 4
GLA_HEADS = 4
GLA_KDIM = 64
N_HEADS = HG_HEADS + GLA_HEADS
GLA_RANK = 16
GLA_GATE_NORMALIZER = 16.0
CHUNK = 64
XA_HEADS = 4
N_EXPERTS = 32
TOP_K = 4
SWIGLU_LIMIT = 7.0
SWIGLU_ALPHA = 1.702

LANES = 128
SUBLANES = 8
VMEM_LIMIT_BYTES = 56 * 1024 * 1024

MIX_ROWS = 256
TOK_TILE = 256
MOE_ROWS = 512
ROW_ALIGN = SUBLANES
STAGE_ROWS = 1280
MAX_PIECES = STAGE_ROWS // ROW_ALIGN
MOE_TILES = 4
META_ROWS = 16

_W = HG_HEADS * HEAD_W
COL_HQ, COL_HF, COL_HI, COL_HGATE = 0, _W, 2 * _W, 3 * _W
COL_GQ, COL_GK, COL_GV, COL_GGATE = 4 * _W, 5 * _W, 6 * _W, 7 * _W
COL_GLR = 8 * _W
PROJ_W = COL_GLR + LANES


def _rms(x, w):
    return x * lax.rsqrt(jnp.mean(x * x, axis=-1, keepdims=True) + EPS) * w


def _dot(a, b):
    return jnp.dot(a, b, preferred_element_type=F32)


def _dot_nt(a, b):
    return lax.dot_general(a, b, (((1,), (1,)), ((), ())), preferred_element_type=F32)


def _dot_tn(a, b):
    return lax.dot_general(a, b, (((0,), (0,)), ((), ())), preferred_element_type=F32)


def _head_chunk(hd, q_scr, k_scr, b2_scr, v, st_ref, sc_ref, masks):
    sl = slice(hd * HEAD_W, (hd + 1) * HEAD_W)
    q, k, b2 = q_scr[:, sl], k_scr[:, sl], b2_scr[:, sl]

    for s in range(0, CHUNK, SUBLANES):
        q_blk, b_blk = q[s:s + SUBLANES], b2[s:s + SUBLANES]
        for j in range(s, s + SUBLANES):
            e = jnp.exp2(b_blk - b2_scr[j:j + 1, sl])
            sc_ref[s:s + SUBLANES, j:j + 1] = jnp.sum(q_blk * (k_scr[j:j + 1, sl] * e), axis=-1, keepdims=True)
    scores = jnp.where(masks[0], sc_ref[...], 0.0)

    for lvl, size in enumerate((16, 32, 64)):
        mids = [b2_scr[r:r + 1, sl] for r in range(size // 2 - 1, CHUNK, size)]
        mid = jnp.concatenate([jnp.broadcast_to(m, (size, HEAD_W)) for m in mids], axis=0)
        dist = b2 - mid
        e = jnp.exp2(jnp.minimum(dist, -dist))
        r = _dot_nt((q * e).astype(BF16), (k * e).astype(BF16))
        scores = jnp.where(masks[lvl + 1], r, scores)

    b2_last = b2_scr[CHUNK - 1:CHUNK, sl]
    q_abs = (q * jnp.exp2(b2)).astype(BF16)
    k_end = (k * jnp.exp2(b2_last - b2)).astype(BF16)
    v16 = v.astype(BF16)
    st = st_ref[...]
    o = _dot(scores.astype(BF16), v16) + _dot_nt(q_abs, st.astype(BF16))
    st_ref[...] = st * jnp.exp2(b2_last) + _dot_tn(v16, k_end)
    return o


def _mix_kernel(x_ref, nw_ref, win_ref, lbl_ref, hgw_ref, wgk_ref, bgk_ref, glw_ref, wout_ref,
                o_ref, proj_ref, oall_ref, st_ref, q_scr, k_scr, b2_scr, sc_scr):
    @pl.when(pl.program_id(1) == 0)
    def _():
        st_ref[...] = jnp.zeros_like(st_ref)
        sc_scr[...] = jnp.zeros_like(sc_scr)

    x = x_ref[0]
    h = _rms(x, nw_ref[...]).astype(BF16)
    proj_ref[...] = _dot(h, win_ref[...])

    lbl = lbl_ref[...]
    e = jnp.exp(lbl - jnp.max(lbl, axis=0, keepdims=True))
    lb = e[0:1] / jnp.sum(e, axis=0, keepdims=True)

    ri = lax.broadcasted_iota(jnp.int32, (CHUNK, CHUNK), 0)
    ci = lax.broadcasted_iota(jnp.int32, (CHUNK, CHUNK), 1)
    tri = jnp.where(ri >= ci, 1.0, 0.0).astype(BF16)
    masks = [(ri // SUBLANES == ci // SUBLANES) & (ri >= ci)]
    for size in (16, 32, 64):
        masks.append((ri // size == ci // size) & (ri // (size // 2) > ci // (size // 2)))

    def chunk(ck):
        rows = pl.ds(ck * CHUNK, CHUNK)
        f = lb + (1.0 - lb) * jax.nn.sigmoid(proj_ref[rows, COL_HF:COL_HF + _W])
        g_hg = jnp.log(f)
        z = _dot(proj_ref[rows, COL_GLR:COL_GLR + LANES].astype(BF16), wgk_ref[...]) + bgk_ref[...]
        g_gla = (jnp.minimum(z, 0.0) - jnp.log1p(jnp.exp(-jnp.abs(z)))) * (1.0 / GLA_GATE_NORMALIZER)
        g_all = jnp.concatenate([g_hg, g_gla], axis=-1)
        g_hi = g_all.astype(BF16)
        g_lo = (g_all - g_hi.astype(F32)).astype(BF16)
        b2_scr[...] = (_dot(tri, g_hi) + _dot(tri, g_lo)) * LOG2E
        hq = proj_ref[rows, COL_HQ:COL_HQ + _W]
        q_scr[:, 0:_W] = hq * jax.nn.sigmoid(hq)
        q_scr[:, _W:2 * _W] = proj_ref[rows, COL_GQ:COL_GQ + _W] * (GLA_KDIM ** -0.5)
        k_scr[:, 0:_W] = 1.0 - f
        k_scr[:, _W:2 * _W] = proj_ref[rows, COL_GK:COL_GK + _W]

        for hd in range(N_HEADS):
            lo = hd * HEAD_W
            col_v = COL_HI + lo if hd < HG_HEADS else COL_GV + lo - _W
            o = _head_chunk(hd, q_scr, k_scr, b2_scr, proj_ref[rows, col_v:col_v + HEAD_W],
                            st_ref.at[hd], sc_scr.at[hd], masks)
            oall_ref[rows, lo:lo + HEAD_W] = o

    for ck in range(MIX_ROWS // CHUNK):
        chunk(ck)

    ys = []
    for hd in range(N_HEADS):
        lo = hd * HEAD_W
        o = oall_ref[:, lo:lo + HEAD_W]
        if hd < HG_HEADS:
            w, gate = hgw_ref[...], proj_ref[:, COL_HGATE + lo:COL_HGATE + lo + HEAD_W]
        else:
            w, gate = glw_ref[...], proj_ref[:, COL_GGATE + lo - _W:COL_GGATE + lo - _W + HEAD_W]
        ys.append((_rms(o, w) * (gate * jax.nn.sigmoid(gate))).astype(BF16))
    y = jnp.concatenate(ys, axis=-1)
    o_ref[0] = x + _dot(y, wout_ref[...])


def _token_mix(x, norm_w, w_in_p, lb_logits, hg_onorm_w, w_gk2_p, b_gk_p, gla_onorm_w, w_out):
    bsz, seq, d = x.shape
    const = lambda shape: pl.BlockSpec(shape, lambda b, t: (0,) * len(shape))
    return pl.pallas_call(
        _mix_kernel,
        grid=(bsz, seq // MIX_ROWS),
        in_specs=[
            pl.BlockSpec((1, MIX_ROWS, d), lambda b, t: (b, t, 0)),
            const((1, d)),
            const((d, PROJ_W)),
            const(lb_logits.shape),
            const((1, HEAD_W)),
            const((LANES, _W)),
            const((1, _W)),
            const((1, HEAD_W)),
            const((2 * _W, d)),
        ],
        out_specs=pl.BlockSpec((1, MIX_ROWS, d), lambda b, t: (b, t, 0)),
        out_shape=jax.ShapeDtypeStruct((bsz, seq, d), F32),
        scratch_shapes=[
            pltpu.VMEM((MIX_ROWS, PROJ_W), F32),
            pltpu.VMEM((MIX_ROWS, N_HEADS * HEAD_W), F32),
            pltpu.VMEM((N_HEADS, HEAD_W, HEAD_W), F32),
            pltpu.VMEM((CHUNK, N_HEADS * HEAD_W), F32),
            pltpu.VMEM((CHUNK, N_HEADS * HEAD_W), F32),
            pltpu.VMEM((CHUNK, N_HEADS * HEAD_W), F32),
            pltpu.VMEM((N_HEADS, CHUNK, CHUNK), F32),
        ],
        compiler_params=pltpu.CompilerParams(
            dimension_semantics=("arbitrary", "arbitrary"), vmem_limit_bytes=VMEM_LIMIT_BYTES),
        name="token_mix",
    )(x, norm_w, w_in_p, lb_logits, hg_onorm_w, w_gk2_p, b_gk_p, gla_onorm_w, w_out)


def _xattn_tile(x, k_scr, v_scr, nxw_ref, wq_ref, wo_ref, nmoe_ref, wr_hi_ref, wr_lo_ref, br_ref):
    d = x.shape[-1]
    hdim = d // XA_HEADS
    q = _dot(_rms(x, nxw_ref[...]).astype(BF16), wq_ref[...])
    outs = []
    for h in range(XA_HEADS):
        sl = slice(h * hdim, (h + 1) * hdim)
        s = _dot_nt(q[:, sl].astype(BF16), k_scr[:, sl]) * (hdim ** -0.5)
        p = jnp.exp(s - jnp.max(s, axis=-1, keepdims=True))
        p = p / jnp.sum(p, axis=-1, keepdims=True)
        outs.append(_dot(p.astype(BF16), v_scr[:, sl]).astype(BF16))
    x2 = x + _dot(jnp.concatenate(outs, axis=-1), wo_ref[...])

    hm = _rms(x2, nmoe_ref[...])
    hm_hi = hm.astype(BF16)

    hm_lo = (hm - hm_hi.astype(F32)).astype(BF16)
    logits = (_dot(hm_hi, wr_hi_ref[...]) + _dot(hm_lo, wr_hi_ref[...]) + _dot(hm_hi, wr_lo_ref[...])
              + br_ref[...])

    rows = logits.shape[0]
    lane = lax.broadcasted_iota(jnp.int32, (rows, LANES), 1)
    lane_f = lane.astype(F32)
    neg_inf = jnp.float32(-jnp.inf)
    top_v, top_i, hots = [], [], []
    work = logits
    for _ in range(TOP_K):
        m = jnp.max(work, axis=-1, keepdims=True)
        idx = jnp.min(jnp.where(work == m, lane_f, float(LANES)), axis=-1, keepdims=True)
        hot = lane_f == idx
        work = jnp.where(hot, neg_inf, work)
        top_v.append(m)
        top_i.append(idx)
        hots.append(hot)
    es = [jnp.exp(v - top_v[0]) for v in top_v]
    denom = es[0] + es[1] + es[2] + es[3]
    gates = [e / denom for e in es]

    chosen = jnp.where(hots[0] | hots[1] | hots[2] | hots[3], 1.0, 0.0)
    ri = lax.broadcasted_iota(jnp.int32, (rows, rows), 0)
    ci = lax.broadcasted_iota(jnp.int32, (rows, rows), 1)
    strict = jnp.where(ri > ci, 1.0, 0.0).astype(BF16)
    before = _dot(strict, chosen.astype(BF16))
    counts = jnp.sum(chosen, axis=0, keepdims=True)
    padded = jnp.floor((counts + (ROW_ALIGN - 1.0)) * (1.0 / ROW_ALIGN)) * ROW_ALIGN
    li = lax.broadcasted_iota(jnp.int32, (LANES, LANES), 0)
    lj = lax.broadcasted_iota(jnp.int32, (LANES, LANES), 1)
    group_start = _dot(jnp.broadcast_to(padded, (SUBLANES, LANES)).astype(BF16),
                       jnp.where(li < lj, 1.0, 0.0).astype(BF16))[0:1]
    slot_of = before + group_start
    slots = [jnp.sum(jnp.where(hot, slot_of, 0.0), axis=-1, keepdims=True) for hot in hots]

    meta = jnp.zeros((rows, LANES), F32)
    for j, col in enumerate(top_i + slots + gates):
        meta = jnp.where(lane == j, col, meta)
    return x2, hm_hi, meta, counts


def _xattn_kernel(x_ref, mem_ref, nxw_ref, nmw_ref, wq_ref, wkv_ref, wo_ref, nmoe_ref,
                  wr_hi_ref, wr_lo_ref, br_ref,
                  x2_ref, hm_ref, meta_ref, meta_t_ref, cnt_ref, k_scr, v_scr):
    d = x_ref.shape[-1]

    @pl.when(pl.program_id(1) == 0)
    def _():
        m = _rms(mem_ref[0], nmw_ref[...]).astype(BF16)
        kv = _dot(m, wkv_ref[...])
        k_scr[...] = kv[:, :d].astype(BF16)
        v_scr[...] = kv[:, d:].astype(BF16)

    for u in range(2):
        rows = slice(u * TOK_TILE, (u + 1) * TOK_TILE)
        x2, hm16, meta, counts = _xattn_tile(x_ref[0, rows, :], k_scr, v_scr, nxw_ref, wq_ref, wo_ref, nmoe_ref,
                                             wr_hi_ref, wr_lo_ref, br_ref)
        x2_ref[0, rows, :] = x2
        hm_ref[0, rows, :] = hm16
        meta_ref[rows, :] = meta
        meta_t_ref[u] = meta.T[0:META_ROWS, :]
        cnt_ref[u * SUBLANES:(u + 1) * SUBLANES, :] = jnp.broadcast_to(counts, (SUBLANES, LANES))


def _xattn_router(x1, mem, norm_xa_w, norm_mem_w, w_xq, w_xkv, w_xo, norm_moe_w, wr_hi, wr_lo, br_p):
    bsz, seq, d = x1.shape
    mlen = mem.shape[1]
    n_t = seq // (2 * TOK_TILE)
    const = lambda shape: pl.BlockSpec(shape, lambda b, t: (0,) * len(shape))
    return pl.pallas_call(
        _xattn_kernel,
        grid=(bsz, n_t),
        in_specs=[
            pl.BlockSpec((1, 2 * TOK_TILE, d), lambda b, t: (b, t, 0)),
            pl.BlockSpec((1, mlen, d), lambda b, t: (b, 0, 0)),
            const((1, d)), const((1, d)),
            const((d, d)), const((d, 2 * d)), const((d, d)),
            const((1, d)),
            const((d, LANES)), const((d, LANES)), const((1, LANES)),
        ],
        out_specs=[
            pl.BlockSpec((1, 2 * TOK_TILE, d), lambda b, t: (b, t, 0)),
            pl.BlockSpec((1, 2 * TOK_TILE, d), lambda b, t: (b, t, 0)),
            pl.BlockSpec((2 * TOK_TILE, LANES), lambda b, t: (b * n_t + t, 0)),
            pl.BlockSpec((2, META_ROWS, TOK_TILE), lambda b, t: (b * n_t + t, 0, 0)),
            pl.BlockSpec((2 * SUBLANES, LANES), lambda b, t: (b * n_t + t, 0)),
        ],
        out_shape=[
            jax.ShapeDtypeStruct((bsz, seq, d), F32),
            jax.ShapeDtypeStruct((bsz, seq, d), BF16),
            jax.ShapeDtypeStruct((bsz * seq, LANES), F32),
            jax.ShapeDtypeStruct((bsz * seq // TOK_TILE, META_ROWS, TOK_TILE), F32),
            jax.ShapeDtypeStruct((bsz * seq // TOK_TILE * SUBLANES, LANES), F32),
        ],
        scratch_shapes=[
            pltpu.VMEM((mlen, d), BF16),
            pltpu.VMEM((mlen, d), BF16),
        ],
        compiler_params=pltpu.CompilerParams(
            dimension_semantics=("arbitrary", "arbitrary"), vmem_limit_bytes=VMEM_LIMIT_BYTES),
        name="xattn_router",
    )(x1, mem, norm_xa_w, norm_mem_w, w_xq, w_xkv, w_xo, norm_moe_w, wr_hi, wr_lo, br_p)


def _piece_copy(src_ref, src_row, dst_ref, dst_row, sem):
    return pltpu.make_async_copy(src_ref.at[pl.ds(pl.multiple_of(src_row, ROW_ALIGN), ROW_ALIGN)],
                                 dst_ref.at[pl.ds(pl.multiple_of(dst_row, ROW_ALIGN), ROW_ALIGN)], sem)


def _group_copies(grp_ref, tile, make_copy):
    def per_expert(e, carry):
        src = grp_ref[tile, 0, e]
        dst = grp_ref[tile, 1, e]
        n = grp_ref[tile, 2, e]
        size = TOK_TILE // ROW_ALIGN
        queue = 0
        while size >= 1:
            @pl.when((n & size) != 0)
            def _(src=src, dst=dst, size=size, queue=queue):
                make_copy(pl.multiple_of(src, ROW_ALIGN), pl.multiple_of(dst, ROW_ALIGN),
                          size * ROW_ALIGN).start(priority=queue)
            step = (n & size) * ROW_ALIGN
            src, dst = src + step, dst + step
            size //= 2
            queue = 1 - queue
        return carry
    lax.fori_loop(0, N_EXPERTS, per_expert, 0)


def _wait_pieces(src_ref, dst_ref, n, sem):
    size = MAX_PIECES
    while size & (size - 1):
        size &= size - 1
    while size >= 1:
        @pl.when((n & size) != 0)
        def _(size=size):
            pltpu.make_async_copy(src_ref.at[pl.ds(0, size * ROW_ALIGN)], dst_ref.at[pl.ds(0, size * ROW_ALIGN)], sem).wait()
        size //= 2


def _block_copy(src_ref, dst_ref, dst_block, sem):
    return pltpu.make_async_copy(src_ref, dst_ref.at[pl.ds(pl.multiple_of(dst_block * MOE_ROWS, MOE_ROWS), MOE_ROWS)], sem)


def _dispatch_kernel(grp_ref, np_ref, tail_ref, cap_ref, mt_ref, hm_ref, xs_ref, stage_ref, zero_ref, sem, sem_z):
    s = pl.program_id(0)
    last = pl.num_programs(0) - 1
    d = hm_ref.shape[1]

    def spare_blocks(fn):
        def body(i, c):
            fn(_block_copy(zero_ref, xs_ref, cap_ref[0] + i, sem_z))
            return c
        lax.fori_loop(0, cap_ref[1], body, 0)

    @pl.when(s == 0)
    def _():
        zero_ref[...] = jnp.zeros_like(zero_ref)
        spare_blocks(lambda cp: cp.start())

        def per_expert(e, carry):
            n = tail_ref[1, e]
            off = tail_ref[0, e]

            def start(i, c):
                _piece_copy(zero_ref, 0, xs_ref, off + i * ROW_ALIGN, sem.at[0]).start()
                return c

            def wait(i, c):
                _piece_copy(zero_ref, 0, xs_ref, 0, sem.at[0]).wait()
                return c

            lax.fori_loop(0, n, start, 0)
            lax.fori_loop(0, n, wait, 0)
            return carry

        lax.fori_loop(0, N_EXPERTS, per_expert, 0)

    def drain(u, tile):
        _wait_pieces(stage_ref.at[u], xs_ref, np_ref[tile], sem.at[u])

    for u in range(MOE_TILES):
        tile = MOE_TILES * s + u

        @pl.when(s > 0)
        def _():
            drain(u, tile - MOE_TILES)

        x16 = hm_ref[u * TOK_TILE:(u + 1) * TOK_TILE, :]
        tr = mt_ref[u, TOP_K:2 * TOP_K, :].astype(jnp.int32)
        g = mt_ref[u, 2 * TOP_K:3 * TOP_K, :]
        for c in range(STAGE_ROWS // TOK_TILE):
            rows = slice(c * TOK_TILE, (c + 1) * TOK_TILE)
            rid = lax.broadcasted_iota(jnp.int32, (TOK_TILE, TOK_TILE), 0) + c * TOK_TILE
            hot = [rid == tr[k:k + 1, :] for k in range(TOP_K)]
            sel = jnp.where(hot[0] | hot[1] | hot[2] | hot[3], 1.0, 0.0).astype(BF16)
            stage_ref[u, rows, 0:d] = _dot(sel, x16)
            gsum = jnp.zeros((TOK_TILE, 1), F32)
            for k in range(TOP_K):
                gsum = gsum + jnp.sum(jnp.where(hot[k], g[k:k + 1, :], 0.0), axis=-1, keepdims=True)
            stage_ref[u, rows, d:d + LANES] = jnp.broadcast_to(gsum, (TOK_TILE, LANES))

        _group_copies(grp_ref, tile, lambda src, dst, rows: pltpu.make_async_copy(
            stage_ref.at[u, pl.ds(src, rows)], xs_ref.at[pl.ds(dst, rows)], sem.at[u]))

    @pl.when(s == last)
    def _():
        for u in range(MOE_TILES):
            drain(u, MOE_TILES * s + u)
        spare_blocks(lambda cp: cp.wait())


def _dispatch(hm, groups, n_pieces, tail_info, spare, meta_t, n_rows):
    n_tok, d = hm.shape
    n_tiles = n_tok // TOK_TILE
    assert n_tiles % MOE_TILES == 0
    smem = pl.BlockSpec(memory_space=pltpu.SMEM)
    return pl.pallas_call(
        _dispatch_kernel,
        grid=(n_tiles // MOE_TILES,),
        in_specs=[
            smem, smem, smem, smem,
            pl.BlockSpec((MOE_TILES, META_ROWS, TOK_TILE), lambda s: (s, 0, 0)),
            pl.BlockSpec((MOE_TILES * TOK_TILE, d), lambda s: (s, 0)),
        ],
        out_specs=pl.BlockSpec(memory_space=pl.ANY),
        out_shape=jax.ShapeDtypeStruct((n_rows, d + LANES), F32),
        scratch_shapes=[pltpu.VMEM((MOE_TILES, STAGE_ROWS, d + LANES), F32), pltpu.VMEM((MOE_ROWS, d + LANES), F32),
                        pltpu.SemaphoreType.DMA((MOE_TILES,)), pltpu.SemaphoreType.DMA],
        compiler_params=pltpu.CompilerParams(
            dimension_semantics=("arbitrary",), vmem_limit_bytes=VMEM_LIMIT_BYTES),
        name="moe_dispatch",
    )(groups, n_pieces, tail_info, spare, meta_t, hm)


def _expert_kernel(be_ref, nu_ref, first_ref, slot_ref, nxt_ref, xs_ref, wgu_hbm, bgu_ref, wd_hbm,
                   bd_ref, ys_ref, wgu32_ref, wd32_ref, wgu16_ref, wd16_ref, sem):
    i = pl.program_id(0)
    f, d = wd16_ref.shape

    def weight_copies(e, slot):
        return (pltpu.make_async_copy(wgu_hbm.at[e], wgu32_ref.at[slot], sem.at[0, slot]),
                pltpu.make_async_copy(wd_hbm.at[e], wd32_ref.at[slot], sem.at[1, slot]))

    @pl.when(i == 0)
    def _():
        for cp in weight_copies(be_ref[0], 0):
            cp.start()

    @pl.when(first_ref[i] == 1)
    def _():
        slot = slot_ref[i]
        for cp in weight_copies(be_ref[i], slot):
            cp.wait()

        @pl.when(nxt_ref[i] >= 0)
        def _():
            for cp in weight_copies(nxt_ref[i], 1 - slot):
                cp.start()

        wgu16_ref[...] = wgu32_ref[slot].astype(BF16)
        wd16_ref[...] = wd32_ref[slot].astype(BF16)

    @pl.when(i < nu_ref[0])
    def _():
        gu = _dot(xs_ref[:, 0:d].astype(BF16), wgu16_ref[...]) + bgu_ref[0]
        gate = jnp.minimum(gu[:, :f], SWIGLU_LIMIT)
        up = jnp.clip(gu[:, f:], -SWIGLU_LIMIT, SWIGLU_LIMIT)
        act = (up + 1.0) * gate * jax.nn.sigmoid(SWIGLU_ALPHA * gate)
        ys_ref[...] = (_dot(act.astype(BF16), wd16_ref[...]) + bd_ref[0]) * xs_ref[:, d:d + 1]

    @pl.when(i >= nu_ref[0])
    def _():
        ys_ref[...] = jnp.zeros_like(ys_ref)


def _expert_mlp(xs, block_e, n_used, first, slot, nxt, w_gu, b_gu, w_d, b_d):
    n_rows = xs.shape[0]
    d = w_d.shape[2]
    n_blocks = n_rows // MOE_ROWS
    f = w_d.shape[1]
    row_map = lambda i, be, nu, *_: (jnp.minimum(i, nu[0] - 1), 0)
    exp_map = lambda i, be, *_: (be[i], 0, 0)
    return pl.pallas_call(
        _expert_kernel,
        grid_spec=pltpu.PrefetchScalarGridSpec(
            num_scalar_prefetch=5,
            grid=(n_blocks,),
            in_specs=[
                pl.BlockSpec((MOE_ROWS, d + LANES), row_map),
                pl.BlockSpec(memory_space=pl.ANY),
                pl.BlockSpec((1, 1, 2 * f), exp_map),
                pl.BlockSpec(memory_space=pl.ANY),
                pl.BlockSpec((1, 1, d), exp_map),
            ],
            out_specs=pl.BlockSpec((MOE_ROWS, d), lambda i, *_: (i, 0)),
            scratch_shapes=[
                pltpu.VMEM((2, d, 2 * f), F32), pltpu.VMEM((2, f, d), F32),
                pltpu.VMEM((d, 2 * f), BF16), pltpu.VMEM((f, d), BF16),
                pltpu.SemaphoreType.DMA((2, 2)),
            ],
        ),
        out_shape=jax.ShapeDtypeStruct((n_rows, d), F32),
        compiler_params=pltpu.CompilerParams(
            dimension_semantics=("arbitrary",), vmem_limit_bytes=VMEM_LIMIT_BYTES),
        name="moe_experts",
    )(block_e, n_used, first, slot, nxt, xs, w_gu, b_gu, w_d, b_d)


def _combine_kernel(grp_ref, np_ref, x2_ref, meta_ref, nfw_ref, ys_ref, out_ref, ybuf_ref, sem):
    s = pl.program_id(0)
    last = pl.num_programs(0) - 1

    def fetch(u, tile):
        _group_copies(grp_ref, tile, lambda src, dst, rows: pltpu.make_async_copy(
            ys_ref.at[pl.ds(dst, rows)], ybuf_ref.at[u, pl.ds(src, rows)], sem.at[u]))

    def finish(u, tile):
        _wait_pieces(ys_ref, ybuf_ref.at[u], np_ref[tile], sem.at[u])

        rows = slice(u * TOK_TILE, (u + 1) * TOK_TILE)
        tr = meta_ref[rows, TOP_K:2 * TOP_K].astype(jnp.int32)
        acc = x2_ref[rows, :]
        for c in range(STAGE_ROWS // TOK_TILE):
            cid = lax.broadcasted_iota(jnp.int32, (TOK_TILE, TOK_TILE), 1) + c * TOK_TILE
            hot = [cid == tr[:, k:k + 1] for k in range(TOP_K)]
            sel = jnp.where(hot[0] | hot[1] | hot[2] | hot[3], 1.0, 0.0).astype(BF16)
            acc = acc + _dot(sel, ybuf_ref[u, c * TOK_TILE:(c + 1) * TOK_TILE, :].astype(BF16))
        out_ref[rows, :] = _rms(acc, nfw_ref[...])

    @pl.when(s == 0)
    def _():
        ybuf_ref[...] = jnp.zeros_like(ybuf_ref)
        for u in range(MOE_TILES - 1):
            fetch(u, u)

    for u in range(MOE_TILES):
        tile = MOE_TILES * s + u
        ahead = u + MOE_TILES - 1
        if ahead < MOE_TILES:
            fetch(ahead, MOE_TILES * s + ahead)
        else:
            @pl.when(s < last)
            def _():
                fetch(ahead - MOE_TILES, MOE_TILES * s + ahead)
        finish(u, tile)


def _combine(x2, meta, groups, n_pieces, ys, norm_final_w):
    n_tok, d = x2.shape
    n_tiles = n_tok // TOK_TILE
    assert n_tiles % MOE_TILES == 0
    smem = pl.BlockSpec(memory_space=pltpu.SMEM)
    return pl.pallas_call(
        _combine_kernel,
        grid=(n_tiles // MOE_TILES,),
        in_specs=[
            smem, smem,
            pl.BlockSpec((MOE_TILES * TOK_TILE, d), lambda s: (s, 0)),
            pl.BlockSpec((MOE_TILES * TOK_TILE, LANES), lambda s: (s, 0)),
            pl.BlockSpec((1, d), lambda s: (0, 0)),
            pl.BlockSpec(memory_space=pl.ANY),
        ],
        out_specs=pl.BlockSpec((MOE_TILES * TOK_TILE, d), lambda s: (s, 0)),
        out_shape=jax.ShapeDtypeStruct((n_tok, d), F32),
        scratch_shapes=[pltpu.VMEM((MOE_TILES, STAGE_ROWS, d), F32), pltpu.SemaphoreType.DMA((MOE_TILES,))],
        compiler_params=pltpu.CompilerParams(
            dimension_semantics=("arbitrary",), vmem_limit_bytes=VMEM_LIMIT_BYTES),
        name="moe_combine",
    )(groups, n_pieces, x2, meta, norm_final_w, ys)


def _pad_heads(w, n_heads, width):
    lead = w.shape[:-1]
    w = w.reshape(lead + (n_heads, width))
    w = jnp.pad(w, [(0, 0)] * len(lead) + [(0, 0), (0, HEAD_W - width)])
    return w.reshape(lead + (n_heads * HEAD_W,))


def _round_up(x, m):
    return (x + m - 1) // m * m


def _routing_tables(cnt, n_tiles):
    i32 = jnp.int32
    counts = cnt.reshape(n_tiles, SUBLANES, LANES)[:, 0, :N_EXPERTS].astype(i32)
    cnt8 = _round_up(counts, ROW_ALIGN)
    lend = jnp.cumsum(cnt8, axis=1)
    lstart = lend - cnt8
    tot = jnp.sum(cnt8, axis=0)
    padded = _round_up(tot, MOE_ROWS)
    pends = jnp.cumsum(padded)
    pstarts = pends - padded
    goff = pstarts[None, :] + jnp.cumsum(cnt8, axis=0) - cnt8

    groups = jnp.stack([lstart, goff, cnt8 // ROW_ALIGN], axis=1).astype(i32)
    n_pieces = (lend[:, -1] // ROW_ALIGN).astype(i32)

    max_rows = n_tiles * (TOK_TILE * TOP_K + N_EXPERTS * (ROW_ALIGN - 1)) + N_EXPERTS * (MOE_ROWS - ROW_ALIGN)
    n_blocks = -(-max_rows // MOE_ROWS)
    n_used = (pends[-1] // MOE_ROWS).astype(i32)
    blk = jnp.arange(n_blocks, dtype=i32)
    block_e = jnp.minimum(jnp.sum(blk[:, None] * MOE_ROWS >= pends[None, :], axis=-1), N_EXPERTS - 1).astype(i32)
    block_e = jnp.where(blk < n_used, block_e, block_e[jnp.maximum(n_used - 1, 0)])
    first = ((blk < n_used) & ((blk == 0) | (block_e != jnp.roll(block_e, 1)))).astype(i32)
    has_rows = padded > 0
    slot_e = (jnp.cumsum(has_rows.astype(i32)) - 1) % 2
    later = jnp.arange(N_EXPERTS, dtype=i32)[None, :] > jnp.arange(N_EXPERTS, dtype=i32)[:, None]
    nxt_e = jnp.min(jnp.where(later & has_rows[None, :], jnp.arange(N_EXPERTS, dtype=i32)[None, :], N_EXPERTS), axis=1)
    nxt_e = jnp.where(nxt_e < N_EXPERTS, nxt_e, -1).astype(i32)
    tail_info = jnp.stack([pstarts + tot, (padded - tot) // ROW_ALIGN]).astype(i32)
    spare = jnp.stack([n_used, n_blocks - n_used]).astype(i32)
    of_block = block_e[:, None] == jnp.arange(N_EXPERTS, dtype=i32)
    slot_b = jnp.sum(jnp.where(of_block, slot_e[None, :], 0), axis=1).astype(i32)
    nxt_b = jnp.sum(jnp.where(of_block, nxt_e[None, :], 0), axis=1).astype(i32)
    return (groups, n_pieces, tail_info, spare,
            (block_e, n_used.reshape(1), first, slot_b, nxt_b), n_blocks * MOE_ROWS)


def kernel(x, mem, norm_mix_w, w_in, hg_lb_logits, hg_onorm_w, gla_w_gk2, gla_b_gk, gla_onorm_w, w_out, norm_xa_w, norm_mem_w, w_xq, w_xkv, w_xo, norm_moe_w, w_router, b_router, w_gate_up, b_gate_up, w_down, b_down, norm_final_w):
    assert w_in.shape[0] == 1, "single-layer block"
    bsz, seq, d = x.shape
    n_tok = bsz * seq
    gk = GLA_HEADS * GLA_KDIM

    wi = w_in[0]
    w_in_p = jnp.concatenate([
        wi[:, :4 * _W],
        _pad_heads(wi[:, 4 * _W:4 * _W + gk], GLA_HEADS, GLA_KDIM),
        _pad_heads(wi[:, 4 * _W + gk:4 * _W + 2 * gk], GLA_HEADS, GLA_KDIM),
        wi[:, 4 * _W + 2 * gk:4 * _W + 2 * gk + 2 * _W],
        jnp.pad(wi[:, 4 * _W + 2 * gk + 2 * _W:], ((0, 0), (0, LANES - GLA_RANK))),
    ], axis=1).astype(BF16)
    w_gk2_p = jnp.pad(_pad_heads(gla_w_gk2[0], GLA_HEADS, GLA_KDIM),
                      ((0, LANES - GLA_RANK), (0, 0))).astype(BF16)
    b_gk_p = _pad_heads(gla_b_gk, GLA_HEADS, GLA_KDIM)
    wr = jnp.pad(w_router[0], ((0, 0), (0, LANES - N_EXPERTS)))
    wr_hi = wr.astype(BF16)
    wr_lo = (wr - wr_hi.astype(F32)).astype(BF16)
    br_p = jnp.pad(b_router, ((0, 0), (0, LANES - N_EXPERTS)), constant_values=-1e30)

    x1 = _token_mix(x, norm_mix_w, w_in_p, hg_lb_logits, hg_onorm_w, w_gk2_p, b_gk_p, gla_onorm_w,
                    w_out[0].astype(BF16))
    x2, hm, meta, meta_t, cnt = _xattn_router(x1, mem, norm_xa_w, norm_mem_w, w_xq[0].astype(BF16),
                                              w_xkv[0].astype(BF16), w_xo[0].astype(BF16), norm_moe_w,
                                              wr_hi, wr_lo, br_p)

    n_tiles = n_tok // TOK_TILE
    groups, n_pieces, tail_info, spare, block_tables, n_rows = _routing_tables(cnt, n_tiles)

    xs = _dispatch(hm.reshape(n_tok, d), groups, n_pieces, tail_info, spare, meta_t, n_rows)
    ys = _expert_mlp(xs, *block_tables, w_gate_up[0], b_gate_up[0][:, None, :],
                     w_down[0], b_down[0][:, None, :])
    out = _combine(x2.reshape(n_tok, d), meta, groups, n_pieces, ys, norm_final_w[None, :])
    return out.reshape(bsz, seq, d)
```

```python
import jax
import jax.numpy as jnp
from jax import lax
from jax.experimental import pallas as pl
from jax.experimental.pallas import tpu as pltpu

F32 = jnp.float32
BF16 = jnp.bfloat16
EPS = 1e-6
LOG2E = 1.4426950408889634

HEAD_W = 128
HG_HEADS = 4
GLA_HEADS = 4
GLA_KDIM = 64
N_HEADS = HG_HEADS + GLA_HEADS
GLA_RANK = 16
GLA_GATE_NORMALIZER = 16.0
CHUNK = 64
XA_HEADS = 4
N_EXPERTS = 32
TOP_K = 4
SWIGLU_LIMIT = 7.0
SWIGLU_ALPHA = 1.702

LANES = 128
SUBLANES = 8
VMEM_PER_CORE_BYTES = 64 * 1024 * 1024
VMEM_LIMIT_BYTES = VMEM_PER_CORE_BYTES * 7 // 8

MIX_ROWS = 256
TOK_TILE = 256
MOE_ROWS = 512
ROW_ALIGN = SUBLANES
STAGE_ROWS = 1280
MAX_PIECES = STAGE_ROWS // ROW_ALIGN
MOE_TILES = 4
META_ROWS = 16

_W = HG_HEADS * HEAD_W
COL_HQ, COL_HF, COL_HI, COL_HGATE = 0, _W, 2 * _W, 3 * _W
COL_GQ, COL_GK, COL_GV, COL_GGATE = 4 * _W, 5 * _W, 6 * _W, 7 * _W
COL_GLR = 8 * _W
PROJ_W = COL_GLR + LANES


def _rms(x, w):
    return x * lax.rsqrt(jnp.mean(x * x, axis=-1, keepdims=True) + EPS) * w


def _dot(a, b):
    return jnp.dot(a, b, preferred_element_type=F32)


def _dot_nt(a, b):
    return lax.dot_general(a, b, (((1,), (1,)), ((), ())), preferred_element_type=F32)


def _dot_tn(a, b):
    return lax.dot_general(a, b, (((0,), (0,)), ((), ())), preferred_element_type=F32)


def _head_chunk(hd, q_scr, k_scr, b2_scr, v, st_ref, sc_ref, masks):
    sl = slice(hd * HEAD_W, (hd + 1) * HEAD_W)
    q, k, b2 = q_scr[:, sl], k_scr[:, sl], b2_scr[:, sl]

    for s in range(0, CHUNK, SUBLANES):
        q_blk, b_blk = q[s:s + SUBLANES], b2[s:s + SUBLANES]
        for j in range(s, s + SUBLANES):
            e = jnp.exp2(b_blk - b2_scr[j:j + 1, sl])
            sc_ref[s:s + SUBLANES, j:j + 1] = jnp.sum(q_blk * (k_scr[j:j + 1, sl] * e), axis=-1, keepdims=True)
    scores = jnp.where(masks[0], sc_ref[...], 0.0)

    for lvl, size in enumerate((16, 32, 64)):
        mids = [b2_scr[r:r + 1, sl] for r in range(size // 2 - 1, CHUNK, size)]
        mid = jnp.concatenate([jnp.broadcast_to(m, (size, HEAD_W)) for m in mids], axis=0)
        dist = b2 - mid
        e = jnp.exp2(jnp.minimum(dist, -dist))
        r = _dot_nt((q * e).astype(BF16), (k * e).astype(BF16))
        scores = jnp.where(masks[lvl + 1], r, scores)

    b2_last = b2_scr[CHUNK - 1:CHUNK, sl]
    q_abs = (q * jnp.exp2(b2)).astype(BF16)
    k_end = (k * jnp.exp2(b2_last - b2)).astype(BF16)
    v16 = v.astype(BF16)
    st = st_ref[...]
    o = _dot(scores.astype(BF16), v16) + _dot_nt(q_abs, st.astype(BF16))
    st_ref[...] = st * jnp.exp2(b2_last) + _dot_tn(v16, k_end)
    return o


def _mix_kernel(x_ref, nw_ref, win_ref, lbl_ref, hgw_ref, wgk_ref, bgk_ref, glw_ref, wout_ref,
                o_ref, proj_ref, oall_ref, st_ref, q_scr, k_scr, b2_scr, sc_scr):
    @pl.when(pl.program_id(1) == 0)
    def _():
        st_ref[...] = jnp.zeros_like(st_ref)
        sc_scr[...] = jnp.zeros_like(sc_scr)

    x = x_ref[0]
    h = _rms(x, nw_ref[...]).astype(BF16)
    proj_ref[...] = _dot(h, win_ref[...])

    lbl = lbl_ref[...]
    e = jnp.exp(lbl - jnp.max(lbl, axis=0, keepdims=True))
    lb = e[0:1] / jnp.sum(e, axis=0, keepdims=True)

    ri = lax.broadcasted_iota(jnp.int32, (CHUNK, CHUNK), 0)
    ci = lax.broadcasted_iota(jnp.int32, (CHUNK, CHUNK), 1)
    tri = jnp.where(ri >= ci, 1.0, 0.0).astype(BF16)
    masks = [(ri // SUBLANES == ci // SUBLANES) & (ri >= ci)]
    for size in (16, 32, 64):
        masks.append((ri // size == ci // size) & (ri // (size // 2) > ci // (size // 2)))

    def chunk(ck):
        rows = pl.ds(ck * CHUNK, CHUNK)
        f = lb + (1.0 - lb) * jax.nn.sigmoid(proj_ref[rows, COL_HF:COL_HF + _W])
        g_hg = jnp.log(f)
        z = _dot(proj_ref[rows, COL_GLR:COL_GLR + LANES].astype(BF16), wgk_ref[...]) + bgk_ref[...]
        g_gla = (jnp.minimum(z, 0.0) - jnp.log1p(jnp.exp(-jnp.abs(z)))) * (1.0 / GLA_GATE_NORMALIZER)
        g_all = jnp.concatenate([g_hg, g_gla], axis=-1)
        g_hi = g_all.astype(BF16)
        g_lo = (g_all - g_hi.astype(F32)).astype(BF16)
        b2_scr[...] = (_dot(tri, g_hi) + _dot(tri, g_lo)) * LOG2E
        hq = proj_ref[rows, COL_HQ:COL_HQ + _W]
        q_scr[:, 0:_W] = hq * jax.nn.sigmoid(hq)
        q_scr[:, _W:2 * _W] = proj_ref[rows, COL_GQ:COL_GQ + _W] * (GLA_KDIM ** -0.5)
        k_scr[:, 0:_W] = 1.0 - f
        k_scr[:, _W:2 * _W] = proj_ref[rows, COL_GK:COL_GK + _W]

        for hd in range(N_HEADS):
            lo = hd * HEAD_W
            col_v = COL_HI + lo if hd < HG_HEADS else COL_GV + lo - _W
            o = _head_chunk(hd, q_scr, k_scr, b2_scr, proj_ref[rows, col_v:col_v + HEAD_W],
                            st_ref.at[hd], sc_scr.at[hd], masks)
            oall_ref[rows, lo:lo + HEAD_W] = o

    for ck in range(MIX_ROWS // CHUNK):
        chunk(ck)

    ys = []
    for hd in range(N_HEADS):
        lo = hd * HEAD_W
        o = oall_ref[:, lo:lo + HEAD_W]
        if hd < HG_HEADS:
            w, gate = hgw_ref[...], proj_ref[:, COL_HGATE + lo:COL_HGATE + lo + HEAD_W]
        else:
            w, gate = glw_ref[...], proj_ref[:, COL_GGATE + lo - _W:COL_GGATE + lo - _W + HEAD_W]
        ys.append((_rms(o, w) * (gate * jax.nn.sigmoid(gate))).astype(BF16))
    y = jnp.concatenate(ys, axis=-1)
    o_ref[0] = x + _dot(y, wout_ref[...])


def _token_mix(x, norm_w, w_in_p, lb_logits, hg_onorm_w, w_gk2_p, b_gk_p, gla_onorm_w, w_out):
    bsz, seq, d = x.shape
    const = lambda shape: pl.BlockSpec(shape, lambda b, t: (0,) * len(shape))
    return pl.pallas_call(
        _mix_kernel,
        grid=(bsz, seq // MIX_ROWS),
        in_specs=[
            pl.BlockSpec((1, MIX_ROWS, d), lambda b, t: (b, t, 0)),
            const((1, d)),
            const((d, PROJ_W)),
            const(lb_logits.shape),
            const((1, HEAD_W)),
            const((LANES, _W)),
            const((1, _W)),
            const((1, HEAD_W)),
            const((2 * _W, d)),
        ],
        out_specs=pl.BlockSpec((1, MIX_ROWS, d), lambda b, t: (b, t, 0)),
        out_shape=jax.ShapeDtypeStruct((bsz, seq, d), F32),
        scratch_shapes=[
            pltpu.VMEM((MIX_ROWS, PROJ_W), F32),
            pltpu.VMEM((MIX_ROWS, N_HEADS * HEAD_W), F32),
            pltpu.VMEM((N_HEADS, HEAD_W, HEAD_W), F32),
            pltpu.VMEM((CHUNK, N_HEADS * HEAD_W), F32),
            pltpu.VMEM((CHUNK, N_HEADS * HEAD_W), F32),
            pltpu.VMEM((CHUNK, N_HEADS * HEAD_W), F32),
            pltpu.VMEM((N_HEADS, CHUNK, CHUNK), F32),
        ],
        compiler_params=pltpu.CompilerParams(
            dimension_semantics=("arbitrary", "arbitrary"), vmem_limit_bytes=VMEM_LIMIT_BYTES),
        name="token_mix",
    )(x, norm_w, w_in_p, lb_logits, hg_onorm_w, w_gk2_p, b_gk_p, gla_onorm_w, w_out)


def _xattn_tile(x, k_scr, v_scr, nxw_ref, wq_ref, wo_ref, nmoe_ref, wr_hi_ref, wr_lo_ref, br_ref):
    d = x.shape[-1]
    hdim = d // XA_HEADS
    q = _dot(_rms(x, nxw_ref[...]).astype(BF16), wq_ref[...])
    outs = []
    for h in range(XA_HEADS):
        sl = slice(h * hdim, (h + 1) * hdim)
        s = _dot_nt(q[:, sl].astype(BF16), k_scr[:, sl]) * (hdim ** -0.5)
        p = jnp.exp(s - jnp.max(s, axis=-1, keepdims=True))
        p = p / jnp.sum(p, axis=-1, keepdims=True)
        outs.append(_dot(p.astype(BF16), v_scr[:, sl]).astype(BF16))
    x2 = x + _dot(jnp.concatenate(outs, axis=-1), wo_ref[...])

    hm = _rms(x2, nmoe_ref[...])
    hm_hi = hm.astype(BF16)

    hm_lo = (hm - hm_hi.astype(F32)).astype(BF16)
    logits = (_dot(hm_hi, wr_hi_ref[...]) + _dot(hm_lo, wr_hi_ref[...]) + _dot(hm_hi, wr_lo_ref[...])
              + br_ref[...])

    rows = logits.shape[0]
    lane = lax.broadcasted_iota(jnp.int32, (rows, LANES), 1)
    lane_f = lane.astype(F32)
    neg_inf = jnp.float32(-jnp.inf)
    top_v, top_i, hots = [], [], []
    work = logits
    for _ in range(TOP_K):
        m = jnp.max(work, axis=-1, keepdims=True)
        idx = jnp.min(jnp.where(work == m, lane_f, float(LANES)), axis=-1, keepdims=True)
        hot = lane_f == idx
        work = jnp.where(hot, neg_inf, work)
        top_v.append(m)
        top_i.append(idx)
        hots.append(hot)
    es = [jnp.exp(v - top_v[0]) for v in top_v]
    denom = es[0] + es[1] + es[2] + es[3]
    gates = [e / denom for e in es]

    chosen = jnp.where(hots[0] | hots[1] | hots[2] | hots[3], 1.0, 0.0)
    ri = lax.broadcasted_iota(jnp.int32, (rows, rows), 0)
    ci = lax.broadcasted_iota(jnp.int32, (rows, rows), 1)
    strict = jnp.where(ri > ci, 1.0, 0.0).astype(BF16)
    before = _dot(strict, chosen.astype(BF16))
    counts = jnp.sum(chosen, axis=0, keepdims=True)
    padded = jnp.floor((counts + (ROW_ALIGN - 1.0)) * (1.0 / ROW_ALIGN)) * ROW_ALIGN
    li = lax.broadcasted_iota(jnp.int32, (LANES, LANES), 0)
    lj = lax.broadcasted_iota(jnp.int32, (LANES, LANES), 1)
    group_start = _dot(jnp.broadcast_to(padded, (SUBLANES, LANES)).astype(BF16),
                       jnp.where(li < lj, 1.0, 0.0).astype(BF16))[0:1]
    slot_of = before + group_start
    slots = [jnp.sum(jnp.where(hot, slot_of, 0.0), axis=-1, keepdims=True) for hot in hots]

    meta = jnp.zeros((rows, LANES), F32)
    for j, col in enumerate(top_i + slots + gates):
        meta = jnp.where(lane == j, col, meta)
    return x2, hm_hi, meta, counts


def _xattn_kernel(x_ref, mem_ref, nxw_ref, nmw_ref, wq_ref, wkv_ref, wo_ref, nmoe_ref,
                  wr_hi_ref, wr_lo_ref, br_ref,
                  x2_ref, hm_ref, meta_ref, meta_t_ref, cnt_ref, k_scr, v_scr):
    d = x_ref.shape[-1]

    @pl.when(pl.program_id(1) == 0)
    def _():
        m = _rms(mem_ref[0], nmw_ref[...]).astype(BF16)
        kv = _dot(m, wkv_ref[...])
        k_scr[...] = kv[:, :d].astype(BF16)
        v_scr[...] = kv[:, d:].astype(BF16)

    for u in range(2):
        rows = slice(u * TOK_TILE, (u + 1) * TOK_TILE)
        x2, hm16, meta, counts = _xattn_tile(x_ref[0, rows, :], k_scr, v_scr, nxw_ref, wq_ref, wo_ref, nmoe_ref,
                                             wr_hi_ref, wr_lo_ref, br_ref)
        x2_ref[0, rows, :] = x2
        hm_ref[0, rows, :] = hm16
        meta_ref[rows, :] = meta
        meta_t_ref[u] = meta.T[0:META_ROWS, :]
        cnt_ref[u * SUBLANES:(u + 1) * SUBLANES, :] = jnp.broadcast_to(counts, (SUBLANES, LANES))


def _xattn_router(x1, mem, norm_xa_w, norm_mem_w, w_xq, w_xkv, w_xo, norm_moe_w, wr_hi, wr_lo, br_p):
    bsz, seq, d = x1.shape
    mlen = mem.shape[1]
    n_t = seq // (2 * TOK_TILE)
    const = lambda shape: pl.BlockSpec(shape, lambda b, t: (0,) * len(shape))
    return pl.pallas_call(
        _xattn_kernel,
        grid=(bsz, n_t),
        in_specs=[
            pl.BlockSpec((1, 2 * TOK_TILE, d), lambda b, t: (b, t, 0)),
            pl.BlockSpec((1, mlen, d), lambda b, t: (b, 0, 0)),
            const((1, d)), const((1, d)),
            const((d, d)), const((d, 2 * d)), const((d, d)),
            const((1, d)),
            const((d, LANES)), const((d, LANES)), const((1, LANES)),
        ],
        out_specs=[
            pl.BlockSpec((1, 2 * TOK_TILE, d), lambda b, t: (b, t, 0)),
            pl.BlockSpec((1, 2 * TOK_TILE, d), lambda b, t: (b, t, 0)),
            pl.BlockSpec((2 * TOK_TILE, LANES), lambda b, t: (b * n_t + t, 0)),
            pl.BlockSpec((2, META_ROWS, TOK_TILE), lambda b, t: (b * n_t + t, 0, 0)),
            pl.BlockSpec((2 * SUBLANES, LANES), lambda b, t: (b * n_t + t, 0)),
        ],
        out_shape=[
            jax.ShapeDtypeStruct((bsz, seq, d), F32),
            jax.ShapeDtypeStruct((bsz, seq, d), BF16),
            jax.ShapeDtypeStruct((bsz * seq, LANES), F32),
            jax.ShapeDtypeStruct((bsz * seq // TOK_TILE, META_ROWS, TOK_TILE), F32),
            jax.ShapeDtypeStruct((bsz * seq // TOK_TILE * SUBLANES, LANES), F32),
        ],
        scratch_shapes=[
            pltpu.VMEM((mlen, d), BF16),
            pltpu.VMEM((mlen, d), BF16),
        ],
        compiler_params=pltpu.CompilerParams(
            dimension_semantics=("arbitrary", "arbitrary"), vmem_limit_bytes=VMEM_LIMIT_BYTES),
        name="xattn_router",
    )(x1, mem, norm_xa_w, norm_mem_w, w_xq, w_xkv, w_xo, norm_moe_w, wr_hi, wr_lo, br_p)


def _piece_copy(src_ref, src_row, dst_ref, dst_row, sem):
    return pltpu.make_async_copy(src_ref.at[pl.ds(pl.multiple_of(src_row, ROW_ALIGN), ROW_ALIGN)],
                                 dst_ref.at[pl.ds(pl.multiple_of(dst_row, ROW_ALIGN), ROW_ALIGN)], sem)


def _group_copies(grp_ref, tile, make_copy):
    def per_expert(e, carry):
        src = grp_ref[tile, 0, e]
        dst = grp_ref[tile, 1, e]
        n = grp_ref[tile, 2, e]
        def classes(src, dst, sizes):
            for i, size in enumerate(sizes):
                @pl.when((n & size) != 0)
                def _(src=src, dst=dst, size=size, queue=i % 2):
                    make_copy(pl.multiple_of(src, ROW_ALIGN), pl.multiple_of(dst, ROW_ALIGN),
                              size * ROW_ALIGN).start(priority=queue)
                step = (n & size) * ROW_ALIGN
                src, dst = src + step, dst + step

        large = [s for s in (32, 16, 8) if s <= TOK_TILE // ROW_ALIGN]
        small = (4, 2, 1)

        @pl.when(n >= large[-1])
        def _():
            classes(src, dst, large)

        skip = (n - (n & (large[-1] - 1))) * ROW_ALIGN
        classes(src + skip, dst + skip, small)
        return carry
    lax.fori_loop(0, N_EXPERTS, per_expert, 0)


def _wait_pieces(src_ref, dst_ref, n, sem):
    size = MAX_PIECES
    while size & (size - 1):
        size &= size - 1
    while size >= 1:
        @pl.when((n & size) != 0)
        def _(size=size):
            pltpu.make_async_copy(src_ref.at[pl.ds(0, size * ROW_ALIGN)], dst_ref.at[pl.ds(0, size * ROW_ALIGN)], sem).wait()
        size //= 2


def _block_copy(src_ref, dst_ref, dst_block, sem):
    return pltpu.make_async_copy(src_ref, dst_ref.at[pl.ds(pl.multiple_of(dst_block * MOE_ROWS, MOE_ROWS), MOE_ROWS)], sem)


def _dispatch_kernel(grp_ref, np_ref, tail_ref, cap_ref, mt_ref, hm_ref, xs_ref, stage_ref, zero_ref, sem, sem_z):
    s = pl.program_id(0)
    last = pl.num_programs(0) - 1
    d = hm_ref.shape[1]

    def spare_blocks(fn):
        def body(i, c):
            fn(_block_copy(zero_ref, xs_ref, cap_ref[0] + i, sem_z))
            return c
        lax.fori_loop(0, cap_ref[1], body, 0)

    @pl.when(s == 0)
    def _():
        zero_ref[...] = jnp.zeros_like(zero_ref)
        spare_blocks(lambda cp: cp.start())

        def per_expert(e, carry):
            n = tail_ref[1, e]
            off = tail_ref[0, e]

            def start(i, c):
                _piece_copy(zero_ref, 0, xs_ref, off + i * ROW_ALIGN, sem.at[0]).start()
                return c

            def wait(i, c):
                _piece_copy(zero_ref, 0, xs_ref, 0, sem.at[0]).wait()
                return c

            lax.fori_loop(0, n, start, 0)
            lax.fori_loop(0, n, wait, 0)
            return carry

        lax.fori_loop(0, N_EXPERTS, per_expert, 0)

    def drain(u, tile):
        _wait_pieces(stage_ref.at[u], xs_ref, np_ref[tile], sem.at[u])

    for u in range(MOE_TILES):
        tile = MOE_TILES * s + u

        @pl.when(s > 0)
        def _():
            drain(u, tile - MOE_TILES)

        x16 = hm_ref[u * TOK_TILE:(u + 1) * TOK_TILE, :]
        tr = mt_ref[u, TOP_K:2 * TOP_K, :].astype(jnp.int32)
        g = mt_ref[u, 2 * TOP_K:3 * TOP_K, :]
        for c in range(STAGE_ROWS // TOK_TILE):
            rows = slice(c * TOK_TILE, (c + 1) * TOK_TILE)
            rid = lax.broadcasted_iota(jnp.int32, (TOK_TILE, TOK_TILE), 0) + c * TOK_TILE
            hot = [rid == tr[k:k + 1, :] for k in range(TOP_K)]
            sel = jnp.where(hot[0] | hot[1] | hot[2] | hot[3], 1.0, 0.0).astype(BF16)
            stage_ref[u, rows, 0:d] = _dot(sel, x16)
            gsum = jnp.zeros((TOK_TILE, 1), F32)
            for k in range(TOP_K):
                gsum = gsum + jnp.sum(jnp.where(hot[k], g[k:k + 1, :], 0.0), axis=-1, keepdims=True)
            stage_ref[u, rows, d:d + LANES] = jnp.broadcast_to(gsum, (TOK_TILE, LANES))

        _group_copies(grp_ref, tile, lambda src, dst, rows: pltpu.make_async_copy(
            stage_ref.at[u, pl.ds(src, rows)], xs_ref.at[pl.ds(dst, rows)], sem.at[u]))

    @pl.when(s == last)
    def _():
        for u in range(MOE_TILES):
            drain(u, MOE_TILES * s + u)
        spare_blocks(lambda cp: cp.wait())


def _dispatch(hm, groups, n_pieces, tail_info, spare, meta_t, n_rows):
    n_tok, d = hm.shape
    n_tiles = n_tok // TOK_TILE
    assert n_tiles % MOE_TILES == 0
    smem = pl.BlockSpec(memory_space=pltpu.SMEM)
    return pl.pallas_call(
        _dispatch_kernel,
        grid=(n_tiles // MOE_TILES,),
        in_specs=[
            smem, smem, smem, smem,
            pl.BlockSpec((MOE_TILES, META_ROWS, TOK_TILE), lambda s: (s, 0, 0)),
            pl.BlockSpec((MOE_TILES * TOK_TILE, d), lambda s: (s, 0)),
        ],
        out_specs=pl.BlockSpec(memory_space=pl.ANY),
        out_shape=jax.ShapeDtypeStruct((n_rows, d + LANES), F32),
        scratch_shapes=[pltpu.VMEM((MOE_TILES, STAGE_ROWS, d + LANES), F32), pltpu.VMEM((MOE_ROWS, d + LANES), F32),
                        pltpu.SemaphoreType.DMA((MOE_TILES,)), pltpu.SemaphoreType.DMA],
        compiler_params=pltpu.CompilerParams(
            dimension_semantics=("arbitrary",), vmem_limit_bytes=VMEM_LIMIT_BYTES),
        name="moe_dispatch",
    )(groups, n_pieces, tail_info, spare, meta_t, hm)


def _expert_kernel(be_ref, nu_ref, first_ref, slot_ref, nxt_ref, xs_ref, wgu_hbm, bgu_ref, wd_hbm,
                   bd_ref, ys_ref, wgu32_ref, wd32_ref, wgu16_ref, wd16_ref, sem):
    i = pl.program_id(0)
    f, d = wd16_ref.shape

    def weight_copies(e, slot):
        return (pltpu.make_async_copy(wgu_hbm.at[e], wgu32_ref.at[slot], sem.at[0, slot]),
                pltpu.make_async_copy(wd_hbm.at[e], wd32_ref.at[slot], sem.at[1, slot]))

    @pl.when(i == 0)
    def _():
        for cp in weight_copies(be_ref[0], 0):
            cp.start()

    @pl.when(first_ref[i] == 1)
    def _():
        slot = slot_ref[i]
        for cp in weight_copies(be_ref[i], slot):
            cp.wait()

        @pl.when(nxt_ref[i] >= 0)
        def _():
            for cp in weight_copies(nxt_ref[i], 1 - slot):
                cp.start()

        wgu16_ref[...] = wgu32_ref[slot].astype(BF16)
        wd16_ref[...] = wd32_ref[slot].astype(BF16)

    @pl.when(i < nu_ref[0])
    def _():
        gu = _dot(xs_ref[:, 0:d].astype(BF16), wgu16_ref[...]) + bgu_ref[0]
        gate = jnp.minimum(gu[:, :f], SWIGLU_LIMIT)
        up = jnp.clip(gu[:, f:], -SWIGLU_LIMIT, SWIGLU_LIMIT)
        act = (up + 1.0) * gate * jax.nn.sigmoid(SWIGLU_ALPHA * gate)
        ys_ref[...] = (_dot(act.astype(BF16), wd16_ref[...]) + bd_ref[0]) * xs_ref[:, d:d + 1]

    @pl.when(i >= nu_ref[0])
    def _():
        ys_ref[...] = jnp.zeros_like(ys_ref)


def _expert_mlp(xs, block_e, n_used, first, slot, nxt, w_gu, b_gu, w_d, b_d):
    n_rows = xs.shape[0]
    d = w_d.shape[2]
    n_blocks = n_rows // MOE_ROWS
    f = w_d.shape[1]
    row_map = lambda i, be, nu, *_: (jnp.minimum(i, nu[0] - 1), 0)
    exp_map = lambda i, be, *_: (be[i], 0, 0)
    return pl.pallas_call(
        _expert_kernel,
        grid_spec=pltpu.PrefetchScalarGridSpec(
            num_scalar_prefetch=5,
            grid=(n_blocks,),
            in_specs=[
                pl.BlockSpec((MOE_ROWS, d + LANES), row_map),
                pl.BlockSpec(memory_space=pl.ANY),
                pl.BlockSpec((1, 1, 2 * f), exp_map),
                pl.BlockSpec(memory_space=pl.ANY),
                pl.BlockSpec((1, 1, d), exp_map),
            ],
            out_specs=pl.BlockSpec((MOE_ROWS, d), lambda i, *_: (i, 0)),
            scratch_shapes=[
                pltpu.VMEM((2, d, 2 * f), F32), pltpu.VMEM((2, f, d), F32),
                pltpu.VMEM((d, 2 * f), BF16), pltpu.VMEM((f, d), BF16),
                pltpu.SemaphoreType.DMA((2, 2)),
            ],
        ),
        out_shape=jax.ShapeDtypeStruct((n_rows, d), F32),
        compiler_params=pltpu.CompilerParams(
            dimension_semantics=("arbitrary",), vmem_limit_bytes=VMEM_LIMIT_BYTES),
        name="moe_experts",
    )(block_e, n_used, first, slot, nxt, xs, w_gu, b_gu, w_d, b_d)


def _combine_kernel(grp_ref, np_ref, x2_ref, meta_ref, nfw_ref, ys_ref, out_ref, ybuf_ref, sem):
    s = pl.program_id(0)
    last = pl.num_programs(0) - 1

    def fetch(u, tile):
        _group_copies(grp_ref, tile, lambda src, dst, rows: pltpu.make_async_copy(
            ys_ref.at[pl.ds(dst, rows)], ybuf_ref.at[u, pl.ds(src, rows)], sem.at[u]))

    def finish(u, tile):
        _wait_pieces(ys_ref, ybuf_ref.at[u], np_ref[tile], sem.at[u])

        rows = slice(u * TOK_TILE, (u + 1) * TOK_TILE)
        tr = meta_ref[rows, TOP_K:2 * TOP_K].astype(jnp.int32)
        acc = x2_ref[rows, :]
        for c in range(STAGE_ROWS // TOK_TILE):
            cid = lax.broadcasted_iota(jnp.int32, (TOK_TILE, TOK_TILE), 1) + c * TOK_TILE
            hot = [cid == tr[:, k:k + 1] for k in range(TOP_K)]
            sel = jnp.where(hot[0] | hot[1] | hot[2] | hot[3], 1.0, 0.0).astype(BF16)
            acc = acc + _dot(sel, ybuf_ref[u, c * TOK_TILE:(c + 1) * TOK_TILE, :].astype(BF16))
        out_ref[rows, :] = _rms(acc, nfw_ref[...])

    @pl.when(s == 0)
    def _():
        ybuf_ref[...] = jnp.zeros_like(ybuf_ref)
        for u in range(MOE_TILES - 1):
            fetch(u, u)

    for u in range(MOE_TILES):
        tile = MOE_TILES * s + u
        ahead = u + MOE_TILES - 1
        if ahead < MOE_TILES:
            fetch(ahead, MOE_TILES * s + ahead)
        else:
            @pl.when(s < last)
            def _():
                fetch(ahead - MOE_TILES, MOE_TILES * s + ahead)
        finish(u, tile)


def _combine(x2, meta, groups, n_pieces, ys, norm_final_w):
    n_tok, d = x2.shape
    n_tiles = n_tok // TOK_TILE
    assert n_tiles % MOE_TILES == 0
    smem = pl.BlockSpec(memory_space=pltpu.SMEM)
    return pl.pallas_call(
        _combine_kernel,
        grid=(n_tiles // MOE_TILES,),
        in_specs=[
            smem, smem,
            pl.BlockSpec((MOE_TILES * TOK_TILE, d), lambda s: (s, 0)),
            pl.BlockSpec((MOE_TILES * TOK_TILE, LANES), lambda s: (s, 0)),
            pl.BlockSpec((1, d), lambda s: (0, 0)),
            pl.BlockSpec(memory_space=pl.ANY),
        ],
        out_specs=pl.BlockSpec((MOE_TILES * TOK_TILE, d), lambda s: (s, 0)),
        out_shape=jax.ShapeDtypeStruct((n_tok, d), F32),
        scratch_shapes=[pltpu.VMEM((MOE_TILES, STAGE_ROWS, d), F32), pltpu.SemaphoreType.DMA((MOE_TILES,))],
        compiler_params=pltpu.CompilerParams(
            dimension_semantics=("arbitrary",), vmem_limit_bytes=VMEM_LIMIT_BYTES),
        name="moe_combine",
    )(groups, n_pieces, x2, meta, norm_final_w, ys)


def _pad_heads(w, n_heads, width):
    lead = w.shape[:-1]
    w = w.reshape(lead + (n_heads, width))
    w = jnp.pad(w, [(0, 0)] * len(lead) + [(0, 0), (0, HEAD_W - width)])
    return w.reshape(lead + (n_heads * HEAD_W,))


def _round_up(x, m):
    return (x + m - 1) // m * m


def _routing_tables(cnt, n_tiles):
    i32 = jnp.int32
    counts = cnt.reshape(n_tiles, SUBLANES, LANES)[:, 0, :N_EXPERTS].astype(i32)
    cnt8 = _round_up(counts, ROW_ALIGN)
    lend = jnp.cumsum(cnt8, axis=1)
    lstart = lend - cnt8
    tot = jnp.sum(cnt8, axis=0)
    padded = _round_up(tot, MOE_ROWS)
    pends = jnp.cumsum(padded)
    pstarts = pends - padded
    goff = pstarts[None, :] + jnp.cumsum(cnt8, axis=0) - cnt8

    groups = jnp.stack([lstart, goff, cnt8 // ROW_ALIGN], axis=1).astype(i32)
    n_pieces = (lend[:, -1] // ROW_ALIGN).astype(i32)

    max_rows = n_tiles * (TOK_TILE * TOP_K + N_EXPERTS * (ROW_ALIGN - 1)) + N_EXPERTS * (MOE_ROWS - ROW_ALIGN)
    n_blocks = -(-max_rows // MOE_ROWS)
    n_used = (pends[-1] // MOE_ROWS).astype(i32)
    blk = jnp.arange(n_blocks, dtype=i32)
    block_e = jnp.minimum(jnp.sum(blk[:, None] * MOE_ROWS >= pends[None, :], axis=-1), N_EXPERTS - 1).astype(i32)
    block_e = jnp.where(blk < n_used, block_e, block_e[jnp.maximum(n_used - 1, 0)])
    first = ((blk < n_used) & ((blk == 0) | (block_e != jnp.roll(block_e, 1)))).astype(i32)
    has_rows = padded > 0
    slot_e = (jnp.cumsum(has_rows.astype(i32)) - 1) % 2
    later = jnp.arange(N_EXPERTS, dtype=i32)[None, :] > jnp.arange(N_EXPERTS, dtype=i32)[:, None]
    nxt_e = jnp.min(jnp.where(later & has_rows[None, :], jnp.arange(N_EXPERTS, dtype=i32)[None, :], N_EXPERTS), axis=1)
    nxt_e = jnp.where(nxt_e < N_EXPERTS, nxt_e, -1).astype(i32)
    tail_info = jnp.stack([pstarts + tot, (padded - tot) // ROW_ALIGN]).astype(i32)
    spare = jnp.stack([n_used, n_blocks - n_used]).astype(i32)
    of_block = block_e[:, None] == jnp.arange(N_EXPERTS, dtype=i32)
    slot_b = jnp.sum(jnp.where(of_block, slot_e[None, :], 0), axis=1).astype(i32)
    nxt_b = jnp.sum(jnp.where(of_block, nxt_e[None, :], 0), axis=1).astype(i32)
    return (groups, n_pieces, tail_info, spare,
            (block_e, n_used.reshape(1), first, slot_b, nxt_b), n_blocks * MOE_ROWS)


def kernel(x, mem, norm_mix_w, w_in, hg_lb_logits, hg_onorm_w, gla_w_gk2, gla_b_gk, gla_onorm_w, w_out, norm_xa_w, norm_mem_w, w_xq, w_xkv, w_xo, norm_moe_w, w_router, b_router, w_gate_up, b_gate_up, w_down, b_down, norm_final_w):
    assert w_in.shape[0] == 1, "single-layer block"
    bsz, seq, d = x.shape
    n_tok = bsz * seq
    gk = GLA_HEADS * GLA_KDIM

    wi = w_in[0]
    w_in_p = jnp.concatenate([
        wi[:, :4 * _W],
        _pad_heads(wi[:, 4 * _W:4 * _W + gk], GLA_HEADS, GLA_KDIM),
        _pad_heads(wi[:, 4 * _W + gk:4 * _W + 2 * gk], GLA_HEADS, GLA_KDIM),
        wi[:, 4 * _W + 2 * gk:4 * _W + 2 * gk + 2 * _W],
        jnp.pad(wi[:, 4 * _W + 2 * gk + 2 * _W:], ((0, 0), (0, LANES - GLA_RANK))),
    ], axis=1).astype(BF16)
    w_gk2_p = jnp.pad(_pad_heads(gla_w_gk2[0], GLA_HEADS, GLA_KDIM),
                      ((0, LANES - GLA_RANK), (0, 0))).astype(BF16)
    b_gk_p = _pad_heads(gla_b_gk, GLA_HEADS, GLA_KDIM)
    wr = jnp.pad(w_router[0], ((0, 0), (0, LANES - N_EXPERTS)))
    wr_hi = wr.astype(BF16)
    wr_lo = (wr - wr_hi.astype(F32)).astype(BF16)
    br_p = jnp.pad(b_router, ((0, 0), (0, LANES - N_EXPERTS)), constant_values=-1e30)

    x1 = _token_mix(x, norm_mix_w, w_in_p, hg_lb_logits, hg_onorm_w, w_gk2_p, b_gk_p, gla_onorm_w,
                    w_out[0].astype(BF16))
    x2, hm, meta, meta_t, cnt = _xattn_router(x1, mem, norm_xa_w, norm_mem_w, w_xq[0].astype(BF16),
                                              w_xkv[0].astype(BF16), w_xo[0].astype(BF16), norm_moe_w,
                                              wr_hi, wr_lo, br_p)

    n_tiles = n_tok // TOK_TILE
    groups, n_pieces, tail_info, spare, block_tables, n_rows = _routing_tables(cnt, n_tiles)

    xs = _dispatch(hm.reshape(n_tok, d), groups, n_pieces, tail_info, spare, meta_t, n_rows)
    ys = _expert_mlp(xs, *block_tables, w_gate_up[0], b_gate_up[0][:, None, :],
                     w_down[0], b_down[0][:, None, :])
    out = _combine(x2.reshape(n_tok, d), meta, groups, n_pieces, ys, norm_final_w[None, :])
    return out.reshape(bsz, seq, d)
```

```python
import jax
import jax.numpy as jnp
from jax import lax
from jax.experimental import pallas as pl
from jax.experimental.pallas import tpu as pltpu

F32 = jnp.float32
BF16 = jnp.bfloat16
EPS = 1e-6
LOG2E = 1.4426950408889634

HEAD_W = 128
HG_HEADS = 4
GLA_HEADS = 4
GLA_KDIM = 64
N_HEADS = HG_HEADS + GLA_HEADS
GLA_RANK = 16
GLA_GATE_NORMALIZER = 16.0
CHUNK = 64
XA_HEADS = 4
N_EXPERTS = 32
TOP_K = 4
SWIGLU_LIMIT = 7.0
SWIGLU_ALPHA = 1.702

LANES = 128
SUBLANES = 8
VMEM_PER_CORE_BYTES = 64 * 1024 * 1024
VMEM_LIMIT_BYTES = VMEM_PER_CORE_BYTES * 7 // 8

MIX_ROWS = 256
TOK_TILE = 512
MOE_ROWS = 512
ROW_ALIGN = SUBLANES
STAGE_ROWS = 2560
MAX_PIECES = STAGE_ROWS // ROW_ALIGN
MOE_TILES = 2
META_ROWS = 16

_W = HG_HEADS * HEAD_W
COL_HQ, COL_HF, COL_HI, COL_HGATE = 0, _W, 2 * _W, 3 * _W
COL_GQ, COL_GK, COL_GV, COL_GGATE = 4 * _W, 5 * _W, 6 * _W, 7 * _W
COL_GLR = 8 * _W
PROJ_W = COL_GLR + LANES


def _rms(x, w):
    return x * lax.rsqrt(jnp.mean(x * x, axis=-1, keepdims=True) + EPS) * w


def _dot(a, b):
    return jnp.dot(a, b, preferred_element_type=F32)


def _dot_nt(a, b):
    return lax.dot_general(a, b, (((1,), (1,)), ((), ())), preferred_element_type=F32)


def _dot_tn(a, b):
    return lax.dot_general(a, b, (((0,), (0,)), ((), ())), preferred_element_type=F32)


def _head_chunk(hd, q_scr, k_scr, b2_scr, v, st_ref, sc_ref, masks):
    sl = slice(hd * HEAD_W, (hd + 1) * HEAD_W)
    q, k, b2 = q_scr[:, sl], k_scr[:, sl], b2_scr[:, sl]

    for s in range(0, CHUNK, SUBLANES):
        q_blk, b_blk = q[s:s + SUBLANES], b2[s:s + SUBLANES]
        for j in range(s, s + SUBLANES):
            e = jnp.exp2(b_blk - b2_scr[j:j + 1, sl])
            sc_ref[s:s + SUBLANES, j:j + 1] = jnp.sum(q_blk * (k_scr[j:j + 1, sl] * e), axis=-1, keepdims=True)
    scores = jnp.where(masks[0], sc_ref[...], 0.0)

    for lvl, size in enumerate((16, 32, 64)):
        mids = [b2_scr[r:r + 1, sl] for r in range(size // 2 - 1, CHUNK, size)]
        mid = jnp.concatenate([jnp.broadcast_to(m, (size, HEAD_W)) for m in mids], axis=0)
        dist = b2 - mid
        e = jnp.exp2(jnp.minimum(dist, -dist))
        r = _dot_nt((q * e).astype(BF16), (k * e).astype(BF16))
        scores = jnp.where(masks[lvl + 1], r, scores)

    b2_last = b2_scr[CHUNK - 1:CHUNK, sl]
    q_abs = (q * jnp.exp2(b2)).astype(BF16)
    k_end = (k * jnp.exp2(b2_last - b2)).astype(BF16)
    v16 = v.astype(BF16)
    st = st_ref[...]
    o = _dot(scores.astype(BF16), v16) + _dot_nt(q_abs, st.astype(BF16))
    st_ref[...] = st * jnp.exp2(b2_last) + _dot_tn(v16, k_end)
    return o


def _mix_kernel(x_ref, nw_ref, win_ref, lbl_ref, hgw_ref, wgk_ref, bgk_ref, glw_ref, wout_ref,
                o_ref, proj_ref, oall_ref, st_ref, q_scr, k_scr, b2_scr, sc_scr):
    @pl.when(pl.program_id(1) == 0)
    def _():
        st_ref[...] = jnp.zeros_like(st_ref)
        sc_scr[...] = jnp.zeros_like(sc_scr)

    x = x_ref[0]
    h = _rms(x, nw_ref[...]).astype(BF16)
    proj_ref[...] = _dot(h, win_ref[...])

    lbl = lbl_ref[...]
    e = jnp.exp(lbl - jnp.max(lbl, axis=0, keepdims=True))
    lb = e[0:1] / jnp.sum(e, axis=0, keepdims=True)

    ri = lax.broadcasted_iota(jnp.int32, (CHUNK, CHUNK), 0)
    ci = lax.broadcasted_iota(jnp.int32, (CHUNK, CHUNK), 1)
    tri = jnp.where(ri >= ci, 1.0, 0.0).astype(BF16)
    masks = [(ri // SUBLANES == ci // SUBLANES) & (ri >= ci)]
    for size in (16, 32, 64):
        masks.append((ri // size == ci // size) & (ri // (size // 2) > ci // (size // 2)))

    def chunk(ck):
        rows = pl.ds(ck * CHUNK, CHUNK)
        f = lb + (1.0 - lb) * jax.nn.sigmoid(proj_ref[rows, COL_HF:COL_HF + _W])
        g_hg = jnp.log(f)
        z = _dot(proj_ref[rows, COL_GLR:COL_GLR + LANES].astype(BF16), wgk_ref[...]) + bgk_ref[...]
        g_gla = (jnp.minimum(z, 0.0) - jnp.log1p(jnp.exp(-jnp.abs(z)))) * (1.0 / GLA_GATE_NORMALIZER)
        g_all = jnp.concatenate([g_hg, g_gla], axis=-1)
        g_hi = g_all.astype(BF16)
        g_lo = (g_all - g_hi.astype(F32)).astype(BF16)
        b2_scr[...] = (_dot(tri, g_hi) + _dot(tri, g_lo)) * LOG2E
        hq = proj_ref[rows, COL_HQ:COL_HQ + _W]
        q_scr[:, 0:_W] = hq * jax.nn.sigmoid(hq)
        q_scr[:, _W:2 * _W] = proj_ref[rows, COL_GQ:COL_GQ + _W] * (GLA_KDIM ** -0.5)
        k_scr[:, 0:_W] = 1.0 - f
        k_scr[:, _W:2 * _W] = proj_ref[rows, COL_GK:COL_GK + _W]

        for hd in range(N_HEADS):
            lo = hd * HEAD_W
            col_v = COL_HI + lo if hd < HG_HEADS else COL_GV + lo - _W
            o = _head_chunk(hd, q_scr, k_scr, b2_scr, proj_ref[rows, col_v:col_v + HEAD_W],
                            st_ref.at[hd], sc_scr.at[hd], masks)
            oall_ref[rows, lo:lo + HEAD_W] = o

    for ck in range(MIX_ROWS // CHUNK):
        chunk(ck)

    ys = []
    for hd in range(N_HEADS):
        lo = hd * HEAD_W
        o = oall_ref[:, lo:lo + HEAD_W]
        if hd < HG_HEADS:
            w, gate = hgw_ref[...], proj_ref[:, COL_HGATE + lo:COL_HGATE + lo + HEAD_W]
        else:
            w, gate = glw_ref[...], proj_ref[:, COL_GGATE + lo - _W:COL_GGATE + lo - _W + HEAD_W]
        ys.append((_rms(o, w) * (gate * jax.nn.sigmoid(gate))).astype(BF16))
    y = jnp.concatenate(ys, axis=-1)
    o_ref[0] = x + _dot(y, wout_ref[...])


def _token_mix(x, norm_w, w_in_p, lb_logits, hg_onorm_w, w_gk2_p, b_gk_p, gla_onorm_w, w_out):
    bsz, seq, d = x.shape
    const = lambda shape: pl.BlockSpec(shape, lambda b, t: (0,) * len(shape))
    return pl.pallas_call(
        _mix_kernel,
        grid=(bsz, seq // MIX_ROWS),
        in_specs=[
            pl.BlockSpec((1, MIX_ROWS, d), lambda b, t: (b, t, 0)),
            const((1, d)),
            const((d, PROJ_W)),
            const(lb_logits.shape),
            const((1, HEAD_W)),
            const((LANES, _W)),
            const((1, _W)),
            const((1, HEAD_W)),
            const((2 * _W, d)),
        ],
        out_specs=pl.BlockSpec((1, MIX_ROWS, d), lambda b, t: (b, t, 0)),
        out_shape=jax.ShapeDtypeStruct((bsz, seq, d), F32),
        scratch_shapes=[
            pltpu.VMEM((MIX_ROWS, PROJ_W), F32),
            pltpu.VMEM((MIX_ROWS, N_HEADS * HEAD_W), F32),
            pltpu.VMEM((N_HEADS, HEAD_W, HEAD_W), F32),
            pltpu.VMEM((CHUNK, N_HEADS * HEAD_W), F32),
            pltpu.VMEM((CHUNK, N_HEADS * HEAD_W), F32),
            pltpu.VMEM((CHUNK, N_HEADS * HEAD_W), F32),
            pltpu.VMEM((N_HEADS, CHUNK, CHUNK), F32),
        ],
        compiler_params=pltpu.CompilerParams(
            dimension_semantics=("arbitrary", "arbitrary"), vmem_limit_bytes=VMEM_LIMIT_BYTES),
        name="token_mix",
    )(x, norm_w, w_in_p, lb_logits, hg_onorm_w, w_gk2_p, b_gk_p, gla_onorm_w, w_out)


def _xattn_tile(x, k_scr, v_scr, nxw_ref, wq_ref, wo_ref, nmoe_ref, wr_hi_ref, wr_lo_ref, br_ref):
    d = x.shape[-1]
    hdim = d // XA_HEADS
    q = _dot(_rms(x, nxw_ref[...]).astype(BF16), wq_ref[...])
    outs = []
    for h in range(XA_HEADS):
        sl = slice(h * hdim, (h + 1) * hdim)
        s = _dot_nt(q[:, sl].astype(BF16), k_scr[:, sl]) * (hdim ** -0.5)
        p = jnp.exp(s - jnp.max(s, axis=-1, keepdims=True))
        p = p / jnp.sum(p, axis=-1, keepdims=True)
        outs.append(_dot(p.astype(BF16), v_scr[:, sl]).astype(BF16))
    x2 = x + _dot(jnp.concatenate(outs, axis=-1), wo_ref[...])

    hm = _rms(x2, nmoe_ref[...])
    hm_hi = hm.astype(BF16)

    hm_lo = (hm - hm_hi.astype(F32)).astype(BF16)
    logits = (_dot(hm_hi, wr_hi_ref[...]) + _dot(hm_lo, wr_hi_ref[...]) + _dot(hm_hi, wr_lo_ref[...])
              + br_ref[...])

    rows = logits.shape[0]
    lane = lax.broadcasted_iota(jnp.int32, (rows, LANES), 1)
    lane_f = lane.astype(F32)
    neg_inf = jnp.float32(-jnp.inf)
    top_v, top_i, hots = [], [], []
    work = logits
    for _ in range(TOP_K):
        m = jnp.max(work, axis=-1, keepdims=True)
        idx = jnp.min(jnp.where(work == m, lane_f, float(LANES)), axis=-1, keepdims=True)
        hot = lane_f == idx
        work = jnp.where(hot, neg_inf, work)
        top_v.append(m)
        top_i.append(idx)
        hots.append(hot)
    es = [jnp.exp(v - top_v[0]) for v in top_v]
    denom = es[0] + es[1] + es[2] + es[3]
    gates = [e / denom for e in es]

    chosen = jnp.where(hots[0] | hots[1] | hots[2] | hots[3], 1.0, 0.0)
    ri = lax.broadcasted_iota(jnp.int32, (rows, rows), 0)
    ci = lax.broadcasted_iota(jnp.int32, (rows, rows), 1)
    strict = jnp.where(ri > ci, 1.0, 0.0).astype(BF16)
    before = _dot(strict, chosen.astype(BF16))
    counts = jnp.sum(chosen, axis=0, keepdims=True)
    padded = jnp.floor((counts + (ROW_ALIGN - 1.0)) * (1.0 / ROW_ALIGN)) * ROW_ALIGN
    li = lax.broadcasted_iota(jnp.int32, (LANES, LANES), 0)
    lj = lax.broadcasted_iota(jnp.int32, (LANES, LANES), 1)
    group_start = _dot(jnp.broadcast_to(padded, (SUBLANES, LANES)).astype(BF16),
                       jnp.where(li < lj, 1.0, 0.0).astype(BF16))[0:1]
    slot_of = before + group_start
    slots = [jnp.sum(jnp.where(hot, slot_of, 0.0), axis=-1, keepdims=True) for hot in hots]

    meta = jnp.zeros((rows, LANES), F32)
    for j, col in enumerate(top_i + slots + gates):
        meta = jnp.where(lane == j, col, meta)
    return x2, hm_hi, meta, counts


def _xattn_kernel(x_ref, mem_ref, nxw_ref, nmw_ref, wq_ref, wkv_ref, wo_ref, nmoe_ref,
                  wr_hi_ref, wr_lo_ref, br_ref,
                  x2_ref, hm_ref, meta_ref, meta_t_ref, cnt_ref, k_scr, v_scr):
    d = x_ref.shape[-1]

    @pl.when(pl.program_id(1) == 0)
    def _():
        m = _rms(mem_ref[0], nmw_ref[...]).astype(BF16)
        kv = _dot(m, wkv_ref[...])
        k_scr[...] = kv[:, :d].astype(BF16)
        v_scr[...] = kv[:, d:].astype(BF16)

    for u in range(2):
        rows = slice(u * TOK_TILE, (u + 1) * TOK_TILE)
        x2, hm16, meta, counts = _xattn_tile(x_ref[0, rows, :], k_scr, v_scr, nxw_ref, wq_ref, wo_ref, nmoe_ref,
                                             wr_hi_ref, wr_lo_ref, br_ref)
        x2_ref[0, rows, :] = x2
        hm_ref[0, rows, :] = hm16
        meta_ref[rows, :] = meta
        meta_t_ref[u] = meta.T[0:META_ROWS, :]
        cnt_ref[u * SUBLANES:(u + 1) * SUBLANES, :] = jnp.broadcast_to(counts, (SUBLANES, LANES))


def _xattn_router(x1, mem, norm_xa_w, norm_mem_w, w_xq, w_xkv, w_xo, norm_moe_w, wr_hi, wr_lo, br_p):
    bsz, seq, d = x1.shape
    mlen = mem.shape[1]
    n_t = seq // (2 * TOK_TILE)
    const = lambda shape: pl.BlockSpec(shape, lambda b, t: (0,) * len(shape))
    return pl.pallas_call(
        _xattn_kernel,
        grid=(bsz, n_t),
        in_specs=[
            pl.BlockSpec((1, 2 * TOK_TILE, d), lambda b, t: (b, t, 0)),
            pl.BlockSpec((1, mlen, d), lambda b, t: (b, 0, 0)),
            const((1, d)), const((1, d)),
            const((d, d)), const((d, 2 * d)), const((d, d)),
            const((1, d)),
            const((d, LANES)), const((d, LANES)), const((1, LANES)),
        ],
        out_specs=[
            pl.BlockSpec((1, 2 * TOK_TILE, d), lambda b, t: (b, t, 0)),
            pl.BlockSpec((1, 2 * TOK_TILE, d), lambda b, t: (b, t, 0)),
            pl.BlockSpec((2 * TOK_TILE, LANES), lambda b, t: (b * n_t + t, 0)),
            pl.BlockSpec((2, META_ROWS, TOK_TILE), lambda b, t: (b * n_t + t, 0, 0)),
            pl.BlockSpec((2 * SUBLANES, LANES), lambda b, t: (b * n_t + t, 0)),
        ],
        out_shape=[
            jax.ShapeDtypeStruct((bsz, seq, d), F32),
            jax.ShapeDtypeStruct((bsz, seq, d), BF16),
            jax.ShapeDtypeStruct((bsz * seq, LANES), F32),
            jax.ShapeDtypeStruct((bsz * seq // TOK_TILE, META_ROWS, TOK_TILE), F32),
            jax.ShapeDtypeStruct((bsz * seq // TOK_TILE * SUBLANES, LANES), F32),
        ],
        scratch_shapes=[
            pltpu.VMEM((mlen, d), BF16),
            pltpu.VMEM((mlen, d), BF16),
        ],
        compiler_params=pltpu.CompilerParams(
            dimension_semantics=("arbitrary", "arbitrary"), vmem_limit_bytes=VMEM_LIMIT_BYTES),
        name="xattn_router",
    )(x1, mem, norm_xa_w, norm_mem_w, w_xq, w_xkv, w_xo, norm_moe_w, wr_hi, wr_lo, br_p)


def _piece_copy(src_ref, src_row, dst_ref, dst_row, sem):
    return pltpu.make_async_copy(src_ref.at[pl.ds(pl.multiple_of(src_row, ROW_ALIGN), ROW_ALIGN)],
                                 dst_ref.at[pl.ds(pl.multiple_of(dst_row, ROW_ALIGN), ROW_ALIGN)], sem)


def _group_copies(grp_ref, tile, make_copy):
    def per_expert(e, carry):
        src = grp_ref[tile, 0, e]
        dst = grp_ref[tile, 1, e]
        n = grp_ref[tile, 2, e]
        def classes(src, dst, sizes):
            for i, size in enumerate(sizes):
                @pl.when((n & size) != 0)
                def _(src=src, dst=dst, size=size, queue=i % 2):
                    make_copy(pl.multiple_of(src, ROW_ALIGN), pl.multiple_of(dst, ROW_ALIGN),
                              size * ROW_ALIGN).start(priority=queue)
                step = (n & size) * ROW_ALIGN
                src, dst = src + step, dst + step

        sizes = [1 << k for k in range((TOK_TILE // ROW_ALIGN).bit_length() - 1, -1, -1)]
        large = [s for s in sizes if s >= 8]
        small = [s for s in sizes if s < 8]

        @pl.when(n >= large[-1])
        def _():
            classes(src, dst, large)

        skip = (n - (n & (large[-1] - 1))) * ROW_ALIGN
        classes(src + skip, dst + skip, small)
        return carry
    lax.fori_loop(0, N_EXPERTS, per_expert, 0)


def _wait_pieces(src_ref, dst_ref, n, sem):
    size = MAX_PIECES
    while size & (size - 1):
        size &= size - 1
    while size >= 1:
        @pl.when((n & size) != 0)
        def _(size=size):
            pltpu.make_async_copy(src_ref.at[pl.ds(0, size * ROW_ALIGN)], dst_ref.at[pl.ds(0, size * ROW_ALIGN)], sem).wait()
        size //= 2


def _block_copy(src_ref, dst_ref, dst_block, sem):
    return pltpu.make_async_copy(src_ref, dst_ref.at[pl.ds(pl.multiple_of(dst_block * MOE_ROWS, MOE_ROWS), MOE_ROWS)], sem)


def _dispatch_kernel(grp_ref, np_ref, tail_ref, cap_ref, mt_ref, hm_ref, xs_ref, stage_ref, zero_ref, sem, sem_z):
    s = pl.program_id(0)
    last = pl.num_programs(0) - 1
    d = hm_ref.shape[1]

    def spare_blocks(fn):
        def body(i, c):
            fn(_block_copy(zero_ref, xs_ref, cap_ref[0] + i, sem_z))
            return c
        lax.fori_loop(0, cap_ref[1], body, 0)

    @pl.when(s == 0)
    def _():
        zero_ref[...] = jnp.zeros_like(zero_ref)
        spare_blocks(lambda cp: cp.start())

        def per_expert(e, carry):
            n = tail_ref[1, e]
            off = tail_ref[0, e]

            def start(i, c):
                _piece_copy(zero_ref, 0, xs_ref, off + i * ROW_ALIGN, sem.at[0]).start()
                return c

            def wait(i, c):
                _piece_copy(zero_ref, 0, xs_ref, 0, sem.at[0]).wait()
                return c

            lax.fori_loop(0, n, start, 0)
            lax.fori_loop(0, n, wait, 0)
            return carry

        lax.fori_loop(0, N_EXPERTS, per_expert, 0)

    def drain(u, tile):
        _wait_pieces(stage_ref.at[u], xs_ref, np_ref[tile], sem.at[u])

    for u in range(MOE_TILES):
        tile = MOE_TILES * s + u

        @pl.when(s > 0)
        def _():
            drain(u, tile - MOE_TILES)

        x16 = hm_ref[u * TOK_TILE:(u + 1) * TOK_TILE, :]
        tr = mt_ref[u, TOP_K:2 * TOP_K, :].astype(jnp.int32)
        g = mt_ref[u, 2 * TOP_K:3 * TOP_K, :]
        for c in range(STAGE_ROWS // TOK_TILE):
            rows = slice(c * TOK_TILE, (c + 1) * TOK_TILE)
            rid = lax.broadcasted_iota(jnp.int32, (TOK_TILE, TOK_TILE), 0) + c * TOK_TILE
            hot = [rid == tr[k:k + 1, :] for k in range(TOP_K)]
            sel = jnp.where(hot[0] | hot[1] | hot[2] | hot[3], 1.0, 0.0).astype(BF16)
            stage_ref[u, rows, 0:d] = _dot(sel, x16)
            gsum = jnp.zeros((TOK_TILE, 1), F32)
            for k in range(TOP_K):
                gsum = gsum + jnp.sum(jnp.where(hot[k], g[k:k + 1, :], 0.0), axis=-1, keepdims=True)
            stage_ref[u, rows, d:d + LANES] = jnp.broadcast_to(gsum, (TOK_TILE, LANES))

        _group_copies(grp_ref, tile, lambda src, dst, rows: pltpu.make_async_copy(
            stage_ref.at[u, pl.ds(src, rows)], xs_ref.at[pl.ds(dst, rows)], sem.at[u]))

    @pl.when(s == last)
    def _():
        for u in range(MOE_TILES):
            drain(u, MOE_TILES * s + u)
        spare_blocks(lambda cp: cp.wait())


def _dispatch(hm, groups, n_pieces, tail_info, spare, meta_t, n_rows):
    n_tok, d = hm.shape
    n_tiles = n_tok // TOK_TILE
    assert n_tiles % MOE_TILES == 0
    smem = pl.BlockSpec(memory_space=pltpu.SMEM)
    return pl.pallas_call(
        _dispatch_kernel,
        grid=(n_tiles // MOE_TILES,),
        in_specs=[
            smem, smem, smem, smem,
            pl.BlockSpec((MOE_TILES, META_ROWS, TOK_TILE), lambda s: (s, 0, 0)),
            pl.BlockSpec((MOE_TILES * TOK_TILE, d), lambda s: (s, 0)),
        ],
        out_specs=pl.BlockSpec(memory_space=pl.ANY),
        out_shape=jax.ShapeDtypeStruct((n_rows, d + LANES), F32),
        scratch_shapes=[pltpu.VMEM((MOE_TILES, STAGE_ROWS, d + LANES), F32), pltpu.VMEM((MOE_ROWS, d + LANES), F32),
                        pltpu.SemaphoreType.DMA((MOE_TILES,)), pltpu.SemaphoreType.DMA],
        compiler_params=pltpu.CompilerParams(
            dimension_semantics=("arbitrary",), vmem_limit_bytes=VMEM_LIMIT_BYTES),
        name="moe_dispatch",
    )(groups, n_pieces, tail_info, spare, meta_t, hm)


def _expert_kernel(be_ref, nu_ref, first_ref, slot_ref, nxt_ref, xs_ref, wgu_hbm, bgu_ref, wd_hbm,
                   bd_ref, ys_ref, wgu32_ref, wd32_ref, wgu16_ref, wd16_ref, sem):
    i = pl.program_id(0)
    f, d = wd16_ref.shape

    def weight_copies(e, slot):
        return (pltpu.make_async_copy(wgu_hbm.at[e], wgu32_ref.at[slot], sem.at[0, slot]),
                pltpu.make_async_copy(wd_hbm.at[e], wd32_ref.at[slot], sem.at[1, slot]))

    @pl.when(i == 0)
    def _():
        for cp in weight_copies(be_ref[0], 0):
            cp.start()

    @pl.when(first_ref[i] == 1)
    def _():
        slot = slot_ref[i]
        for cp in weight_copies(be_ref[i], slot):
            cp.wait()

        @pl.when(nxt_ref[i] >= 0)
        def _():
            for cp in weight_copies(nxt_ref[i], 1 - slot):
                cp.start()

        wgu16_ref[...] = wgu32_ref[slot].astype(BF16)
        wd16_ref[...] = wd32_ref[slot].astype(BF16)

    @pl.when(i < nu_ref[0])
    def _():
        gu = _dot(xs_ref[:, 0:d].astype(BF16), wgu16_ref[...]) + bgu_ref[0]
        gate = jnp.minimum(gu[:, :f], SWIGLU_LIMIT)
        up = jnp.clip(gu[:, f:], -SWIGLU_LIMIT, SWIGLU_LIMIT)
        act = (up + 1.0) * gate * jax.nn.sigmoid(SWIGLU_ALPHA * gate)
        ys_ref[...] = (_dot(act.astype(BF16), wd16_ref[...]) + bd_ref[0]) * xs_ref[:, d:d + 1]

    @pl.when(i >= nu_ref[0])
    def _():
        ys_ref[...] = jnp.zeros_like(ys_ref)


def _expert_mlp(xs, block_e, n_used, first, slot, nxt, w_gu, b_gu, w_d, b_d):
    n_rows = xs.shape[0]
    d = w_d.shape[2]
    n_blocks = n_rows // MOE_ROWS
    f = w_d.shape[1]
    row_map = lambda i, be, nu, *_: (jnp.minimum(i, nu[0] - 1), 0)
    exp_map = lambda i, be, *_: (be[i], 0, 0)
    return pl.pallas_call(
        _expert_kernel,
        grid_spec=pltpu.PrefetchScalarGridSpec(
            num_scalar_prefetch=5,
            grid=(n_blocks,),
            in_specs=[
                pl.BlockSpec((MOE_ROWS, d + LANES), row_map),
                pl.BlockSpec(memory_space=pl.ANY),
                pl.BlockSpec((1, 1, 2 * f), exp_map),
                pl.BlockSpec(memory_space=pl.ANY),
                pl.BlockSpec((1, 1, d), exp_map),
            ],
            out_specs=pl.BlockSpec((MOE_ROWS, d), lambda i, *_: (i, 0)),
            scratch_shapes=[
                pltpu.VMEM((2, d, 2 * f), F32), pltpu.VMEM((2, f, d), F32),
                pltpu.VMEM((d, 2 * f), BF16), pltpu.VMEM((f, d), BF16),
                pltpu.SemaphoreType.DMA((2, 2)),
            ],
        ),
        out_shape=jax.ShapeDtypeStruct((n_rows, d), F32),
        compiler_params=pltpu.CompilerParams(
            dimension_semantics=("arbitrary",), vmem_limit_bytes=VMEM_LIMIT_BYTES),
        name="moe_experts",
    )(block_e, n_used, first, slot, nxt, xs, w_gu, b_gu, w_d, b_d)


def _combine_kernel(grp_ref, np_ref, x2_ref, meta_ref, nfw_ref, ys_ref, out_ref, ybuf_ref, sem):
    s = pl.program_id(0)
    last = pl.num_programs(0) - 1

    def fetch(u, tile):
        _group_copies(grp_ref, tile, lambda src, dst, rows: pltpu.make_async_copy(
            ys_ref.at[pl.ds(dst, rows)], ybuf_ref.at[u, pl.ds(src, rows)], sem.at[u]))

    def finish(u, tile):
        _wait_pieces(ys_ref, ybuf_ref.at[u], np_ref[tile], sem.at[u])

        rows = slice(u * TOK_TILE, (u + 1) * TOK_TILE)
        tr = meta_ref[rows, TOP_K:2 * TOP_K].astype(jnp.int32)
        acc = x2_ref[rows, :]
        for c in range(STAGE_ROWS // TOK_TILE):
            cid = lax.broadcasted_iota(jnp.int32, (TOK_TILE, TOK_TILE), 1) + c * TOK_TILE
            hot = [cid == tr[:, k:k + 1] for k in range(TOP_K)]
            sel = jnp.where(hot[0] | hot[1] | hot[2] | hot[3], 1.0, 0.0).astype(BF16)
            acc = acc + _dot(sel, ybuf_ref[u, c * TOK_TILE:(c + 1) * TOK_TILE, :].astype(BF16))
        out_ref[rows, :] = _rms(acc, nfw_ref[...])

    @pl.when(s == 0)
    def _():
        ybuf_ref[...] = jnp.zeros_like(ybuf_ref)
        for u in range(MOE_TILES - 1):
            fetch(u, u)

    for u in range(MOE_TILES):
        tile = MOE_TILES * s + u
        ahead = u + MOE_TILES - 1
        if ahead < MOE_TILES:
            fetch(ahead, MOE_TILES * s + ahead)
        else:
            @pl.when(s < last)
            def _():
                fetch(ahead - MOE_TILES, MOE_TILES * s + ahead)
        finish(u, tile)


def _combine(x2, meta, groups, n_pieces, ys, norm_final_w):
    n_tok, d = x2.shape
    n_tiles = n_tok // TOK_TILE
    assert n_tiles % MOE_TILES == 0
    smem = pl.BlockSpec(memory_space=pltpu.SMEM)
    return pl.pallas_call(
        _combine_kernel,
        grid=(n_tiles // MOE_TILES,),
        in_specs=[
            smem, smem,
            pl.BlockSpec((MOE_TILES * TOK_TILE, d), lambda s: (s, 0)),
            pl.BlockSpec((MOE_TILES * TOK_TILE, LANES), lambda s: (s, 0)),
            pl.BlockSpec((1, d), lambda s: (0, 0)),
            pl.BlockSpec(memory_space=pl.ANY),
        ],
        out_specs=pl.BlockSpec((MOE_TILES * TOK_TILE, d), lambda s: (s, 0)),
        out_shape=jax.ShapeDtypeStruct((n_tok, d), F32),
        scratch_shapes=[pltpu.VMEM((MOE_TILES, STAGE_ROWS, d), F32), pltpu.SemaphoreType.DMA((MOE_TILES,))],
        compiler_params=pltpu.CompilerParams(
            dimension_semantics=("arbitrary",), vmem_limit_bytes=VMEM_LIMIT_BYTES),
        name="moe_combine",
    )(groups, n_pieces, x2, meta, norm_final_w, ys)


def _pad_heads(w, n_heads, width):
    lead = w.shape[:-1]
    w = w.reshape(lead + (n_heads, width))
    w = jnp.pad(w, [(0, 0)] * len(lead) + [(0, 0), (0, HEAD_W - width)])
    return w.reshape(lead + (n_heads * HEAD_W,))


def _round_up(x, m):
    return (x + m - 1) // m * m


def _routing_tables(cnt, n_tiles):
    i32 = jnp.int32
    counts = cnt.reshape(n_tiles, SUBLANES, LANES)[:, 0, :N_EXPERTS].astype(i32)
    cnt8 = _round_up(counts, ROW_ALIGN)
    lend = jnp.cumsum(cnt8, axis=1)
    lstart = lend - cnt8
    tot = jnp.sum(cnt8, axis=0)
    padded = _round_up(tot, MOE_ROWS)
    pends = jnp.cumsum(padded)
    pstarts = pends - padded
    goff = pstarts[None, :] + jnp.cumsum(cnt8, axis=0) - cnt8

    groups = jnp.stack([lstart, goff, cnt8 // ROW_ALIGN], axis=1).astype(i32)
    n_pieces = (lend[:, -1] // ROW_ALIGN).astype(i32)

    max_rows = n_tiles * (TOK_TILE * TOP_K + N_EXPERTS * (ROW_ALIGN - 1)) + N_EXPERTS * (MOE_ROWS - ROW_ALIGN)
    n_blocks = -(-max_rows // MOE_ROWS)
    n_used = (pends[-1] // MOE_ROWS).astype(i32)
    blk = jnp.arange(n_blocks, dtype=i32)
    block_e = jnp.minimum(jnp.sum(blk[:, None] * MOE_ROWS >= pends[None, :], axis=-1), N_EXPERTS - 1).astype(i32)
    block_e = jnp.where(blk < n_used, block_e, block_e[jnp.maximum(n_used - 1, 0)])
    first = ((blk < n_used) & ((blk == 0) | (block_e != jnp.roll(block_e, 1)))).astype(i32)
    has_rows = padded > 0
    slot_e = (jnp.cumsum(has_rows.astype(i32)) - 1) % 2
    later = jnp.arange(N_EXPERTS, dtype=i32)[None, :] > jnp.arange(N_EXPERTS, dtype=i32)[:, None]
    nxt_e = jnp.min(jnp.where(later & has_rows[None, :], jnp.arange(N_EXPERTS, dtype=i32)[None, :], N_EXPERTS), axis=1)
    nxt_e = jnp.where(nxt_e < N_EXPERTS, nxt_e, -1).astype(i32)
    tail_info = jnp.stack([pstarts + tot, (padded - tot) // ROW_ALIGN]).astype(i32)
    spare = jnp.stack([n_used, n_blocks - n_used]).astype(i32)
    of_block = block_e[:, None] == jnp.arange(N_EXPERTS, dtype=i32)
    slot_b = jnp.sum(jnp.where(of_block, slot_e[None, :], 0), axis=1).astype(i32)
    nxt_b = jnp.sum(jnp.where(of_block, nxt_e[None, :], 0), axis=1).astype(i32)
    return (groups, n_pieces, tail_info, spare,
            (block_e, n_used.reshape(1), first, slot_b, nxt_b), n_blocks * MOE_ROWS)


def kernel(x, mem, norm_mix_w, w_in, hg_lb_logits, hg_onorm_w, gla_w_gk2, gla_b_gk, gla_onorm_w, w_out, norm_xa_w, norm_mem_w, w_xq, w_xkv, w_xo, norm_moe_w, w_router, b_router, w_gate_up, b_gate_up, w_down, b_down, norm_final_w):
    assert w_in.shape[0] == 1, "single-layer block"
    bsz, seq, d = x.shape
    n_tok = bsz * seq
    gk = GLA_HEADS * GLA_KDIM

    wi = w_in[0]
    w_in_p = jnp.concatenate([
        wi[:, :4 * _W],
        _pad_heads(wi[:, 4 * _W:4 * _W + gk], GLA_HEADS, GLA_KDIM),
        _pad_heads(wi[:, 4 * _W + gk:4 * _W + 2 * gk], GLA_HEADS, GLA_KDIM),
        wi[:, 4 * _W + 2 * gk:4 * _W + 2 * gk + 2 * _W],
        jnp.pad(wi[:, 4 * _W + 2 * gk + 2 * _W:], ((0, 0), (0, LANES - GLA_RANK))),
    ], axis=1).astype(BF16)
    w_gk2_p = jnp.pad(_pad_heads(gla_w_gk2[0], GLA_HEADS, GLA_KDIM),
                      ((0, LANES - GLA_RANK), (0, 0))).astype(BF16)
    b_gk_p = _pad_heads(gla_b_gk, GLA_HEADS, GLA_KDIM)
    wr = jnp.pad(w_router[0], ((0, 0), (0, LANES - N_EXPERTS)))
    wr_hi = wr.astype(BF16)
    wr_lo = (wr - wr_hi.astype(F32)).astype(BF16)
    br_p = jnp.pad(b_router, ((0, 0), (0, LANES - N_EXPERTS)), constant_values=-1e30)

    x1 = _token_mix(x, norm_mix_w, w_in_p, hg_lb_logits, hg_onorm_w, w_gk2_p, b_gk_p, gla_onorm_w,
                    w_out[0].astype(BF16))
    x2, hm, meta, meta_t, cnt = _xattn_router(x1, mem, norm_xa_w, norm_mem_w, w_xq[0].astype(BF16),
                                              w_xkv[0].astype(BF16), w_xo[0].astype(BF16), norm_moe_w,
                                              wr_hi, wr_lo, br_p)

    n_tiles = n_tok // TOK_TILE
    groups, n_pieces, tail_info, spare, block_tables, n_rows = _routing_tables(cnt, n_tiles)

    xs = _dispatch(hm.reshape(n_tok, d), groups, n_pieces, tail_info, spare, meta_t, n_rows)
    ys = _expert_mlp(xs, *block_tables, w_gate_up[0], b_gate_up[0][:, None, :],
                     w_down[0], b_down[0][:, None, :])
    out = _combine(x2.reshape(n_tok, d), meta, groups, n_pieces, ys, norm_final_w[None, :])
    return out.reshape(bsz, seq, d)
```
